```python
import jax, jax.numpy as jnp
from jax import lax
import numpy as np

D_MODEL = 1024
BATCH = 4
SEQ = 4096
DEPTH = 4
DEC_BATCH = 32
DEC_SEQ = 4
PAST_LEN = 8192
PAGE_SIZE = 128

DH = 64
H_A = 8
HKV_A = 2
CMP_LEN = 32
CMP_STRIDE = 16
SEL_BLOCK = 64
N_SEL = 16
WINDOW = 512
H_B = 8
HKV_B = 2
H_IDX = 8
D_IDX = 64
DSA_TOPK = 256
H_C = 8
D_C = H_C * DH
CONV_W = 4
CHUNK = 64
H_D = 8
HKV_D = 2
MOBA_BLOCK = 256
MOBA_TOPK = 3
D_FF = 4 * D_MODEL
Q_BLOCK = 128
N_EVEN = (DEPTH + 1) // 2
N_ODD = DEPTH // 2
EVEN_COLS = (H_A * DH, 6 * HKV_A * DH, 3 * H_A, H_B * DH, 2 * HKV_B * DH, H_IDX * D_IDX, D_IDX, H_IDX)
ODD_COLS = (D_C, H_C * DH, H_C * DH, H_C, H_C, H_D * DH, 2 * HKV_D * DH)
W_EVEN = sum(EVEN_COLS)
W_ODD = sum(ODD_COLS)
D_MIX_EVEN = (H_A + H_B) * DH
D_MIX_ODD = (H_C + H_D) * DH
EPS = 1e-6
NEG = -1e30
TINY = 1e-30
FORCE = 1e4

kernel_name = 'hybrid_nsa_dsa_mlstm_moba_step'


def rmsnorm(x, g):
    xf = x.astype(jnp.float32)
    y = xf * lax.rsqrt(jnp.mean(xf * xf, axis=-1, keepdims=True) + EPS)
    return (y * g.astype(jnp.float32)).astype(x.dtype)


def split_cols(z, sizes):
    return jnp.split(z, np.cumsum(sizes)[:-1].tolist(), axis=-1)


def masked_softmax(s, mask):
    s = jnp.where(mask, s.astype(jnp.float32), NEG)
    m = jnp.max(s, axis=-1, keepdims=True)
    e = jnp.where(mask, jnp.exp(s - m), 0.0)
    return e / jnp.maximum(jnp.sum(e, axis=-1, keepdims=True), TINY)


def map_query_blocks(fn, q_args, q_pos):
    T = q_pos.shape[0]
    if T <= Q_BLOCK or T % Q_BLOCK:
        return fn(q_args, q_pos)
    nb = T // Q_BLOCK
    def to_blocks(a):
        return jnp.moveaxis(a.reshape(a.shape[0], nb, Q_BLOCK, *a.shape[2:]), 1, 0)
    out = lax.map(lambda args: fn(*args), (tuple(to_blocks(a) for a in q_args), q_pos.reshape(nb, Q_BLOCK)))
    out = jnp.moveaxis(out, 0, 1)
    return out.reshape(out.shape[0], T, *out.shape[3:])


def gather_pages(pool, page_table):
    g = pool[page_table]
    return g.reshape(page_table.shape[0], -1, *pool.shape[2:])


def nsa_compress(kv, pe, w):
    B, L = kv.shape[:2]
    n16 = -(-L // CMP_STRIDE)
    kv = jnp.pad(kv, ((0, 0), (0, n16 * CMP_STRIDE - L), (0, 0), (0, 0)))
    r = kv.reshape(B, n16, CMP_STRIDE, *kv.shape[2:])
    blocks = jnp.concatenate([r[:, :-1], r[:, 1:]], axis=2)
    return jnp.einsum('bnlkd,lde->bnke', blocks + pe[:, None, :], w)


def nsa_cmp_sel(q, kv4, cmp_pe, cmp_w, q_pos):
    B, Lk = kv4.shape[:2]
    kc = nsa_compress(kv4[:, :, 0], cmp_pe[0], cmp_w[0])
    vc = nsa_compress(kv4[:, :, 1], cmp_pe[1], cmp_w[1])
    nc = kc.shape[1]
    cmp_end = jnp.arange(nc) * CMP_STRIDE + CMP_LEN - 1
    ns = -(-Lk // SEL_BLOCK)
    cs = jnp.arange(nc)[:, None] * CMP_STRIDE
    ss = jnp.arange(ns)[None, :] * SEL_BLOCK
    overlap = ((cs < ss + SEL_BLOCK) & (cs + CMP_LEN > ss)).astype(jnp.float32)
    def padsel(a):
        a = jnp.pad(a, ((0, 0), (0, ns * SEL_BLOCK - Lk), (0, 0), (0, 0)))
        return a.reshape(B, ns, SEL_BLOCK, HKV_A, DH)
    ks, vs = padsel(kv4[:, :, 2]), padsel(kv4[:, :, 3])
    n_sel = min(N_SEL, ns)
    bi = jnp.arange(B)[:, None, None, None]
    hi = jnp.arange(HKV_A)[None, None, :, None]
    j = jnp.arange(ns)
    scale = DH ** -0.5
    def block(args, pos):
        (qb,) = args
        Tb, G = qb.shape[1], qb.shape[3]
        s = jnp.einsum('btkgd,bnkd->btkgn', qb, kc) * scale
        p = masked_softmax(s, (cmp_end[None, :] <= pos[:, None])[None, :, None, None, :])
        o_cmp = jnp.einsum('btkgn,bnkd->btkgd', p.astype(vc.dtype), vc)
        imp = jnp.einsum('btkgn,nj->btkj', p, overlap)
        blk = pos // SEL_BLOCK
        forced = ((j[None, :] == 0) | (j[None, :] == blk[:, None]) | (j[None, :] == blk[:, None] - 1))[None, :, None, :]
        adm = (j[None, :] <= blk[:, None])[None, :, None, :]
        score = jnp.where(adm, imp + FORCE * forced, NEG)
        val, idx = lax.top_k(score, n_sel)
        kg = ks[bi, idx, :, hi]
        vg = vs[bi, idx, :, hi]
        s2 = jnp.einsum('btkgd,btkjpd->btkgjp', qb, kg) * scale
        tok = idx[..., None] * SEL_BLOCK + jnp.arange(SEL_BLOCK)
        m2 = (val > NEG / 2)[..., None] & (tok <= pos[None, :, None, None, None])
        p2 = masked_softmax(s2.reshape(B, Tb, HKV_A, G, n_sel * SEL_BLOCK), m2.reshape(B, Tb, HKV_A, 1, n_sel * SEL_BLOCK))
        o_sel = jnp.einsum('btkgs,btksd->btkgd', p2.astype(vg.dtype), vg.reshape(B, Tb, HKV_A, n_sel * SEL_BLOCK, DH))
        return jnp.stack([o_cmp, o_sel], axis=2)
    return map_query_blocks(block, (q,), q_pos)


def nsa_window(q, kv_ext, q_pos, n_prefix):
    q0 = q_pos[0]
    scale = DH ** -0.5
    def block(args, pos):
        (qb,) = args
        Tb = qb.shape[1]
        band = lax.dynamic_slice_in_dim(kv_ext, pos[0] - q0, n_prefix + Tb, axis=1)
        k_pos = pos[0] - n_prefix + jnp.arange(n_prefix + Tb)
        s = jnp.einsum('btkgd,bskd->btkgs', qb, band[:, :, 0]) * scale
        diff = pos[:, None] - k_pos[None, :]
        mask = (diff >= 0) & (diff < WINDOW) & (k_pos >= 0)[None, :]
        p = masked_softmax(s, mask[None, :, None, None, :])
        return jnp.einsum('btkgs,bskd->btkgd', p.astype(band.dtype), band[:, :, 1])
    return map_query_blocks(block, (q,), q_pos)


def dsa_attention(q, iq, iw, kv, ik, q_pos):
    B, Lk = kv.shape[:2]
    n_top = min(DSA_TOPK, Lk // 4)
    bi = jnp.arange(B)[:, None, None]
    k_pos = jnp.arange(Lk)
    scale = DH ** -0.5
    def block(args, pos):
        qb, iqb, iwb = args
        score = jax.nn.relu(jnp.einsum('bthd,bsd->bths', iqb, ik).astype(jnp.float32))
        score = jnp.einsum('bths,bth->bts', score, iwb.astype(jnp.float32))
        score = jnp.where(k_pos[None, None, :] <= pos[None, :, None], score, NEG)
        val, idx = lax.top_k(score, n_top)
        kg = kv[bi, idx, 0]
        vg = kv[bi, idx, 1]
        s = jnp.einsum('btkgd,btskd->btkgs', qb, kg) * scale
        p = masked_softmax(s, (val > NEG / 2)[:, :, None, None, :])
        return jnp.einsum('btkgs,btskd->btkgd', p.astype(vg.dtype), vg)
    return map_query_blocks(block, (q, iq, iw), q_pos)


def moba_attention(q, kv, q_pos):
    B, Lk = kv.shape[:2]
    nblk = -(-Lk // MOBA_BLOCK)
    kvb = jnp.pad(kv, ((0, 0), (0, nblk * MOBA_BLOCK - Lk), (0, 0), (0, 0), (0, 0)))
    kvb = kvb.reshape(B, nblk, MOBA_BLOCK, 2, HKV_D, DH)
    kb, vb = kvb[:, :, :, 0], kvb[:, :, :, 1]
    kmean = jnp.mean(kb.astype(jnp.float32), axis=2)
    n_top = min(MOBA_TOPK, nblk - 1)
    bi = jnp.arange(B)[:, None, None, None, None]
    hi = jnp.arange(HKV_D)[None, None, :, None, None]
    scale = DH ** -0.5
    def block(args, pos):
        (qb,) = args
        Tb, G = qb.shape[1], qb.shape[3]
        own = pos // MOBA_BLOCK
        own_idx = jnp.broadcast_to(own[None, :, None, None, None], (B, Tb, HKV_D, G, 1))
        if n_top > 0:
            s = jnp.einsum('btkgd,bnkd->btkgn', qb.astype(jnp.float32), kmean)
            past = jnp.arange(nblk)[None, :] < own[:, None]
            s = jnp.where(past[None, :, None, None, :], s, NEG)
            val, top = lax.top_k(s, n_top)
            idx = jnp.concatenate([top, own_idx], axis=-1)
            ok = jnp.concatenate([val > NEG / 2, jnp.ones(own_idx.shape, bool)], axis=-1)
        else:
            idx = own_idx
            ok = jnp.ones(own_idx.shape, bool)
        n = idx.shape[-1]
        kg = kb[bi, idx, :, hi]
        vg = vb[bi, idx, :, hi]
        s2 = jnp.einsum('btkgd,btkgnpd->btkgnp', qb, kg) * scale
        tok = idx[..., None] * MOBA_BLOCK + jnp.arange(MOBA_BLOCK)
        m = ok[..., None] & (tok <= pos[None, :, None, None, None, None])
        p = masked_softmax(s2.reshape(B, Tb, HKV_D, G, n * MOBA_BLOCK), m.reshape(B, Tb, HKV_D, G, n * MOBA_BLOCK))
        return jnp.einsum('btkgs,btkgsd->btkgd', p.astype(vg.dtype), vg.reshape(B, Tb, HKV_D, G, n * MOBA_BLOCK, DH))
    return map_query_blocks(block, (q,), q_pos)


def mlstm_chunk(carry, xs):
    C, n, m = carry
    q, k, v, ig, lf = xs
    L = q.shape[1]
    b = jnp.cumsum(lf, axis=1)
    dlog = b[:, :, None, :] - b[:, None, :, :] + ig[:, None, :, :]
    causal = jnp.tril(jnp.ones((L, L), bool))[None, :, :, None]
    dlog = jnp.where(causal, dlog, NEG)
    inter = b + m[:, None, :]
    m_t = jnp.maximum(inter, jnp.max(dlog, axis=2))
    dw = jnp.exp(dlog - m_t[:, :, None, :])
    iw = jnp.exp(inter - m_t)
    qk = jnp.einsum('bthd,bshd->btsh', q, k) * dw
    num = iw[..., None] * jnp.einsum('bhvd,bthd->bthv', C, q) + jnp.einsum('btsh,bshv->bthv', qk, v)
    den = iw * jnp.einsum('bhd,bthd->bth', n, q) + jnp.sum(qk, axis=2)
    h = num / jnp.maximum(jnp.abs(den), jnp.exp(-m_t))[..., None]
    m_new = m_t[:, -1]
    decay = jnp.exp(b[:, -1] + m - m_new)
    ws = jnp.exp(b[:, -1:, :] - b + ig - m_new[:, None, :])
    C_new = decay[..., None, None] * C + jnp.einsum('bsh,bshv,bshd->bhvd', ws, v, k)
    n_new = decay[..., None] * n + jnp.einsum('bsh,bshd->bhd', ws, k)
    return (C_new, n_new, m_new), h


def mlstm_scan(q, k, v, ig, lf, C, n, m):
    B, T = q.shape[:2]
    L = CHUNK if T % CHUNK == 0 else T
    nc = T // L
    def to_chunks(a):
        return jnp.moveaxis(a.reshape(B, nc, L, *a.shape[2:]), 1, 0)
    (C, n, m), h = lax.scan(mlstm_chunk, (C, n, m), tuple(to_chunks(a) for a in (q, k, v, ig, lf)))
    h = jnp.moveaxis(h, 0, 1).reshape(B, T, H_C, DH)
    return h, C, n, m


def even_mixer(h, w_in, b_in, w_out, cmp_pe, cmp_w, past, start):
    B, T, _ = h.shape
    q_pos = start + jnp.arange(T)
    a_q, a_kv, a_g, b_q, b_kv, b_iq, b_ik, b_iw = split_cols(h @ w_in + b_in, EVEN_COLS)
    a_q = a_q.reshape(B, T, HKV_A, H_A // HKV_A, DH)
    a_kv = a_kv.reshape(B, T, 6, HKV_A, DH)
    nsa_rows, win_rows = a_kv[:, :, :4], a_kv[:, :, 4:]
    b_q = b_q.reshape(B, T, HKV_B, H_B // HKV_B, DH)
    b_kv = b_kv.reshape(B, T, 2, HKV_B, DH)
    b_iq = b_iq.reshape(B, T, H_IDX, D_IDX)
    if past is None:
        nsa_all, dsa_all, idx_all = nsa_rows, b_kv, b_ik
        win_ext = jnp.concatenate([jnp.zeros((B, WINDOW) + win_rows.shape[2:], win_rows.dtype), win_rows], axis=1)
        n_prefix, n_keep = WINDOW, min(WINDOW, T)
    else:
        nsa_past, win_buf, dsa_past, idx_past = past
        nsa_all = jnp.concatenate([nsa_past.astype(nsa_rows.dtype), nsa_rows], axis=1)
        dsa_all = jnp.concatenate([dsa_past.astype(b_kv.dtype), b_kv], axis=1)
        idx_all = jnp.concatenate([idx_past.astype(b_ik.dtype), b_ik], axis=1)
        win_ext = jnp.concatenate([win_buf.astype(win_rows.dtype), win_rows], axis=1)
        n_prefix = n_keep = win_buf.shape[1]
    gates = jax.nn.sigmoid(a_g.astype(jnp.float32)).reshape(B, T, 3, HKV_A, H_A // HKV_A)[..., None]
    o_cs = nsa_cmp_sel(a_q, nsa_all, cmp_pe, cmp_w, q_pos)
    o_win = nsa_window(a_q, win_ext, q_pos, n_prefix)
    o_a = gates[:, :, 0] * o_cs[:, :, 0] + gates[:, :, 1] * o_cs[:, :, 1] + gates[:, :, 2] * o_win
    o_b = dsa_attention(b_q, b_iq, b_iw, dsa_all, idx_all, q_pos)
    y = jnp.concatenate([o_a.reshape(B, T, -1).astype(h.dtype), o_b.reshape(B, T, -1).astype(h.dtype)], axis=-1) @ w_out
    return y, (nsa_rows, win_ext[:, -n_keep:], b_kv, b_ik)


def odd_mixer(h, w_in, b_in, w_out, conv_w, conv_b, wq, wk, f_bias, norm_g, past, start):
    B, T, _ = h.shape
    q_pos = start + jnp.arange(T)
    u, c_v, c_o, c_i, c_f, d_q, d_kv = split_cols(h @ w_in + b_in, ODD_COLS)
    d_kv = d_kv.reshape(B, T, 2, HKV_D, DH)
    if past is None:
        c0 = jnp.zeros((B, H_C, DH, DH), jnp.float32)
        n0 = jnp.zeros((B, H_C, DH), jnp.float32)
        m0 = jnp.zeros((B, H_C), jnp.float32)
        buf = jnp.zeros((B, CONV_W - 1, D_C), u.dtype)
        moba_all = d_kv
    else:
        c0, n0, m0, buf, moba_past = past
        moba_all = jnp.concatenate([moba_past.astype(d_kv.dtype), d_kv], axis=1)
    ext = jnp.concatenate([buf.astype(u.dtype), u], axis=1)
    conv = sum(ext[:, j:j + T] * conv_w[j] for j in range(CONV_W)) + conv_b
    uc = jax.nn.silu(conv).reshape(B, T, H_C, DH)
    q = jnp.einsum('bthd,hde->bthe', uc, wq).astype(jnp.float32)
    k = (jnp.einsum('bthd,hde->bthe', uc, wk) * DH ** -0.5).astype(jnp.float32)
    v = c_v.reshape(B, T, H_C, DH).astype(jnp.float32)
    ig = c_i.astype(jnp.float32)
    lf = jax.nn.log_sigmoid((c_f + f_bias).astype(jnp.float32))
    hc, c1, n1, m1 = mlstm_scan(q, k, v, ig, lf, c0.astype(jnp.float32), n0.astype(jnp.float32), m0.astype(jnp.float32))
    hc = hc * lax.rsqrt(jnp.mean(hc * hc, axis=-1, keepdims=True) + EPS) * norm_g.astype(jnp.float32)
    hc = jax.nn.sigmoid(c_o.astype(jnp.float32)).reshape(B, T, H_C, DH) * hc
    o_d = moba_attention(d_q.reshape(B, T, HKV_D, H_D // HKV_D, DH), moba_all, q_pos)
    y = jnp.concatenate([hc.reshape(B, T, -1).astype(h.dtype), o_d.reshape(B, T, -1).astype(h.dtype)], axis=-1) @ w_out
    return y, (c1, n1, m1, ext[:, -(CONV_W - 1):], d_kv)


def stack_layers(states):
    return tuple(jnp.stack(a) for a in zip(*states))


def setup_inputs(seed: int = 0) -> dict:
    key = jax.random.key(seed)
    ks = iter(jax.random.split(key, 48))
    def nrm(shape, s=1.0):
        return jax.random.normal(next(ks), shape, jnp.float32) * s
    d = D_MODEL
    n_pages = PAST_LEN // PAGE_SIZE
    n_phys = (5 * DEC_BATCH * n_pages) // 4
    wb = min(WINDOW, PAST_LEN)
    perm = jax.random.permutation(next(ks), n_phys)
    page_table = perm[:DEC_BATCH * n_pages].reshape(DEC_BATCH, n_pages).astype(jnp.int32)
    return {
        'x_prompt': nrm((BATCH, SEQ, d)),
        'x_sample': nrm((DEC_BATCH, DEC_SEQ, d)),
        'cache_nsa_kv': nrm((N_EVEN, n_phys, PAGE_SIZE, 4, HKV_A, DH)),
        'state_nsa_win': nrm((N_EVEN, DEC_BATCH, wb, 2, HKV_A, DH)),
        'cache_dsa_kv': nrm((N_EVEN, n_phys, PAGE_SIZE, 2, HKV_B, DH)),
        'cache_dsa_idx': nrm((N_EVEN, n_phys, PAGE_SIZE, D_IDX)),
        'state_mlstm_c': nrm((N_ODD, DEC_BATCH, H_C, DH, DH), 0.3),
        'state_mlstm_n': nrm((N_ODD, DEC_BATCH, H_C, DH), 0.3),
        'state_mlstm_m': nrm((N_ODD, DEC_BATCH, H_C)),
        'state_mlstm_conv': nrm((N_ODD, DEC_BATCH, CONV_W - 1, D_C)),
        'cache_moba_kv': nrm((N_ODD, n_phys, PAGE_SIZE, 2, HKV_D, DH)),
        'page_table': page_table,
        'c_prompt': nrm((BATCH, d)),
        'c_sample': nrm((DEC_BATCH, d)),
        'ada_w': nrm((DEPTH, d, 6 * d), 0.5 * d ** -0.5),
        'ada_b': nrm((DEPTH, 6 * d), 0.02),
        'norm1_g': 1.0 + nrm((DEPTH, d), 0.05),
        'norm2_g': 1.0 + nrm((DEPTH, d), 0.05),
        'ev_w_in': nrm((N_EVEN, d, W_EVEN), d ** -0.5),
        'ev_b_in': nrm((N_EVEN, W_EVEN), 0.02),
        'ev_w_out': nrm((N_EVEN, D_MIX_EVEN, d), D_MIX_EVEN ** -0.5),
        'nsa_cmp_pe': nrm((N_EVEN, 2, CMP_LEN, DH), 0.5),
        'nsa_cmp_w': nrm((N_EVEN, 2, CMP_LEN, DH, DH), (CMP_LEN * DH) ** -0.5),
        'od_w_in': nrm((N_ODD, d, W_ODD), d ** -0.5),
        'od_b_in': nrm((N_ODD, W_ODD), 0.02),
        'od_w_out': nrm((N_ODD, D_MIX_ODD, d), D_MIX_ODD ** -0.5),
        'ml_conv_w': nrm((N_ODD, CONV_W, D_C), CONV_W ** -0.5),
        'ml_conv_b': nrm((N_ODD, D_C), 0.02),
        'ml_wq': nrm((N_ODD, H_C, DH, DH), DH ** -0.5),
        'ml_wk': nrm((N_ODD, H_C, DH, DH), DH ** -0.5),
        'ml_f_bias': 3.0 + nrm((N_ODD, H_C), 0.1),
        'ml_norm_g': 1.0 + nrm((N_ODD, H_C, DH), 0.05),
        'mlp_w1': nrm((DEPTH, d, D_FF), d ** -0.5),
        'mlp_w2': nrm((DEPTH, D_FF, d), D_FF ** -0.5),
        'final_g': 1.0 + nrm((d,), 0.05),
    }


def reference(x_prompt, x_sample, cache_nsa_kv, state_nsa_win, cache_dsa_kv, cache_dsa_idx,
              state_mlstm_c, state_mlstm_n, state_mlstm_m, state_mlstm_conv, cache_moba_kv,
              page_table, c_prompt, c_sample, ada_w, ada_b, norm1_g, norm2_g,
              ev_w_in, ev_b_in, ev_w_out, nsa_cmp_pe, nsa_cmp_w,
              od_w_in, od_b_in, od_w_out, ml_conv_w, ml_conv_b, ml_wq, ml_wk, ml_f_bias, ml_norm_g,
              mlp_w1, mlp_w2, final_g):
    past_len = page_table.shape[1] * cache_nsa_kv.shape[2]

    def layer_past(l):
        i = l // 2
        if l % 2 == 0:
            return (gather_pages(cache_nsa_kv[i], page_table), state_nsa_win[i],
                    gather_pages(cache_dsa_kv[i], page_table), gather_pages(cache_dsa_idx[i], page_table))
        return (state_mlstm_c[i], state_mlstm_n[i], state_mlstm_m[i], state_mlstm_conv[i],
                gather_pages(cache_moba_kv[i], page_table))

    def run_group(x, c, start, with_past):
        cs = jax.nn.silu(c)
        ev_states, od_states = [], []
        for l in range(DEPTH):
            i = l // 2
            sh1, sc1, g1, sh2, sc2, g2 = jnp.split((cs @ ada_w[l] + ada_b[l])[:, None, :], 6, axis=-1)
            hm = rmsnorm(x, norm1_g[l]) * (1 + sc1) + sh1
            past = layer_past(l) if with_past else None
            if l % 2 == 0:
                y, st = even_mixer(hm, ev_w_in[i], ev_b_in[i], ev_w_out[i], nsa_cmp_pe[i], nsa_cmp_w[i], past, start)
                ev_states.append(st)
            else:
                y, st = odd_mixer(hm, od_w_in[i], od_b_in[i], od_w_out[i], ml_conv_w[i], ml_conv_b[i],
                                  ml_wq[i], ml_wk[i], ml_f_bias[i], ml_norm_g[i], past, start)
                od_states.append(st)
            x = x + g1 * y
            hf = rmsnorm(x, norm2_g[l]) * (1 + sc2) + sh2
            x = x + g2 * (jnp.square(jax.nn.relu(hf @ mlp_w1[l])) @ mlp_w2[l])
        return rmsnorm(x, final_g), stack_layers(ev_states), stack_layers(od_states)

    y_prompt, ev_p, od_p = run_group(x_prompt, c_prompt, 0, False)
    y_sample, ev_s, od_s = run_group(x_sample, c_sample, past_len, True)
    nsa_kv_p, nsa_win_p, dsa_kv_p, dsa_idx_p = ev_p
    nsa_kv_s, nsa_win_s, dsa_kv_s, dsa_idx_s = ev_s
    ml_c_p, ml_n_p, ml_m_p, ml_conv_p, moba_kv_p = od_p
    ml_c_s, ml_n_s, ml_m_s, ml_conv_s, moba_kv_s = od_s
    return (y_prompt, y_sample,
            nsa_kv_p, nsa_kv_s, nsa_win_p, nsa_win_s, dsa_kv_p, dsa_kv_s, dsa_idx_p, dsa_idx_s,
            ml_c_p, ml_c_s, ml_n_p, ml_n_s, ml_m_p, ml_m_s, ml_conv_p, ml_conv_s, moba_kv_p, moba_kv_s)
```

```python
import functools

import numpy as np
import jax
import jax.numpy as jnp
from jax import lax
from jax.experimental import pallas as pl
from jax.experimental.pallas import tpu as pltpu

F32 = jnp.float32
BF16 = jnp.bfloat16
I32 = jnp.int32

DH = 64
N_HEADS = 8
HKV = 2
GROUP = N_HEADS // HKV
CMP_LEN = 32
CMP_STRIDE = 16
SEL_BLOCK = 64
N_SEL = 16
WINDOW = 512
DSA_TOPK = 256
CONV_W = 4
CHUNK = 64
MOBA_BLOCK = 256
MOBA_TOPK = 3
EPS = 1e-6
NEG = -1e30
TINY = 1e-30
FORCE = 1e4
SCALE = DH ** -0.5
LANES = 128
INT_MIN = -2 ** 31

VMEM_LIMIT = 56 * 1024 * 1024


def _cparams(sem):
    return pltpu.CompilerParams(dimension_semantics=sem, vmem_limit_bytes=VMEM_LIMIT)


def _dot(a, b):
    return jnp.dot(a, b, preferred_element_type=F32)


def _dot_nt(a, b):
    return lax.dot_general(a, b, (((1,), (1,)), ((), ())), preferred_element_type=F32)


def _split3(x):
    hi = x.astype(BF16)
    r = x - hi.astype(F32)
    mid = r.astype(BF16)
    lo = (r - mid.astype(F32)).astype(BF16)
    return hi, mid, lo


def _ada_kernel(c_ref, w_ref, b_ref, o_ref):
    c = c_ref[...]
    cs = (c * jax.nn.sigmoid(c)).astype(BF16)
    o_ref[0] = _dot(cs, w_ref[0]) + b_ref[0]


def ada_mod(c_all, ada_w_bf, ada_b):
    n_layers, d, n = ada_w_bf.shape
    r = c_all.shape[0]
    tn = 1536
    return pl.pallas_call(
        _ada_kernel,
        grid=(n_layers, n // tn),
        in_specs=[pl.BlockSpec((r, d), lambda l, j: (0, 0)),
                  pl.BlockSpec((1, d, tn), lambda l, j: (l, 0, j)),
                  pl.BlockSpec((1, 1, tn), lambda l, j: (l, 0, j))],
        out_specs=pl.BlockSpec((1, r, tn), lambda l, j: (l, 0, j)),
        out_shape=jax.ShapeDtypeStruct((n_layers, r, n), F32),
        compiler_params=_cparams(("arbitrary", "arbitrary")),
        name="ada_mod",
    )(c_all, ada_w_bf, ada_b.reshape(n_layers, 1, n))


def _mod_chunk(mod_ref, k, per_row):
    return mod_ref[k] if per_row else mod_ref[0, k:k + 1, :]


def _mod_spec(per_row, tm, d, tiles_per_batch):
    if per_row:
        return pl.BlockSpec((6, tm, d), lambda i, *_: (0, i, 0))
    return pl.BlockSpec((1, 6, d), lambda i, *_: (i // tiles_per_batch, 0, 0))


def _norm_mod(x, g, shift, scale):
    y = x * lax.rsqrt(jnp.mean(x * x, axis=-1, keepdims=True) + EPS) * g
    return y * (1.0 + scale) + shift


def _kin_kernel(per_row, groups, x_ref, mod_ref, g_ref, w_ref, b_ref, *refs):
    outs, h_scr = refs[:-1], refs[-1]
    h_scr[...] = _norm_mod(x_ref[...], g_ref[...], _mod_chunk(mod_ref, 0, per_row),
                           _mod_chunk(mod_ref, 1, per_row)).astype(BF16)
    for (c0, wpad, wout), o_ref in zip(groups, outs):
        z = _dot(h_scr[...], w_ref[:, c0:c0 + wpad]) + b_ref[:, c0:c0 + wpad]
        o_ref[...] = z[:, :wout]


def k_in(x2d, mod, g, w_bf, b, groups, tm, tiles_per_batch, per_row):
    m, d = x2d.shape
    wp = w_bf.shape[1]
    return pl.pallas_call(
        functools.partial(_kin_kernel, per_row, groups),
        grid=(m // tm,),
        in_specs=[pl.BlockSpec((tm, d), lambda i: (i, 0)),
                  _mod_spec(per_row, tm, d, tiles_per_batch),
                  pl.BlockSpec((1, d), lambda i: (0, 0)),
                  pl.BlockSpec((d, wp), lambda i: (0, 0)),
                  pl.BlockSpec((1, wp), lambda i: (0, 0))],
        out_specs=[pl.BlockSpec((tm, wout), lambda i: (i, 0)) for _, _, wout in groups],
        out_shape=[jax.ShapeDtypeStruct((m, wout), F32) for _, _, wout in groups],
        scratch_shapes=[pltpu.VMEM((tm, d), BF16)],
        compiler_params=_cparams(("arbitrary",)),
        name="k_in",
    )(x2d, mod, g.reshape(1, d), w_bf, b.reshape(1, wp))


def _kout_kernel(per_row, n_a, *refs):
    a_refs = refs[:n_a]
    b_ref, x_ref, mod_ref, w_ref, o_ref = refs[n_a:]
    a = a_refs[0][...]
    for r in a_refs[1:]:
        a = a + r[...]
    half = a.shape[1]
    y = _dot(a.astype(BF16), w_ref[:half, :]) + _dot(b_ref[...].astype(BF16), w_ref[half:, :])
    o_ref[...] = x_ref[...] + _mod_chunk(mod_ref, 2, per_row) * y


def k_out(a_list, b2d, x2d, mod, w_bf, tm, tiles_per_batch, per_row):
    m, d = x2d.shape
    half = b2d.shape[1]
    n_a = len(a_list)
    row_spec = pl.BlockSpec((tm, half), lambda i: (i, 0))
    return pl.pallas_call(
        functools.partial(_kout_kernel, per_row, n_a),
        grid=(m // tm,),
        in_specs=[row_spec] * (n_a + 1) + [
            pl.BlockSpec((tm, d), lambda i: (i, 0)),
            _mod_spec(per_row, tm, d, tiles_per_batch),
            pl.BlockSpec((2 * half, d), lambda i: (0, 0))],
        out_specs=pl.BlockSpec((tm, d), lambda i: (i, 0)),
        out_shape=jax.ShapeDtypeStruct((m, d), F32),
        compiler_params=_cparams(("arbitrary",)),
        name="k_out",
    )(*a_list, b2d, x2d, mod, w_bf)


def _mlp_kernel(per_row, final, x_ref, mod_ref, g_ref, fg_ref, w1_ref, w2_ref, o_ref, h_scr, acc_scr):
    f = pl.program_id(1)

    @pl.when(f == 0)
    def _():
        h_scr[...] = _norm_mod(x_ref[...], g_ref[...], _mod_chunk(mod_ref, 3, per_row),
                               _mod_chunk(mod_ref, 4, per_row)).astype(BF16)
        acc_scr[...] = jnp.zeros_like(acc_scr)

    a = jnp.maximum(_dot(h_scr[...], w1_ref[...]), 0.0)
    acc_scr[...] += _dot((a * a).astype(BF16), w2_ref[...])

    @pl.when(f == pl.num_programs(1) - 1)
    def _():
        xn = x_ref[...] + _mod_chunk(mod_ref, 5, per_row) * acc_scr[...]
        if final:
            xn = xn * lax.rsqrt(jnp.mean(xn * xn, axis=-1, keepdims=True) + EPS) * fg_ref[...]
        o_ref[...] = xn


def k_mlp(x2d, mod, g, final_g, w1_bf, w2_bf, tm, tf, tiles_per_batch, per_row, final):
    m, d = x2d.shape
    dff = w1_bf.shape[1]
    return pl.pallas_call(
        functools.partial(_mlp_kernel, per_row, final),
        grid=(m // tm, dff // tf),
        in_specs=[pl.BlockSpec((tm, d), lambda i, f: (i, 0)),
                  _mod_spec(per_row, tm, d, tiles_per_batch),
                  pl.BlockSpec((1, d), lambda i, f: (0, 0)),
                  pl.BlockSpec((1, d), lambda i, f: (0, 0)),
                  pl.BlockSpec((d, tf), lambda i, f: (0, f)),
                  pl.BlockSpec((tf, d), lambda i, f: (f, 0))],
        out_specs=pl.BlockSpec((tm, d), lambda i, f: (i, 0)),
        out_shape=jax.ShapeDtypeStruct((m, d), F32),
        scratch_shapes=[pltpu.VMEM((tm, d), BF16), pltpu.VMEM((tm, d), F32)],
        compiler_params=_cparams(("arbitrary", "arbitrary")),
        name="k_mlp",
    )(x2d, mod, g.reshape(1, d), final_g.reshape(1, d), w1_bf, w2_bf)


def _qz_block(q_ref, h, scale):
    pair = q_ref[0, :, (h // 2) * LANES:(h // 2 + 1) * LANES]
    if (h % 2) != (h // GROUP):
        pair = pltpu.roll(pair, DH, 1)
    lane = lax.broadcasted_iota(I32, pair.shape, 1)
    keep = (lane < DH) if h // GROUP == 0 else (lane >= DH)
    return jnp.where(keep, pair * scale, 0.0).astype(BF16)


def _assemble_heads(o_list):
    lane = lax.broadcasted_iota(I32, o_list[0].shape, 1)
    pairs = []
    for p in range(N_HEADS // 2):
        a, b = o_list[2 * p], o_list[2 * p + 1]
        if (2 * p) // GROUP != 0:
            a = pltpu.roll(a, DH, 1)
        if (2 * p + 1) // GROUP != 1:
            b = pltpu.roll(b, DH, 1)
        pairs.append(jnp.where(lane < DH, a, b))
    return jnp.concatenate(pairs, axis=1)


def _gate(g_ref, col):
    return jax.nn.sigmoid(g_ref[0, :, col:col + 1])


def _flash_init(m_scr, l_scr, acc_scr):
    m_scr[...] = jnp.full(m_scr.shape, NEG, F32)
    l_scr[...] = jnp.zeros(l_scr.shape, F32)
    acc_scr[...] = jnp.zeros(acc_scr.shape, F32)


def _flash_update(s, mask, v_bf, m_scr, l_scr, acc_scr, r0, nr):
    s = jnp.where(mask, s, NEG)
    m_old = m_scr[r0:r0 + nr, :]
    m_new = jnp.maximum(m_old, jnp.max(s, axis=-1, keepdims=True))
    p = jnp.where(mask, jnp.exp(s - m_new), 0.0)
    alpha = jnp.exp(m_old - m_new)
    l_scr[r0:r0 + nr, :] = alpha * l_scr[r0:r0 + nr, :] + jnp.sum(p, axis=-1, keepdims=True)
    acc_scr[r0:r0 + nr, :] = alpha * acc_scr[r0:r0 + nr, :] + _dot(p.astype(BF16), v_bf)
    m_scr[r0:r0 + nr, :] = m_new


def _flash_heads(l_scr, acc_scr, tq):
    return [acc_scr[h * tq:(h + 1) * tq, :] / jnp.maximum(l_scr[h * tq:(h + 1) * tq, :], TINY)
            for h in range(N_HEADS)]


def _topk_mask(s, k):
    n = s.shape[1]
    key = pltpu.bitcast(s + 0.0, I32)
    key = jnp.where(key < 0, key ^ jnp.int32(0x7FFFFFFF), key)

    def step(it, t):
        cand = t + lax.shift_left(jnp.int32(1), 31 - it)
        cnt = jnp.sum((key >= cand).astype(F32), axis=-1, keepdims=True)
        return jnp.where(cnt >= k, cand, t)

    t = lax.fori_loop(0, 32, step, jnp.full((s.shape[0], 1), INT_MIN, I32))
    gt = key > t
    eq = key == t
    need = k - jnp.sum(gt.astype(F32), axis=-1, keepdims=True)
    lower = (lax.broadcasted_iota(I32, (n, n), 0) < lax.broadcasted_iota(I32, (n, n), 1))
    before = _dot(eq.astype(BF16), lower.astype(BF16))
    return gt | (eq & (before < need))


def _battn_kernel(cfg, *refs):
    if cfg["mode"] == "block":
        q_ref, k_ref, v_ref, bm_ref, g_ref, o_ref, qz_scr, m_scr, l_scr, acc_scr = refs
    else:
        q_ref, k_ref, v_ref, g_ref, o_ref, qz_scr, m_scr, l_scr, acc_scr = refs
    tq, ck, koff = cfg["tq"], cfg["ck"], cfg["koff"]
    n_chunks = k_ref.shape[1] // ck
    qbase = cfg["q0"] + pl.program_id(1) * tq
    for h in range(N_HEADS):
        qz_scr[h * tq:(h + 1) * tq, :] = _qz_block(q_ref, h, SCALE)
    _flash_init(m_scr, l_scr, acc_scr)
    qpos = qbase + lax.broadcasted_iota(I32, (tq, 1), 0)
    hi_chunk = jnp.minimum((qbase + tq - 1 - koff) // ck + 1, n_chunks)
    if cfg["mode"] == "block":
        lo_chunk = 0
    else:
        lo_chunk = jnp.maximum(qbase - (WINDOW - 1) - koff, 0) // ck
    hpb = cfg["hpb"]

    def body(c, carry):
        k0 = pl.multiple_of(c * ck, ck)
        k_bf = k_ref[0, pl.ds(k0, ck), :].astype(BF16)
        v_bf = v_ref[0, pl.ds(k0, ck), :].astype(BF16)
        kpos = koff + k0 + lax.broadcasted_iota(I32, (1, ck), 1)
        base = kpos <= qpos
        if cfg["mode"] == "window":
            base = base & (qpos - kpos < WINDOW) & (kpos >= 0)
            head_masks = [base] * N_HEADS
        else:
            nbp, mg = cfg["nbp"], cfg["mask_group"]
            blk = lax.shift_right_logical(kpos, cfg["bshift"])
            expand = (lax.broadcasted_iota(I32, (nbp, ck), 0) == blk).astype(BF16)
            group_masks = [base & (_dot(bm_ref[0, :, g * nbp:(g + 1) * nbp], expand) > 0.5)
                           for g in range(N_HEADS // mg)]
            head_masks = [group_masks[h // mg] for h in range(N_HEADS)]
        for h0 in range(0, N_HEADS, hpb):
            r0, nr = h0 * tq, hpb * tq
            mask = head_masks[h0] if hpb == 1 else jnp.concatenate(head_masks[h0:h0 + hpb], axis=0)
            s = _dot_nt(qz_scr[r0:r0 + nr, :], k_bf)
            _flash_update(s, mask, v_bf, m_scr, l_scr, acc_scr, r0, nr)
        return carry

    lax.fori_loop(lo_chunk, hi_chunk, body, 0)
    heads = _flash_heads(l_scr, acc_scr, tq)
    if cfg["gate_col"] is not None:
        heads = [o * _gate(g_ref, cfg["gate_col"] + h) for h, o in enumerate(heads)]
    o_ref[0] = _assemble_heads(heads)


def block_attention(q, kv, k_col, v_col, gates, bmask, *, mode, tq, ck, q0, koff=0, nbp=0, bshift=0,
                    mask_group=1, gate_col=None, hpb=1):
    b, t, dq = q.shape
    lp = kv.shape[1]
    cfg = dict(mode=mode, tq=tq, ck=ck, q0=q0, koff=koff, nbp=nbp, bshift=bshift,
               mask_group=mask_group, gate_col=gate_col, hpb=hpb)
    in_specs = [pl.BlockSpec((1, tq, dq), lambda bi, i: (bi, i, 0)),
                pl.BlockSpec((1, lp, LANES), lambda bi, i: (bi, 0, k_col)),
                pl.BlockSpec((1, lp, LANES), lambda bi, i: (bi, 0, v_col))]
    args = [q, kv, kv]
    if mode == "block":
        in_specs.append(pl.BlockSpec((1, tq, bmask.shape[2]), lambda bi, i: (bi, i, 0)))
        args.append(bmask)
    in_specs.append(pl.BlockSpec((1, tq, LANES), lambda bi, i: (bi, i, 0)))
    args.append(gates)
    return pl.pallas_call(
        functools.partial(_battn_kernel, cfg),
        grid=(b, t // tq),
        in_specs=in_specs,
        out_specs=pl.BlockSpec((1, tq, dq), lambda bi, i: (bi, i, 0)),
        out_shape=jax.ShapeDtypeStruct((b, t, dq), F32),
        scratch_shapes=[pltpu.VMEM((N_HEADS * tq, LANES), BF16),
                        pltpu.VMEM((N_HEADS * tq, 1), F32),
                        pltpu.VMEM((N_HEADS * tq, 1), F32),
                        pltpu.VMEM((N_HEADS * tq, LANES), F32)],
        compiler_params=_cparams(("arbitrary", "arbitrary")),
        name="attn_" + mode,
    )(*args)


def _compress_kernel(n_groups, x_ref, pe_ref, w_ref, o_ref, a_scr, b_scr):
    acc_a = jnp.zeros((n_groups, LANES), F32)
    acc_b = jnp.zeros((n_groups, LANES), F32)
    for l in range(CMP_STRIDE):
        x = x_ref[0, pl.ds(l, n_groups, stride=CMP_STRIDE), :]
        acc_a += _dot((x + pe_ref[0, l:l + 1, :]).astype(BF16), w_ref[0, l])
        acc_b += _dot((x + pe_ref[0, CMP_STRIDE + l:CMP_STRIDE + l + 1, :]).astype(BF16),
                      w_ref[0, CMP_STRIDE + l])
    a_scr[...] = acc_a
    b_scr[0:n_groups, :] = acc_b
    b_scr[n_groups:n_groups + 8, :] = jnp.zeros((8, LANES), F32)
    o_ref[0] = jnp.zeros(o_ref.shape[1:], F32)
    o_ref[0, 0:n_groups, :] = a_scr[...] + b_scr[pl.ds(1, n_groups), :]


def nsa_compress(kv, pe2, w_bd, ncp):
    b, lp, _ = kv.shape
    n_groups = lp // CMP_STRIDE
    return pl.pallas_call(
        functools.partial(_compress_kernel, n_groups),
        grid=(b, 2),
        in_specs=[pl.BlockSpec((1, lp, LANES), lambda bi, j: (bi, 0, j)),
                  pl.BlockSpec((1, CMP_LEN, LANES), lambda bi, j: (j, 0, 0)),
                  pl.BlockSpec((1, CMP_LEN, LANES, LANES), lambda bi, j: (j, 0, 0, 0))],
        out_specs=pl.BlockSpec((1, ncp, LANES), lambda bi, j: (bi, 0, j)),
        out_shape=jax.ShapeDtypeStruct((b, ncp, 2 * LANES), F32),
        scratch_shapes=[pltpu.VMEM((n_groups, LANES), F32), pltpu.VMEM((n_groups + 8, LANES), F32)],
        compiler_params=_cparams(("arbitrary", "arbitrary")),
        name="nsa_compress",
    )(kv, pe2, w_bd)


def _nsa_select_kernel(cfg, q_ref, c_ref, g_ref, o_ref, bm_ref):
    tq, nc, nsp = cfg["tq"], cfg["nc"], cfg["nsp"]
    ncp = c_ref.shape[1]
    qbase = cfg["q0"] + pl.program_id(1) * tq
    qpos = qbase + lax.broadcasted_iota(I32, (tq, 1), 0)
    kc = c_ref[0, :, 0:LANES].astype(BF16)
    vc = c_ref[0, :, LANES:2 * LANES].astype(BF16)
    n = lax.broadcasted_iota(I32, (1, ncp), 1)
    mask = (n * CMP_STRIDE + (CMP_LEN - 1) <= qpos) & (n < nc)
    heads, psum = [], [None] * HKV
    for h in range(N_HEADS):
        s = jnp.where(mask, _dot_nt(_qz_block(q_ref, h, SCALE), kc), NEG)
        m = jnp.max(s, axis=-1, keepdims=True)
        e = jnp.where(mask, jnp.exp(s - m), 0.0)
        p = e / jnp.maximum(jnp.sum(e, axis=-1, keepdims=True), TINY)
        heads.append(_dot(p.astype(BF16), vc) * _gate(g_ref, cfg["gate_col"] + h))
        psum[h // GROUP] = p if psum[h // GROUP] is None else psum[h // GROUP] + p
    o_ref[0] = _assemble_heads(heads)
    cs = lax.broadcasted_iota(I32, (ncp, nsp), 0) * CMP_STRIDE
    ss = lax.broadcasted_iota(I32, (ncp, nsp), 1) * SEL_BLOCK
    overlap = ((cs < ss + SEL_BLOCK) & (cs + CMP_LEN > ss)).astype(BF16)
    j = lax.broadcasted_iota(I32, (1, nsp), 1)
    blk = qpos // SEL_BLOCK
    forced = (j == 0) | (j == blk) | (j == blk - 1)
    for g in range(HKV):
        hi, mid, lo = _split3(psum[g])
        imp = _dot(hi, overlap) + _dot(mid, overlap) + _dot(lo, overlap)
        score = jnp.where(j <= blk, imp + jnp.where(forced, FORCE, 0.0), NEG)
        sel = _topk_mask(score, N_SEL) & (score > NEG / 2)
        bm_ref[0, :, g * nsp:(g + 1) * nsp] = sel.astype(BF16)


def nsa_select(q, cmp_kv, gates, *, tq, q0, nc, nsp, gate_col):
    b, t, dq = q.shape
    ncp = cmp_kv.shape[1]
    cfg = dict(tq=tq, q0=q0, nc=nc, nsp=nsp, gate_col=gate_col)
    return pl.pallas_call(
        functools.partial(_nsa_select_kernel, cfg),
        grid=(b, t // tq),
        in_specs=[pl.BlockSpec((1, tq, dq), lambda bi, i: (bi, i, 0)),
                  pl.BlockSpec((1, ncp, 2 * LANES), lambda bi, i: (bi, 0, 0)),
                  pl.BlockSpec((1, tq, LANES), lambda bi, i: (bi, i, 0))],
        out_specs=[pl.BlockSpec((1, tq, dq), lambda bi, i: (bi, i, 0)),
                   pl.BlockSpec((1, tq, HKV * nsp), lambda bi, i: (bi, i, 0))],
        out_shape=[jax.ShapeDtypeStruct((b, t, dq), F32),
                   jax.ShapeDtypeStruct((b, t, HKV * nsp), BF16)],
        compiler_params=_cparams(("arbitrary", "arbitrary")),
        name="nsa_select",
    )(q, cmp_kv, gates)


def _sort_key(x):
    key = pltpu.bitcast(x + 0.0, I32)
    return jnp.where(key < 0, key ^ jnp.int32(0x7FFFFFFF), key)


_HALF_NEG_KEY = int(np.float32(NEG / 2).view(np.int32) ^ 0x7FFFFFFF)


def _dsa_kernel(cfg, q_ref, iq_ref, g_ref, ik_ref, k_ref, v_ref, o_ref,
                qz_scr, iq_scr, key_scr, low_scr, m_scr, l_scr, acc_scr):
    tq, ck, n_top, hpb = cfg["tq"], cfg["ck"], cfg["n_top"], cfg["hpb"]
    n_chunks = k_ref.shape[1] // ck
    qbase = cfg["q0"] + pl.program_id(1) * tq
    qpos = qbase + lax.broadcasted_iota(I32, (tq, 1), 0)
    hi_chunk = jnp.minimum((qbase + tq - 1) // ck + 1, n_chunks)
    for h in range(N_HEADS):
        qz_scr[h * tq:(h + 1) * tq, :] = _qz_block(q_ref, h, SCALE)
        pair = iq_ref[0, :, (h // 2) * LANES:(h // 2 + 1) * LANES]
        if h % 2:
            pair = pltpu.roll(pair, DH, 1)
        iq_scr[h * tq:(h + 1) * tq, :] = pair[:, :DH].astype(BF16)
    low_scr[...] = (lax.broadcasted_iota(I32, (ck, ck), 0)
                    < lax.broadcasted_iota(I32, (ck, ck), 1)).astype(BF16)

    def score_body(c, carry):
        k0 = pl.multiple_of(c * ck, ck)
        ik = ik_ref[0, pl.ds(k0, ck), :].astype(BF16)
        s_all = jnp.maximum(_dot_nt(iq_scr[...], ik), 0.0)
        sc = s_all[0:tq, :] * g_ref[0, :, 0:1]
        for h in range(1, N_HEADS):
            sc = sc + s_all[h * tq:(h + 1) * tq, :] * g_ref[0, :, h:h + 1]
        kpos = k0 + lax.broadcasted_iota(I32, (1, ck), 1)
        key_scr[c] = _sort_key(jnp.where(kpos <= qpos, sc, NEG))
        return carry

    lax.fori_loop(0, hi_chunk, score_body, 0)

    def count(pred):
        def body(c, a):
            hit = pred(key_scr[c]).astype(F32)
            for j in range(ck // LANES):
                a = a + hit[:, j * LANES:(j + 1) * LANES]
            return a
        a = lax.fori_loop(0, hi_chunk, body, jnp.zeros((tq, LANES), F32))
        return jnp.sum(a, axis=-1, keepdims=True)

    def bit_step(it, t):
        cand = t + lax.shift_left(jnp.int32(1), 31 - it)
        return jnp.where(count(lambda kk: kk >= cand) >= n_top, cand, t)

    t = lax.fori_loop(0, 32, bit_step, jnp.full((tq, 1), INT_MIN, I32))
    need = n_top - count(lambda kk: kk > t)
    _flash_init(m_scr, l_scr, acc_scr)

    def attn_body(c, before):
        k0 = pl.multiple_of(c * ck, ck)
        kk = key_scr[c]
        eq = kk == t
        rank_eq = before + _dot(eq.astype(BF16), low_scr[...])
        mask = ((kk > t) | (eq & (rank_eq < need))) & (kk > _HALF_NEG_KEY)
        k_bf = k_ref[0, pl.ds(k0, ck), :].astype(BF16)
        v_bf = v_ref[0, pl.ds(k0, ck), :].astype(BF16)
        for h0 in range(0, N_HEADS, hpb):
            r0, nr = h0 * tq, hpb * tq
            mk = mask if hpb == 1 else jnp.concatenate([mask] * hpb, axis=0)
            _flash_update(_dot_nt(qz_scr[r0:r0 + nr, :], k_bf), mk, v_bf, m_scr, l_scr, acc_scr, r0, nr)
        return before + jnp.sum(eq.astype(F32), axis=-1, keepdims=True)

    lax.fori_loop(0, hi_chunk, attn_body, jnp.zeros((tq, 1), F32))
    o_ref[0] = _assemble_heads(_flash_heads(l_scr, acc_scr, tq))


def dsa_attention(q, iq, misc, ik, kv, *, tq, ck, q0, n_top, hpb=1):
    b, t, dq = q.shape
    lp = kv.shape[1]
    cfg = dict(tq=tq, ck=ck, q0=q0, n_top=n_top, hpb=hpb)
    qspec = pl.BlockSpec((1, tq, dq), lambda bi, i: (bi, i, 0))
    return pl.pallas_call(
        functools.partial(_dsa_kernel, cfg),
        grid=(b, t // tq),
        in_specs=[qspec, qspec,
                  pl.BlockSpec((1, tq, LANES), lambda bi, i: (bi, i, 0)),
                  pl.BlockSpec((1, lp, DH), lambda bi, i: (bi, 0, 0)),
                  pl.BlockSpec((1, lp, LANES), lambda bi, i: (bi, 0, 0)),
                  pl.BlockSpec((1, lp, LANES), lambda bi, i: (bi, 0, 1))],
        out_specs=qspec,
        out_shape=jax.ShapeDtypeStruct((b, t, dq), F32),
        scratch_shapes=[pltpu.VMEM((N_HEADS * tq, LANES), BF16),
                        pltpu.VMEM((N_HEADS * tq, DH), BF16),
                        pltpu.VMEM((lp // ck, tq, ck), I32),
                        pltpu.VMEM((ck, ck), BF16),
                        pltpu.VMEM((N_HEADS * tq, 1), F32),
                        pltpu.VMEM((N_HEADS * tq, 1), F32),
                        pltpu.VMEM((N_HEADS * tq, LANES), F32)],
        compiler_params=_cparams(("arbitrary", "arbitrary")),
        name="dsa_attention",
    )(q, iq, misc, ik, kv, kv)


def _moba_select_kernel(cfg, q_ref, k_ref, bm_ref, km_scr):
    tq, nblk, nbp, n_top = cfg["tq"], cfg["nblk"], cfg["nbp"], cfg["n_top"]
    lp = k_ref.shape[1]
    qpos = cfg["q0"] + pl.program_id(1) * tq + lax.broadcasted_iota(I32, (tq, 1), 0)
    km_scr[...] = jnp.zeros(km_scr.shape, F32)
    for j in range(nblk):
        r1 = min((j + 1) * MOBA_BLOCK, lp)
        km_scr[j:j + 1, :] = jnp.sum(k_ref[0, j * MOBA_BLOCK:r1, :], axis=0, keepdims=True) * (1.0 / MOBA_BLOCK)
    km = km_scr[...].astype(BF16)
    own = qpos // MOBA_BLOCK
    j = lax.broadcasted_iota(I32, (1, nbp), 1)
    for h in range(N_HEADS):
        s = jnp.where(j < own, _dot_nt(_qz_block(q_ref, h, 1.0), km), NEG)
        sel = (j == own)
        if n_top > 0:
            sel = sel | (_topk_mask(s, n_top) & (s > NEG / 2))
        bm_ref[0, :, h * nbp:(h + 1) * nbp] = sel.astype(BF16)


def moba_select(q, kv, *, tq, q0, nblk, nbp):
    b, t, dq = q.shape
    lp = kv.shape[1]
    cfg = dict(tq=tq, q0=q0, nblk=nblk, nbp=nbp, n_top=min(MOBA_TOPK, nblk - 1))
    return pl.pallas_call(
        functools.partial(_moba_select_kernel, cfg),
        grid=(b, t // tq),
        in_specs=[pl.BlockSpec((1, tq, dq), lambda bi, i: (bi, i, 0)),
                  pl.BlockSpec((1, lp, LANES), lambda bi, i: (bi, 0, 0))],
        out_specs=pl.BlockSpec((1, tq, N_HEADS * nbp), lambda bi, i: (bi, i, 0)),
        out_shape=jax.ShapeDtypeStruct((b, t, N_HEADS * nbp), BF16),
        scratch_shapes=[pltpu.VMEM((nbp, LANES), F32)],
        compiler_params=_cparams(("arbitrary", "arbitrary")),
        name="moba_select",
    )(q, kv)


def _log_sigmoid(x):
    return jnp.minimum(x, 0.0) - jnp.log(1.0 + jnp.exp(-jnp.abs(x)))


def _dot3_rhs(a_bf, x):
    hi, mid, lo = _split3(x)
    return _dot(a_bf, hi) + _dot(a_bf, mid) + _dot(a_bf, lo)


def _dot3_lhs(x, b_bf):
    hi, mid, lo = _split3(x)
    return _dot(hi, b_bf) + _dot(mid, b_bf) + _dot(lo, b_bf)


def _mlstm_kernel(cfg, u_ref, v_ref, og_ref, g_ref, cin_ref, ct0_ref, n0_ref, m0_ref,
                  cw_ref, cb_ref, wq_ref, wk_ref, fb_ref, ng_ref,
                  h_out, ct_out, n_out, m_out,
                  ubuf, q_scr, k_scr, v_scr, g_scr, h_scr, ct_scr, n_scr, m_scr, hm_scr):
    tc, tcp, t_valid = cfg["tc"], cfg["tcp"], cfg["t_valid"]
    d_c = N_HEADS * DH
    i = pl.program_id(1)

    @pl.when(i == 0)
    def _():
        ubuf[0:8, :] = cin_ref[0]
        ct_scr[...] = ct0_ref[0]
        n_scr[...] = n0_ref[0]
        m_scr[...] = m0_ref[0]
        hm_scr[...] = (lax.broadcasted_iota(I32, (d_c, d_c), 0) // DH
                       == lax.broadcasted_iota(I32, (d_c, d_c), 1) // DH).astype(F32)

    if tc < tcp:
        ubuf[8:, :] = jnp.zeros((tcp, d_c), F32)
        v_scr[...] = jnp.zeros((tcp, d_c), F32)
        g_scr[...] = jnp.zeros((tcp, LANES), F32)
    ubuf[8:8 + tc, :] = u_ref[0]
    v_scr[0:tc, :] = v_ref[0]
    g_scr[0:tc, :] = g_ref[0]
    conv = ubuf[pl.ds(CONV_W + 1, tcp), :] * cw_ref[0:1, :]
    for j in range(1, CONV_W):
        conv = conv + ubuf[pl.ds(CONV_W + 1 + j, tcp), :] * cw_ref[j:j + 1, :]
    conv = conv + cb_ref[...]
    uc = (conv * jax.nn.sigmoid(conv)).astype(BF16)
    q_scr[...] = _dot(uc, wq_ref[...])
    k_scr[...] = _dot(uc, wk_ref[...]) * SCALE

    lane = lax.broadcasted_iota(I32, (1, LANES), 1)
    head_lane = lane < N_HEADS
    t_io = lax.broadcasted_iota(I32, (CHUNK, 1), 0)
    causal = lane <= t_io
    tri = (lax.broadcasted_iota(I32, (CHUNK, CHUNK), 1)
           <= lax.broadcasted_iota(I32, (CHUNK, CHUNK), 0)).astype(BF16)
    tri_t = (lax.broadcasted_iota(I32, (LANES, LANES), 0)
             <= lax.broadcasted_iota(I32, (LANES, LANES), 1)).astype(BF16)
    expand = (lax.broadcasted_iota(I32, (LANES, d_c), 0)
              == lax.broadcasted_iota(I32, (LANES, d_c), 1) // DH).astype(BF16)
    zeros_gate = jnp.zeros((CHUNK, LANES), F32)
    zeros_feat = jnp.zeros((CHUNK, d_c), F32)

    def chunk_body(c, carry):
        r0 = pl.multiple_of(c * CHUNK, CHUNK)
        hm = hm_scr[...]
        g = g_scr[pl.ds(r0, CHUNK), :]
        valid = (i * tc + r0 + t_io) < t_valid
        ig = jnp.where(head_lane, jnp.where(valid, g, NEG), 0.0)
        lf = pltpu.roll(_log_sigmoid(g + fb_ref[...]), LANES - N_HEADS, 1)
        lf = jnp.where(head_lane & valid, lf, 0.0)
        b_col = _dot3_rhs(tri, lf)
        ig_t = jnp.concatenate([ig, zeros_gate], axis=0).T[0:8, :]
        lf_t = jnp.concatenate([lf, zeros_gate], axis=0).T[0:8, :]
        rowterm = ig_t - _dot3_lhs(lf_t, tri_t)
        m_row = m_scr[...]
        dws, iws, emts = [], [], []
        for h in range(N_HEADS):
            bc = b_col[:, h:h + 1]
            dlog = jnp.where(causal, bc + rowterm[h:h + 1, :], NEG)
            inter = bc + m_row[:, h:h + 1]
            m_t = jnp.maximum(inter, jnp.max(dlog, axis=-1, keepdims=True))
            dws.append(jnp.exp(dlog - m_t))
            iws.append(jnp.exp(inter - m_t))
            emts.append(jnp.exp(-m_t))
        dw = jnp.concatenate(dws, axis=0)
        iw = jnp.concatenate(iws, axis=0)
        emt = jnp.concatenate(emts, axis=0)
        q_c = q_scr[pl.ds(r0, CHUNK), :]
        k_c = k_scr[pl.ds(r0, CHUNK), :]
        v_c = v_scr[pl.ds(r0, CHUNK), :]
        qz = jnp.concatenate([q_c] * N_HEADS, axis=0) * hm
        qz_bf = qz.astype(BF16)
        k_pad = jnp.concatenate([k_c, zeros_feat], axis=0)
        v_pad = jnp.concatenate([v_c, zeros_feat], axis=0).astype(BF16)
        qkw = _dot_nt(qz_bf, k_pad.astype(BF16)) * dw
        intra = _dot(qkw.astype(BF16), v_pad)
        inter_z = _dot(qz_bf, ct_scr[...].astype(BF16))
        num = iw * inter_z + intra * hm
        den = iw * jnp.sum(qz * n_scr[...], axis=-1, keepdims=True) + jnp.sum(qkw, axis=-1, keepdims=True)
        hz = num / jnp.maximum(jnp.abs(den), emt)
        h_c = hz[0:CHUNK, :]
        for h in range(1, N_HEADS):
            h_c = h_c + hz[h * CHUNK:(h + 1) * CHUNK, :]
        h_scr[pl.ds(r0, CHUNK), :] = h_c
        b_last = b_col[CHUNK - 1:CHUNK, :]
        m_new = jnp.maximum(b_last + m_row, jnp.max(b_last - b_col + ig, axis=0, keepdims=True))
        decay = jnp.where(head_lane, jnp.exp(b_last + m_row - m_new), 0.0)
        ws = jnp.where(head_lane, jnp.exp(b_last - b_col + ig - m_new), 0.0)
        wide = _dot3_lhs(jnp.concatenate([ws, jnp.broadcast_to(decay, (8, LANES))], axis=0), expand)
        w8, decay_w = wide[0:CHUNK, :], wide[CHUNK:CHUNK + 1, :]
        vw_pad = jnp.concatenate([v_c * w8, zeros_feat], axis=0).astype(BF16)
        k_t = jnp.concatenate([k_pad[:, j * LANES:(j + 1) * LANES].T for j in range(d_c // LANES)], axis=0)
        ct_scr[...] = decay_w * ct_scr[...] + _dot(k_t.astype(BF16), vw_pad) * hm
        n_scr[...] = decay_w * n_scr[...] + jnp.sum(k_c * w8, axis=0, keepdims=True)
        m_scr[...] = m_new
        return carry

    lax.fori_loop(0, tcp // CHUNK, chunk_body, 0)
    h_all = h_scr[...]
    hi, mid, lo = _split3(h_all * h_all)
    hm_bf = hm_scr[...].astype(BF16)
    ms = (_dot(hi, hm_bf) + _dot(mid, hm_bf) + _dot(lo, hm_bf)) * (1.0 / DH)
    hc = h_all * lax.rsqrt(ms + EPS) * ng_ref[...]
    h_out[0] = jax.nn.sigmoid(og_ref[0]) * hc[0:tc, :]
    ubuf[0:8, :] = ubuf[tc:tc + 8, :]

    @pl.when(i == pl.num_programs(1) - 1)
    def _():
        ct_out[0] = ct_scr[...]
        n_out[0] = n_scr[...]
        m_out[0] = m_scr[...]


def mlstm(u, v, og, misc, conv_in, ct0, n0, m0, conv_w, conv_b, wq_bd, wk_bd, fb_row, norm_g, *, tc, t_valid):
    b, t, d_c = u.shape
    tcp = max(tc, CHUNK)
    cfg = dict(tc=tc, tcp=tcp, t_valid=t_valid)
    row = pl.BlockSpec((1, tc, d_c), lambda bi, i: (bi, i, 0))
    const = lambda shape: pl.BlockSpec(shape, lambda bi, i: (0,) * len(shape))
    per_b = lambda shape: pl.BlockSpec((1,) + shape, lambda bi, i: (bi,) + (0,) * len(shape))
    return pl.pallas_call(
        functools.partial(_mlstm_kernel, cfg),
        grid=(b, t // tc),
        in_specs=[row, row, row, pl.BlockSpec((1, tc, LANES), lambda bi, i: (bi, i, 0)),
                  per_b((8, d_c)), per_b((d_c, d_c)), per_b((1, d_c)), per_b((1, LANES)),
                  const((CONV_W, d_c)), const((1, d_c)), const((d_c, d_c)), const((d_c, d_c)),
                  const((1, LANES)), const((1, d_c))],
        out_specs=[row, per_b((d_c, d_c)), per_b((1, d_c)), per_b((1, LANES))],
        out_shape=[jax.ShapeDtypeStruct((b, t, d_c), F32),
                   jax.ShapeDtypeStruct((b, d_c, d_c), F32),
                   jax.ShapeDtypeStruct((b, 1, d_c), F32),
                   jax.ShapeDtypeStruct((b, 1, LANES), F32)],
        scratch_shapes=[pltpu.VMEM((tcp + 8, d_c), F32),
                        pltpu.VMEM((tcp, d_c), F32), pltpu.VMEM((tcp, d_c), F32), pltpu.VMEM((tcp, d_c), F32),
                        pltpu.VMEM((tcp, LANES), F32), pltpu.VMEM((tcp, d_c), F32),
                        pltpu.VMEM((d_c, d_c), F32), pltpu.VMEM((1, d_c), F32), pltpu.VMEM((1, LANES), F32),
                        pltpu.VMEM((d_c, d_c), F32)],
        compiler_params=_cparams(("arbitrary", "arbitrary")),
        name="mlstm",
    )(u, v, og, misc, conv_in, ct0, n0, m0, conv_w, conv_b, wq_bd, wk_bd, fb_row, norm_g)


def _gather_kernel(layer, n_pages, page, pt_ref, pool_ref, new_ref, out_ref, sems):
    b = pl.program_id(0)

    def page_copy(p):
        return pltpu.make_async_copy(pool_ref.at[layer, pt_ref[b, p]],
                                     out_ref.at[b, pl.ds(p * page, page)], sems.at[p])

    new_copy = pltpu.make_async_copy(new_ref.at[0], out_ref.at[b, pl.ds(n_pages * page, page)],
                                     sems.at[n_pages])
    for p in range(n_pages):
        page_copy(p).start()
    new_copy.start()
    for p in range(n_pages):
        page_copy(p).wait()
    new_copy.wait()


def gather_pages(pool, layer, page_table, new_block):
    bsz, n_pages = page_table.shape
    page, c = pool.shape[2], pool.shape[3]
    return pl.pallas_call(
        functools.partial(_gather_kernel, layer, n_pages, page),
        grid_spec=pltpu.PrefetchScalarGridSpec(
            num_scalar_prefetch=1,
            grid=(bsz,),
            in_specs=[pl.BlockSpec(memory_space=pl.ANY),
                      pl.BlockSpec((1, page, c), lambda bi, pt: (bi, 0, 0))],
            out_specs=pl.BlockSpec(memory_space=pl.ANY),
            scratch_shapes=[pltpu.SemaphoreType.DMA((n_pages + 1,))]),
        out_shape=jax.ShapeDtypeStruct((bsz, (n_pages + 1) * page, c), F32),
        compiler_params=_cparams(("arbitrary",)),
        name="gather_pages",
    )(page_table, pool, new_block)


def _regroup_columns(w, b, pieces):
    n_src = w.shape[-1]
    idx = []
    for s, wd, wp in pieces:
        idx += list(range(s, s + wd)) + [n_src] * (wp - wd)
    idx = np.asarray(idx, np.int32)
    w_ext = jnp.concatenate([w, jnp.zeros(w.shape[:-1] + (1,), w.dtype)], axis=-1)
    b_ext = jnp.concatenate([b, jnp.zeros(b.shape[:-1] + (1,), b.dtype)], axis=-1)
    return jnp.take(w_ext, idx, axis=-1).astype(BF16), jnp.take(b_ext, idx, axis=-1)


def _block_diag(w):
    h, a, b = w.shape[-3:]
    eye = jnp.eye(h, dtype=w.dtype)
    out = w[..., :, :, None, :] * eye[:, None, :, None]
    return out.reshape(w.shape[:-3] + (h * a, h * b))


_EVEN_SRC = [(0, 512, 512), (512, 512, 512), (1024, 256, 256), (1304, 512, 512), (1816, 256, 256),
             (2072, 512, 512), (2584, 64, 128), (2648, 8, 8), (1280, 24, 120)]
_EVEN_GROUPS = [(0, 512, 512), (512, 512, 512), (1024, 256, 256), (1280, 512, 512), (1792, 256, 256),
                (2048, 512, 512), (2560, 128, 64), (2688, 128, 128)]
_ODD_SRC = [(0, 512, 512), (512, 512, 512), (1024, 512, 512), (1552, 512, 512), (2064, 256, 256),
            (1536, 16, 128)]
_ODD_GROUPS = [(0, 512, 512), (512, 512, 512), (1024, 512, 512), (1536, 512, 512), (2048, 256, 256),
               (2304, 128, 128)]
_GATE_COL = N_HEADS


def _pick_chunk(n, limit):
    return max(c for c in range(LANES, limit + 1, LANES) if n % c == 0)


def _pad_rows(a, rows):
    return jnp.pad(a, ((0, 0), (0, rows - a.shape[1])) + ((0, 0),) * (a.ndim - 2))


def _even_mixer(p, i, outs, bsz, t, t_real, past, page_table, q0):
    a_q, kv4, win, b_q, b_kv, b_iq, b_ik, misc = [o.reshape(bsz, t, -1) for o in outs]
    decode = past is not None
    if decode:
        page = past["nsa"].shape[2]
        new = lambda a: _pad_rows(a[:, :t_real], page)
        nsa_all = gather_pages(past["nsa"], i, page_table, new(kv4))
        dsa_all = gather_pages(past["dsa"], i, page_table, new(b_kv))
        idx_all = gather_pages(past["idx"], i, page_table, new(b_ik))
        lk = page_table.shape[1] * page + t_real
        win_buf = past["win"][i]
        ckw = -(-(win_buf.shape[1] + t_real) // LANES) * LANES
        win_all = _pad_rows(jnp.concatenate([win_buf, win[:, :t_real]], axis=1), ckw)
        koff = q0 - win_buf.shape[1]
        tq, ck, hpb = t, _pick_chunk(nsa_all.shape[1], 640), N_HEADS
    else:
        nsa_all, dsa_all, idx_all, win_all, lk, koff = kv4, b_kv, b_ik, win, t, 0
        tq, ck, ckw, hpb = 128, 512, 128, 1
    n16 = -(-lk // CMP_STRIDE)
    ncp = -(-(nsa_all.shape[1] // CMP_STRIDE) // LANES) * LANES
    nsp = -(-(-(-lk // SEL_BLOCK)) // LANES) * LANES
    cmp_kv = nsa_compress(nsa_all, p["pe2"][i], p["cmp_w"][i], ncp)
    o_cmp, bmask = nsa_select(a_q, cmp_kv, misc, tq=tq, q0=q0, nc=n16 - 1, nsp=nsp, gate_col=_GATE_COL)
    o_sel = block_attention(a_q, nsa_all, 2, 3, misc, bmask, mode="block", tq=tq, ck=ck, q0=q0, nbp=nsp,
                            bshift=6, mask_group=GROUP, gate_col=_GATE_COL + N_HEADS, hpb=hpb)
    o_win = block_attention(a_q, win_all, 0, 1, misc, None, mode="window", tq=tq, ck=ckw, q0=q0, koff=koff,
                            gate_col=_GATE_COL + 2 * N_HEADS, hpb=hpb)
    o_dsa = dsa_attention(b_q, b_iq, misc, idx_all, dsa_all, tq=tq, ck=ck, q0=q0,
                          n_top=min(DSA_TOPK, lk // 4), hpb=hpb)
    if decode:
        win_state = jnp.concatenate([win_buf, win[:, :t_real]], axis=1)[:, -win_buf.shape[1]:]
    else:
        win_state = win[:, -min(WINDOW, t):]
    state = (kv4[:, :t_real].reshape(bsz, t_real, 4, HKV, DH),
             win_state.reshape(bsz, -1, 2, HKV, DH),
             b_kv[:, :t_real].reshape(bsz, t_real, 2, HKV, DH),
             b_ik[:, :t_real])
    return [o_cmp, o_sel, o_win], o_dsa, state


def _odd_mixer(p, i, outs, bsz, t, t_real, past, page_table, q0):
    u, c_v, c_o, d_q, d_kv, misc = [o.reshape(bsz, t, -1) for o in outs]
    d_c = N_HEADS * DH
    decode = past is not None
    idx = np.arange(N_HEADS)
    if decode:
        page = past["moba"].shape[2]
        moba_all = gather_pages(past["moba"], i, page_table, _pad_rows(d_kv[:, :t_real], page))
        lk = page_table.shape[1] * page + t_real
        c_t = jnp.swapaxes(past["c"][i], -1, -2)
        ct0 = jnp.zeros((bsz, N_HEADS, DH, N_HEADS, DH), F32).at[:, idx, :, idx, :].set(
            jnp.moveaxis(c_t, 1, 0)).reshape(bsz, d_c, d_c)
        n0 = past["n"][i].reshape(bsz, 1, d_c)
        m0 = jnp.pad(past["m"][i], ((0, 0), (0, LANES - N_HEADS)))[:, None, :]
        conv_prev = past["conv"][i]
        tq, ck, hpb, tc = t, _pick_chunk(moba_all.shape[1], 640), N_HEADS, t
    else:
        moba_all, lk = d_kv, t
        ct0 = jnp.zeros((bsz, d_c, d_c), F32)
        n0 = jnp.zeros((bsz, 1, d_c), F32)
        m0 = jnp.zeros((bsz, 1, LANES), F32)
        conv_prev = jnp.zeros((bsz, CONV_W - 1, d_c), F32)
        tq, ck, hpb, tc = 128, 512, 1, 512
    conv_in = jnp.concatenate([jnp.zeros((bsz, 8 - (CONV_W - 1), d_c), F32), conv_prev], axis=1)
    hc, ct1, n1, m1 = mlstm(u, c_v, c_o, misc, conv_in, ct0, n0, m0, p["conv_w"][i], p["conv_b"][i],
                            p["wq_bd"][i], p["wk_bd"][i], p["fb_row"][i], p["norm_g"][i], tc=tc, t_valid=t_real)
    nblk = -(-lk // MOBA_BLOCK)
    bmask = moba_select(d_q, moba_all, tq=tq, q0=q0, nblk=nblk, nbp=LANES)
    o_d = block_attention(d_q, moba_all, 0, 1, misc, bmask, mode="block", tq=tq, ck=ck, q0=q0, nbp=LANES,
                          bshift=8, mask_group=1, gate_col=None, hpb=hpb)
    c1 = jnp.swapaxes(jnp.moveaxis(ct1.reshape(bsz, N_HEADS, DH, N_HEADS, DH)[:, idx, :, idx, :], 0, 1), -1, -2)
    conv_state = jnp.concatenate([conv_prev, u[:, :t_real]], axis=1)[:, -(CONV_W - 1):]
    state = (c1, n1.reshape(bsz, N_HEADS, DH), m1[:, 0, :N_HEADS], conv_state,
             d_kv[:, :t_real].reshape(bsz, t_real, 2, HKV, DH))
    return [hc], o_d, state


def _run_group(p, x, mod, t_real, past, page_table, q0, per_row):
    bsz, t, d = x.shape
    m = bsz * t
    tm = min(512, m)
    tmm = min(1024, m)
    tiles = max(t // tm, 1)
    tiles_mlp = max(t // tmm, 1)
    x2d = x.reshape(m, d)
    n_layers = p["mlp_w1"].shape[0]
    ev_states, od_states = [], []
    for l in range(n_layers):
        i = l // 2
        if per_row:
            mod_l = jnp.moveaxis(jnp.repeat(mod[l], t, axis=0), 1, 0)
        else:
            mod_l = mod[l]
        if l % 2 == 0:
            outs = k_in(x2d, mod_l, p["norm1_g"][l], p["ev_w"][i], p["ev_b"][i], _EVEN_GROUPS, tm, tiles, per_row)
            a_list, b_o, st = _even_mixer(p, i, outs, bsz, t, t_real, past, page_table, q0)
            ev_states.append(st)
            w_out = p["ev_w_out"][i]
        else:
            outs = k_in(x2d, mod_l, p["norm1_g"][l], p["od_w"][i], p["od_b"][i], _ODD_GROUPS, tm, tiles, per_row)
            a_list, b_o, st = _odd_mixer(p, i, outs, bsz, t, t_real, past, page_table, q0)
            od_states.append(st)
            w_out = p["od_w_out"][i]
        x2d = k_out([a.reshape(m, -1) for a in a_list], b_o.reshape(m, -1), x2d, mod_l, w_out, tm, tiles, per_row)
        x2d = k_mlp(x2d, mod_l, p["norm2_g"][l], p["final_g"], p["mlp_w1"][l], p["mlp_w2"][l], tmm,
                    min(1024, p["mlp_w1"].shape[2]), tiles_mlp, per_row, final=(l == n_layers - 1))
    stack = lambda states: tuple(jnp.stack(a) for a in zip(*states))
    return x2d.reshape(bsz, t, d)[:, :t_real], stack(ev_states), stack(od_states)


def kernel(x_prompt, x_sample, cache_nsa_kv, state_nsa_win, cache_dsa_kv, cache_dsa_idx, state_mlstm_c, state_mlstm_n, state_mlstm_m, state_mlstm_conv, cache_moba_kv, page_table, c_prompt, c_sample, ada_w, ada_b, norm1_g, norm2_g, ev_w_in, ev_b_in, ev_w_out, nsa_cmp_pe, nsa_cmp_w, od_w_in, od_b_in, od_w_out, ml_conv_w, ml_conv_b, ml_wq, ml_wk, ml_f_bias, ml_norm_g, mlp_w1, mlp_w2, final_g):
    n_even, n_odd = ev_w_in.shape[0], od_w_in.shape[0]
    d = x_prompt.shape[-1]
    bp, bs = x_prompt.shape[0], x_sample.shape[0]
    t_dec = x_sample.shape[1]
    t_pad = -(-t_dec // 8) * 8
    n_phys, page = cache_nsa_kv.shape[1], cache_nsa_kv.shape[2]

    ev_w, ev_b = _regroup_columns(ev_w_in, ev_b_in, _EVEN_SRC)
    od_w, od_b = _regroup_columns(od_w_in, od_b_in, _ODD_SRC)
    d_c = N_HEADS * DH
    p = dict(
        norm1_g=norm1_g, norm2_g=norm2_g, final_g=final_g,
        ev_w=ev_w, ev_b=ev_b, od_w=od_w, od_b=od_b,
        ev_w_out=ev_w_out.astype(BF16), od_w_out=od_w_out.astype(BF16),
        mlp_w1=mlp_w1.astype(BF16), mlp_w2=mlp_w2.astype(BF16),
        pe2=jnp.tile(nsa_cmp_pe, (1, 1, 1, HKV)),
        cmp_w=_block_diag(jnp.broadcast_to(nsa_cmp_w[:, :, :, None], nsa_cmp_w.shape[:3] + (HKV, DH, DH))).astype(BF16),
        conv_w=ml_conv_w, conv_b=ml_conv_b.reshape(n_odd, 1, d_c),
        wq_bd=_block_diag(ml_wq).astype(BF16), wk_bd=_block_diag(ml_wk).astype(BF16),
        fb_row=jnp.pad(ml_f_bias, ((0, 0), (N_HEADS, LANES - 2 * N_HEADS))).reshape(n_odd, 1, LANES),
        norm_g=ml_norm_g.reshape(n_odd, 1, d_c),
    )
    n_rows = -(-(bs + bp) // 8) * 8
    c_all = jnp.pad(jnp.concatenate([c_sample, c_prompt], axis=0), ((0, n_rows - bs - bp), (0, 0)))
    mod = ada_mod(c_all, ada_w.astype(BF16), ada_b).reshape(ada_w.shape[0], n_rows, 6, d)
    mod_s, mod_p = mod[:, :bs], mod[:, bs:bs + bp]

    y_p, ev_p, od_p = _run_group(p, x_prompt, mod_p, x_prompt.shape[1], None, None, 0, False)

    past = dict(
        nsa=cache_nsa_kv.reshape(n_even, n_phys, page, -1), win=state_nsa_win.reshape(state_nsa_win.shape[:3] + (-1,)),
        dsa=cache_dsa_kv.reshape(n_even, n_phys, page, -1), idx=cache_dsa_idx,
        c=state_mlstm_c, n=state_mlstm_n, m=state_mlstm_m, conv=state_mlstm_conv,
        moba=cache_moba_kv.reshape(n_odd, n_phys, page, -1))
    x_s = _pad_rows(x_sample, t_pad)
    y_s, ev_s, od_s = _run_group(p, x_s, mod_s, t_dec, past, page_table, page_table.shape[1] * page, True)

    nsa_kv_p, nsa_win_p, dsa_kv_p, dsa_idx_p = ev_p
    nsa_kv_s, nsa_win_s, dsa_kv_s, dsa_idx_s = ev_s
    ml_c_p, ml_n_p, ml_m_p, ml_conv_p, moba_kv_p = od_p
    ml_c_s, ml_n_s, ml_m_s, ml_conv_s, moba_kv_s = od_s
    return (y_p, y_s,
            nsa_kv_p, nsa_kv_s, nsa_win_p, nsa_win_s, dsa_kv_p, dsa_kv_s, dsa_idx_p, dsa_idx_s,
            ml_c_p, ml_c_s, ml_n_p, ml_n_s, ml_m_p, ml_m_s, ml_conv_p, ml_conv_s, moba_kv_p, moba_kv_s)
```

```python
import functools

import numpy as np
import jax
import jax.numpy as jnp
from jax import lax
from jax.experimental import pallas as pl
from jax.experimental.pallas import tpu as pltpu

F32 = jnp.float32
BF16 = jnp.bfloat16
I32 = jnp.int32

DH = 64
N_HEADS = 8
HKV = 2
GROUP = N_HEADS // HKV
CMP_LEN = 32
CMP_STRIDE = 16
SEL_BLOCK = 64
N_SEL = 16
WINDOW = 512
DSA_TOPK = 256
CONV_W = 4
CHUNK = 64
MOBA_BLOCK = 256
MOBA_TOPK = 3
EPS = 1e-6
NEG = -1e30
TINY = 1e-30
FORCE = 1e4
SCALE = DH ** -0.5
LANES = 128
INT_MIN = -2 ** 31

VMEM_LIMIT = 56 * 1024 * 1024


def _cparams(sem):
    return pltpu.CompilerParams(dimension_semantics=sem, vmem_limit_bytes=VMEM_LIMIT)


def _dot(a, b):
    return jnp.dot(a, b, preferred_element_type=F32)


def _dot_nt(a, b):
    return lax.dot_general(a, b, (((1,), (1,)), ((), ())), preferred_element_type=F32)


def _split3(x):
    hi = x.astype(BF16)
    r = x - hi.astype(F32)
    mid = r.astype(BF16)
    lo = (r - mid.astype(F32)).astype(BF16)
    return hi, mid, lo


def _ada_kernel(c_ref, w_ref, b_ref, o_ref):
    c = c_ref[...]
    cs = (c * jax.nn.sigmoid(c)).astype(BF16)
    o_ref[0] = _dot(cs, w_ref[0]) + b_ref[0]


def ada_mod(c_all, ada_w_bf, ada_b):
    n_layers, d, n = ada_w_bf.shape
    r = c_all.shape[0]
    tn = 1536
    return pl.pallas_call(
        _ada_kernel,
        grid=(n_layers, n // tn),
        in_specs=[pl.BlockSpec((r, d), lambda l, j: (0, 0)),
                  pl.BlockSpec((1, d, tn), lambda l, j: (l, 0, j)),
                  pl.BlockSpec((1, 1, tn), lambda l, j: (l, 0, j))],
        out_specs=pl.BlockSpec((1, r, tn), lambda l, j: (l, 0, j)),
        out_shape=jax.ShapeDtypeStruct((n_layers, r, n), F32),
        compiler_params=_cparams(("arbitrary", "arbitrary")),
        name="ada_mod",
    )(c_all, ada_w_bf, ada_b.reshape(n_layers, 1, n))


def _mod_chunk(mod_ref, k, per_row):
    return mod_ref[k] if per_row else mod_ref[0, k:k + 1, :]


def _mod_spec(per_row, tm, d, tiles_per_batch):
    if per_row:
        return pl.BlockSpec((6, tm, d), lambda i, *_: (0, i, 0))
    return pl.BlockSpec((1, 6, d), lambda i, *_: (i // tiles_per_batch, 0, 0))


def _norm_mod(x, g, shift, scale):
    y = x * lax.rsqrt(jnp.mean(x * x, axis=-1, keepdims=True) + EPS) * g
    return y * (1.0 + scale) + shift


def _kin_kernel(per_row, groups, x_ref, mod_ref, g_ref, w_ref, b_ref, *refs):
    outs, h_scr = refs[:-1], refs[-1]
    h_scr[...] = _norm_mod(x_ref[...], g_ref[...], _mod_chunk(mod_ref, 0, per_row),
                           _mod_chunk(mod_ref, 1, per_row)).astype(BF16)
    for (c0, wpad, wout), o_ref in zip(groups, outs):
        z = _dot(h_scr[...], w_ref[:, c0:c0 + wpad]) + b_ref[:, c0:c0 + wpad]
        o_ref[...] = z[:, :wout]


def k_in(x2d, mod, g, w_bf, b, groups, tm, tiles_per_batch, per_row):
    m, d = x2d.shape
    wp = w_bf.shape[1]
    return pl.pallas_call(
        functools.partial(_kin_kernel, per_row, groups),
        grid=(m // tm,),
        in_specs=[pl.BlockSpec((tm, d), lambda i: (i, 0)),
                  _mod_spec(per_row, tm, d, tiles_per_batch),
                  pl.BlockSpec((1, d), lambda i: (0, 0)),
                  pl.BlockSpec((d, wp), lambda i: (0, 0)),
                  pl.BlockSpec((1, wp), lambda i: (0, 0))],
        out_specs=[pl.BlockSpec((tm, wout), lambda i: (i, 0)) for _, _, wout in groups],
        out_shape=[jax.ShapeDtypeStruct((m, wout), F32) for _, _, wout in groups],
        scratch_shapes=[pltpu.VMEM((tm, d), BF16)],
        compiler_params=_cparams(("arbitrary",)),
        name="k_in",
    )(x2d, mod, g.reshape(1, d), w_bf, b.reshape(1, wp))


def _kout_kernel(per_row, n_a, *refs):
    a_refs = refs[:n_a]
    b_ref, x_ref, mod_ref, w_ref, o_ref = refs[n_a:]
    a = a_refs[0][...]
    for r in a_refs[1:]:
        a = a + r[...]
    half = a.shape[1]
    y = _dot(a.astype(BF16), w_ref[:half, :]) + _dot(b_ref[...].astype(BF16), w_ref[half:, :])
    o_ref[...] = x_ref[...] + _mod_chunk(mod_ref, 2, per_row) * y


def k_out(a_list, b2d, x2d, mod, w_bf, tm, tiles_per_batch, per_row):
    m, d = x2d.shape
    half = b2d.shape[1]
    n_a = len(a_list)
    row_spec = pl.BlockSpec((tm, half), lambda i: (i, 0))
    return pl.pallas_call(
        functools.partial(_kout_kernel, per_row, n_a),
        grid=(m // tm,),
        in_specs=[row_spec] * (n_a + 1) + [
            pl.BlockSpec((tm, d), lambda i: (i, 0)),
            _mod_spec(per_row, tm, d, tiles_per_batch),
            pl.BlockSpec((2 * half, d), lambda i: (0, 0))],
        out_specs=pl.BlockSpec((tm, d), lambda i: (i, 0)),
        out_shape=jax.ShapeDtypeStruct((m, d), F32),
        compiler_params=_cparams(("arbitrary",)),
        name="k_out",
    )(*a_list, b2d, x2d, mod, w_bf)


def _mlp_kernel(per_row, final, x_ref, mod_ref, g_ref, fg_ref, w1_ref, w2_ref, o_ref, h_scr, acc_scr):
    f = pl.program_id(1)

    @pl.when(f == 0)
    def _():
        h_scr[...] = _norm_mod(x_ref[...], g_ref[...], _mod_chunk(mod_ref, 3, per_row),
                               _mod_chunk(mod_ref, 4, per_row)).astype(BF16)
        acc_scr[...] = jnp.zeros_like(acc_scr)

    a = jnp.maximum(_dot(h_scr[...], w1_ref[...]), 0.0)
    acc_scr[...] += _dot((a * a).astype(BF16), w2_ref[...])

    @pl.when(f == pl.num_programs(1) - 1)
    def _():
        xn = x_ref[...] + _mod_chunk(mod_ref, 5, per_row) * acc_scr[...]
        if final:
            xn = xn * lax.rsqrt(jnp.mean(xn * xn, axis=-1, keepdims=True) + EPS) * fg_ref[...]
        o_ref[...] = xn


def k_mlp(x2d, mod, g, final_g, w1_bf, w2_bf, tm, tf, tiles_per_batch, per_row, final):
    m, d = x2d.shape
    dff = w1_bf.shape[1]
    return pl.pallas_call(
        functools.partial(_mlp_kernel, per_row, final),
        grid=(m // tm, dff // tf),
        in_specs=[pl.BlockSpec((tm, d), lambda i, f: (i, 0)),
                  _mod_spec(per_row, tm, d, tiles_per_batch),
                  pl.BlockSpec((1, d), lambda i, f: (0, 0)),
                  pl.BlockSpec((1, d), lambda i, f: (0, 0)),
                  pl.BlockSpec((d, tf), lambda i, f: (0, f)),
                  pl.BlockSpec((tf, d), lambda i, f: (f, 0))],
        out_specs=pl.BlockSpec((tm, d), lambda i, f: (i, 0)),
        out_shape=jax.ShapeDtypeStruct((m, d), F32),
        scratch_shapes=[pltpu.VMEM((tm, d), BF16), pltpu.VMEM((tm, d), F32)],
        compiler_params=_cparams(("arbitrary", "arbitrary")),
        name="k_mlp",
    )(x2d, mod, g.reshape(1, d), final_g.reshape(1, d), w1_bf, w2_bf)


MASK_BIG = 2e30


def _qz_block(q_ref, h, scale):
    pair = q_ref[0, :, (h // 2) * LANES:(h // 2 + 1) * LANES]
    if (h % 2) != (h // GROUP):
        pair = pltpu.roll(pair, DH, 1)
    lane = lax.broadcasted_iota(I32, pair.shape, 1)
    keep = (lane < DH) if h // GROUP == 0 else (lane >= DH)
    return jnp.where(keep, pair * scale, 0.0).astype(BF16)


def _assemble_heads(o_list):
    lane = lax.broadcasted_iota(I32, o_list[0].shape, 1)
    pairs = []
    for p in range(N_HEADS // 2):
        a, b = o_list[2 * p], o_list[2 * p + 1]
        if (2 * p) // GROUP != 0:
            a = pltpu.roll(a, DH, 1)
        if (2 * p + 1) // GROUP != 1:
            b = pltpu.roll(b, DH, 1)
        pairs.append(jnp.where(lane < DH, a, b))
    return jnp.concatenate(pairs, axis=1)


def _gate(g_ref, col):
    return jax.nn.sigmoid(g_ref[0, :, col:col + 1])


def _lanes(x, n):
    return x if n == LANES else jnp.concatenate([x] * (n // LANES), axis=1)


def _flash_init(m_scr, l_scr, acc_scr):
    m_scr[...] = jnp.full(m_scr.shape, NEG, F32)
    l_scr[...] = jnp.zeros(l_scr.shape, F32)
    acc_scr[...] = jnp.zeros(acc_scr.shape, F32)


def _flash_chunk(s_all, keep, v_bf, tq, m_scr, l_scr, acc_scr):
    ck = s_all.shape[1]
    ps, alphas = [], []
    for h in range(N_HEADS):
        r0 = h * tq
        s = s_all[r0:r0 + tq, :]
        if keep is not None:
            s = jnp.where(keep, s, -MASK_BIG)
        m_old = m_scr[r0:r0 + tq, :]
        m_new = jnp.maximum(m_old, jnp.max(s, axis=-1, keepdims=True))
        p = jnp.exp(s - _lanes(m_new, ck))
        alpha = jnp.exp(m_old - m_new)
        l_scr[r0:r0 + tq, :] = alpha * l_scr[r0:r0 + tq, :] + jnp.sum(p, axis=-1, keepdims=True)
        m_scr[r0:r0 + tq, :] = m_new
        ps.append(p.astype(BF16))
        alphas.append(alpha)
    acc_scr[...] = jnp.concatenate(alphas, axis=0) * acc_scr[...] + _dot(jnp.concatenate(ps, axis=0), v_bf)


def _flash_heads(l_scr, acc_scr, tq):
    return [acc_scr[h * tq:(h + 1) * tq, :] / jnp.maximum(l_scr[h * tq:(h + 1) * tq, :], TINY)
            for h in range(N_HEADS)]


def _sort_key(x):
    key = pltpu.bitcast(x + 0.0, I32)
    return jnp.where(key < 0, key ^ jnp.int32(0x7FFFFFFF), key)


_HALF_NEG_KEY = int(np.float32(NEG / 2).view(np.int32) ^ 0x7FFFFFFF)


def _topk_mask(s, k):
    r, n = s.shape
    key = _sort_key(s)

    def step(it, t):
        shift = 30 - 2 * it
        digit = jnp.zeros((r, LANES), I32)
        for c in (1, 2, 3):
            cand = t + lax.shift_left(jnp.int32(c), shift)
            cnt = jnp.sum((key >= _lanes(cand, n)).astype(F32), axis=-1, keepdims=True)
            digit = digit + (cnt >= k).astype(I32)
        return t + lax.shift_left(digit, shift)

    t = _lanes(lax.fori_loop(0, 16, step, jnp.full((r, LANES), INT_MIN, I32)), n)
    gt = key > t
    eq = key == t
    need = k - jnp.sum(gt.astype(F32), axis=-1, keepdims=True)
    lower = (lax.broadcasted_iota(I32, (n, n), 0) < lax.broadcasted_iota(I32, (n, n), 1))
    before = _dot(eq.astype(BF16), lower.astype(BF16))
    return gt | (eq & (before < need))


def _top_few_mask(s, k):
    idx = lax.broadcasted_iota(I32, s.shape, 1).astype(F32)
    sel = jnp.zeros(s.shape, jnp.bool_)
    cur = s
    for _ in range(k):
        m = jnp.max(cur, axis=-1, keepdims=True)
        first = jnp.min(jnp.where(cur == m, idx, 3e38), axis=-1, keepdims=True)
        pick = idx == first
        sel = sel | pick
        cur = jnp.where(pick, -3e38, cur)
    return sel


def _load_pages(pool_ref, layer, pt_ref, b, col0, new_ref, buf_ref, sems):
    n_pages = pt_ref.shape[1]
    page = pool_ref.shape[2]
    ncols = buf_ref.shape[1]

    def page_copy(p):
        return pltpu.make_async_copy(pool_ref.at[layer, pt_ref[b, p], :, pl.ds(col0, ncols)],
                                     buf_ref.at[pl.ds(p * page, page), :], sems.at[p])

    for p in range(n_pages):
        page_copy(p).start()
    tail = buf_ref.shape[0] - n_pages * page
    buf_ref[n_pages * page:n_pages * page + 8, :] = new_ref[0]
    buf_ref[n_pages * page + 8:, :] = jnp.zeros((tail - 8, ncols), F32)
    for p in range(n_pages):
        page_copy(p).wait()


def _paged_call(kernel_fn, paged, grid, in_specs, out_specs, out_shape, scratch, name, args, page_table=None):
    if not paged:
        return pl.pallas_call(
            kernel_fn, grid=grid, in_specs=in_specs, out_specs=out_specs, out_shape=out_shape,
            scratch_shapes=scratch, compiler_params=_cparams(("arbitrary",) * len(grid)), name=name)(*args)
    return pl.pallas_call(
        kernel_fn,
        grid_spec=pltpu.PrefetchScalarGridSpec(num_scalar_prefetch=1, grid=grid, in_specs=in_specs,
                                               out_specs=out_specs, scratch_shapes=scratch),
        out_shape=out_shape, compiler_params=_cparams(("arbitrary",) * len(grid)), name=name,
    )(page_table, *args)


def _kv_specs(paged, kv, cols, new):
    if not paged:
        lp = kv.shape[1]
        width = LANES if kv.shape[2] >= LANES else kv.shape[2]
        return ([pl.BlockSpec((1, lp, width), functools.partial(lambda c, bi, i: (bi, 0, c), c)) for c in cols],
                [kv] * len(cols))
    width = LANES if new.shape[2] >= LANES else new.shape[2]
    specs = [pl.BlockSpec(memory_space=pl.ANY)]
    specs += [pl.BlockSpec((1, 8, width), functools.partial(lambda c, bi, i, pt: (bi, 0, c), c)) for c in cols]
    return specs, [kv] + [new] * len(cols)


def _battn_kernel(cfg, *refs):
    tq, ck, nbp, mg, lp = cfg["tq"], cfg["ck"], cfg["nbp"], cfg["mask_group"], cfg["lp"]
    if cfg["paged"]:
        (pt_ref, q_ref, pool_ref, knew_ref, vnew_ref, bm_ref, g_ref, o_ref,
         qa_scr, m_scr, l_scr, acc_scr, k_buf, v_buf, sems) = refs
        b = pl.program_id(0)
        _load_pages(pool_ref, cfg["layer"], pt_ref, b, cfg["k_col"] * LANES, knew_ref, k_buf, sems.at[0])
        _load_pages(pool_ref, cfg["layer"], pt_ref, b, cfg["v_col"] * LANES, vnew_ref, v_buf, sems.at[1])
        get_k = lambda k0: k_buf[pl.ds(k0, ck), :]
        get_v = lambda k0: v_buf[pl.ds(k0, ck), :]
    else:
        q_ref, k_ref, v_ref, bm_ref, g_ref, o_ref, qa_scr, m_scr, l_scr, acc_scr = refs
        get_k = lambda k0: k_ref[0, pl.ds(k0, ck), :]
        get_v = lambda k0: v_ref[0, pl.ds(k0, ck), :]
    qbase = cfg["q0"] + pl.program_id(1) * tq
    for h in range(N_HEADS):
        g = h // mg
        qa_scr[h * tq:(h + 1) * tq, 0:LANES] = _qz_block(q_ref, h, SCALE)
        qa_scr[h * tq:(h + 1) * tq, LANES:LANES + nbp] = bm_ref[0, :, g * nbp:(g + 1) * nbp] - 1
    _flash_init(m_scr, l_scr, acc_scr)
    qpos = qbase + lax.broadcasted_iota(I32, (tq, 1), 0)
    last = jnp.minimum((qbase + tq - 1) // ck, lp // ck - 1)

    def chunk(c, causal):
        k0 = pl.multiple_of(c * ck, ck)
        blk = lax.shift_right_logical(k0 + lax.broadcasted_iota(I32, (ck, 1), 0), cfg["bshift"])
        bias = jnp.where(lax.broadcasted_iota(I32, (ck, nbp), 1) == blk, MASK_BIG, 0.0).astype(BF16)
        k_aug = jnp.concatenate([get_k(k0).astype(BF16), bias], axis=1)
        s_all = _dot_nt(qa_scr[...], k_aug)
        keep = ((k0 + lax.broadcasted_iota(I32, (1, ck), 1)) <= qpos) if causal else None
        _flash_chunk(s_all, keep, get_v(k0).astype(BF16), tq, m_scr, l_scr, acc_scr)

    def body(c, carry):
        chunk(c, False)
        return carry

    lax.fori_loop(0, last, body, 0)
    chunk(last, True)
    heads = _flash_heads(l_scr, acc_scr, tq)
    if cfg["gate_col"] is not None:
        heads = [o * _gate(g_ref, cfg["gate_col"] + h) for h, o in enumerate(heads)]
    o_ref[0] = _assemble_heads(heads)


def block_attention(q, kv, k_col, v_col, gates, bmask, *, tq, ck, q0, nbp, bshift, mask_group, gate_col,
                    lp=None, layer=None, page_table=None, new=None):
    b, t, dq = q.shape
    paged = page_table is not None
    lp = lp if paged else kv.shape[1]
    assert ck % tq == 0 and q0 % tq == 0 and lp % ck == 0
    cfg = dict(tq=tq, ck=ck, q0=q0, nbp=nbp, bshift=bshift, mask_group=mask_group, gate_col=gate_col,
               lp=lp, paged=paged, layer=layer, k_col=k_col, v_col=v_col)
    rows = N_HEADS * tq
    im = (lambda bi, i, pt: (bi, i, 0)) if paged else (lambda bi, i: (bi, i, 0))
    kv_specs, kv_args = _kv_specs(paged, kv, (k_col, v_col), new)
    scratch = [pltpu.VMEM((rows, LANES + nbp), BF16), pltpu.VMEM((rows, LANES), F32),
               pltpu.VMEM((rows, LANES), F32), pltpu.VMEM((rows, LANES), F32)]
    if paged:
        scratch += [pltpu.VMEM((lp, LANES), F32), pltpu.VMEM((lp, LANES), F32),
                    pltpu.SemaphoreType.DMA((2, page_table.shape[1]))]
    return _paged_call(
        functools.partial(_battn_kernel, cfg), paged, (b, t // tq),
        [pl.BlockSpec((1, tq, dq), im)] + kv_specs
        + [pl.BlockSpec((1, tq, bmask.shape[2]), im), pl.BlockSpec((1, tq, LANES), im)],
        pl.BlockSpec((1, tq, dq), im), jax.ShapeDtypeStruct((b, t, dq), F32), scratch, "attn_block",
        [q] + kv_args + [bmask, gates], page_table)


def _win_kernel(cfg, q_ref, k_ref, v_ref, g_ref, o_ref):
    tq, ckw, koff = cfg["tq"], cfg["ckw"], cfg["koff"]
    lp = k_ref.shape[1]
    qbase = cfg["q0"] + pl.program_id(1) * tq
    start = pl.multiple_of(jnp.clip(qbase - koff - WINDOW, 0, lp - ckw), 8)
    k_bf = k_ref[0, pl.ds(start, ckw), :].astype(BF16)
    v_bf = v_ref[0, pl.ds(start, ckw), :].astype(BF16)
    kpos = koff + start + lax.broadcasted_iota(I32, (1, ckw), 1)
    qpos = qbase + lax.broadcasted_iota(I32, (tq, 1), 0)
    mask = (kpos <= qpos) & (qpos - kpos < WINDOW) & (kpos >= 0)
    s_all = _dot_nt(jnp.concatenate([_qz_block(q_ref, h, SCALE) for h in range(N_HEADS)], axis=0), k_bf)
    es, ls = [], []
    for h in range(N_HEADS):
        s = jnp.where(mask, s_all[h * tq:(h + 1) * tq, :], NEG)
        e = jnp.where(mask, jnp.exp(s - jnp.max(s, axis=-1, keepdims=True)), 0.0)
        ls.append(jnp.maximum(jnp.sum(e, axis=-1, keepdims=True), TINY))
        es.append(e.astype(BF16))
    o_all = _dot(jnp.concatenate(es, axis=0), v_bf)
    heads = [o_all[h * tq:(h + 1) * tq, :] / ls[h] * _gate(g_ref, cfg["gate_col"] + h) for h in range(N_HEADS)]
    o_ref[0] = _assemble_heads(heads)


def window_attention(q, kv, gates, *, tq, ckw, q0, koff, gate_col):
    b, t, dq = q.shape
    lp = kv.shape[1]
    cfg = dict(tq=tq, ckw=ckw, q0=q0, koff=koff, gate_col=gate_col)
    return pl.pallas_call(
        functools.partial(_win_kernel, cfg),
        grid=(b, t // tq),
        in_specs=[pl.BlockSpec((1, tq, dq), lambda bi, i: (bi, i, 0)),
                  pl.BlockSpec((1, lp, LANES), lambda bi, i: (bi, 0, 0)),
                  pl.BlockSpec((1, lp, LANES), lambda bi, i: (bi, 0, 1)),
                  pl.BlockSpec((1, tq, LANES), lambda bi, i: (bi, i, 0))],
        out_specs=pl.BlockSpec((1, tq, dq), lambda bi, i: (bi, i, 0)),
        out_shape=jax.ShapeDtypeStruct((b, t, dq), F32),
        compiler_params=_cparams(("arbitrary", "arbitrary")),
        name="attn_window",
    )(q, kv, kv, gates)


def _compress_kernel(cfg, *refs):
    n_groups = cfg["n_groups"]
    if cfg["paged"]:
        pt_ref, pool_ref, new_ref, pe_ref, w_ref, o_ref, a_scr, b_scr, x_buf, sems = refs
        col0 = pl.multiple_of(pl.program_id(1) * LANES, LANES)
        _load_pages(pool_ref, cfg["layer"], pt_ref, pl.program_id(0), col0, new_ref, x_buf, sems)
        rows = lambda l: x_buf[pl.ds(l, n_groups, stride=CMP_STRIDE), :]
    else:
        x_ref, pe_ref, w_ref, o_ref, a_scr, b_scr = refs
        rows = lambda l: x_ref[0, pl.ds(l, n_groups, stride=CMP_STRIDE), :]
    acc_a = jnp.zeros((n_groups, LANES), F32)
    acc_b = jnp.zeros((n_groups, LANES), F32)
    for l in range(CMP_STRIDE):
        x = rows(l)
        acc_a += _dot((x + pe_ref[0, l:l + 1, :]).astype(BF16), w_ref[0, l])
        acc_b += _dot((x + pe_ref[0, CMP_STRIDE + l:CMP_STRIDE + l + 1, :]).astype(BF16),
                      w_ref[0, CMP_STRIDE + l])
    a_scr[...] = acc_a
    b_scr[0:n_groups, :] = acc_b
    b_scr[n_groups:n_groups + 8, :] = jnp.zeros((8, LANES), F32)
    o_ref[0] = jnp.zeros(o_ref.shape[1:], F32)
    o_ref[0, 0:n_groups, :] = a_scr[...] + b_scr[pl.ds(1, n_groups), :]


def nsa_compress(kv, pe2, w_bd, ncp, *, lp=None, layer=None, page_table=None, new=None):
    paged = page_table is not None
    b = new.shape[0] if paged else kv.shape[0]
    lp = lp if paged else kv.shape[1]
    n_groups = lp // CMP_STRIDE
    cfg = dict(n_groups=n_groups, paged=paged, layer=layer)
    if paged:
        kv_specs = [pl.BlockSpec(memory_space=pl.ANY), pl.BlockSpec((1, 8, LANES), lambda bi, j, pt: (bi, 0, j))]
        kv_args = [kv, new]
        c3 = lambda bi, j, pt: (j, 0, 0)
        c4 = lambda bi, j, pt: (j, 0, 0, 0)
        om = lambda bi, j, pt: (bi, 0, j)
    else:
        kv_specs = [pl.BlockSpec((1, lp, LANES), lambda bi, j: (bi, 0, j))]
        kv_args = [kv]
        c3 = lambda bi, j: (j, 0, 0)
        c4 = lambda bi, j: (j, 0, 0, 0)
        om = lambda bi, j: (bi, 0, j)
    scratch = [pltpu.VMEM((n_groups, LANES), F32), pltpu.VMEM((n_groups + 8, LANES), F32)]
    if paged:
        scratch += [pltpu.VMEM((lp, LANES), F32), pltpu.SemaphoreType.DMA((page_table.shape[1],))]
    return _paged_call(
        functools.partial(_compress_kernel, cfg), paged, (b, 2),
        kv_specs + [pl.BlockSpec((1, CMP_LEN, LANES), c3), pl.BlockSpec((1, CMP_LEN, LANES, LANES), c4)],
        pl.BlockSpec((1, ncp, LANES), om), jax.ShapeDtypeStruct((b, ncp, 2 * LANES), F32), scratch,
        "nsa_compress", kv_args + [pe2, w_bd], page_table)


def _nsa_select_kernel(cfg, q_ref, c_ref, g_ref, o_ref, bm_ref):
    tq, nc, nsp = cfg["tq"], cfg["nc"], cfg["nsp"]
    ncp = c_ref.shape[1]
    qbase = cfg["q0"] + pl.program_id(1) * tq
    qpos = qbase + lax.broadcasted_iota(I32, (tq, 1), 0)
    kc = c_ref[0, :, 0:LANES].astype(BF16)
    vc = c_ref[0, :, LANES:2 * LANES].astype(BF16)
    n = lax.broadcasted_iota(I32, (1, ncp), 1)
    mask = (n * CMP_STRIDE + (CMP_LEN - 1) <= qpos) & (n < nc)
    s_all = _dot_nt(jnp.concatenate([_qz_block(q_ref, h, SCALE) for h in range(N_HEADS)], axis=0), kc)
    ps, psum = [], [None] * HKV
    for h in range(N_HEADS):
        s = jnp.where(mask, s_all[h * tq:(h + 1) * tq, :], NEG)
        e = jnp.where(mask, jnp.exp(s - jnp.max(s, axis=-1, keepdims=True)), 0.0)
        p = e / jnp.maximum(jnp.sum(e, axis=-1, keepdims=True), TINY)
        ps.append(p.astype(BF16))
        psum[h // GROUP] = p if psum[h // GROUP] is None else psum[h // GROUP] + p
    o_all = _dot(jnp.concatenate(ps, axis=0), vc)
    o_ref[0] = _assemble_heads([o_all[h * tq:(h + 1) * tq, :] * _gate(g_ref, cfg["gate_col"] + h)
                                for h in range(N_HEADS)])
    cs = lax.broadcasted_iota(I32, (ncp, nsp), 0) * CMP_STRIDE
    ss = lax.broadcasted_iota(I32, (ncp, nsp), 1) * SEL_BLOCK
    overlap = ((cs < ss + SEL_BLOCK) & (cs + CMP_LEN > ss)).astype(BF16)
    hi, mid, lo = _split3(jnp.concatenate(psum, axis=0))
    imp = _dot(hi, overlap) + _dot(mid, overlap) + _dot(lo, overlap)
    j = lax.broadcasted_iota(I32, (1, nsp), 1)
    blk = jnp.concatenate([qpos // SEL_BLOCK] * HKV, axis=0)
    forced = (j == 0) | (j == blk) | (j == blk - 1)
    score = jnp.where(j <= blk, imp + jnp.where(forced, FORCE, 0.0), NEG)
    sel = (_topk_mask(score, N_SEL) & (score > NEG / 2)).astype(BF16)
    for g in range(HKV):
        bm_ref[0, :, g * nsp:(g + 1) * nsp] = sel[g * tq:(g + 1) * tq, :]


def nsa_select(q, cmp_kv, gates, *, tq, q0, nc, nsp, gate_col):
    b, t, dq = q.shape
    ncp = cmp_kv.shape[1]
    cfg = dict(tq=tq, q0=q0, nc=nc, nsp=nsp, gate_col=gate_col)
    return pl.pallas_call(
        functools.partial(_nsa_select_kernel, cfg),
        grid=(b, t // tq),
        in_specs=[pl.BlockSpec((1, tq, dq), lambda bi, i: (bi, i, 0)),
                  pl.BlockSpec((1, ncp, 2 * LANES), lambda bi, i: (bi, 0, 0)),
                  pl.BlockSpec((1, tq, LANES), lambda bi, i: (bi, i, 0))],
        out_specs=[pl.BlockSpec((1, tq, dq), lambda bi, i: (bi, i, 0)),
                   pl.BlockSpec((1, tq, HKV * nsp), lambda bi, i: (bi, i, 0))],
        out_shape=[jax.ShapeDtypeStruct((b, t, dq), F32),
                   jax.ShapeDtypeStruct((b, t, HKV * nsp), BF16)],
        compiler_params=_cparams(("arbitrary", "arbitrary")),
        name="nsa_select",
    )(q, cmp_kv, gates)


def _dsa_kernel(cfg, *refs):
    tq, ck, n_top, lp = cfg["tq"], cfg["ck"], cfg["n_top"], cfg["lp"]
    if cfg["paged"]:
        (pt_ref, q_ref, iq_ref, g_ref, ipool_ref, inew_ref, pool_ref, knew_ref, vnew_ref, o_ref,
         qz_scr, iq_scr, iw_scr, key_scr, low_scr, m_scr, l_scr, acc_scr, i_buf, k_buf, v_buf, sems) = refs
        b = pl.program_id(0)
        _load_pages(ipool_ref, cfg["layer"], pt_ref, b, 0, inew_ref, i_buf, sems.at[0])
        _load_pages(pool_ref, cfg["layer"], pt_ref, b, 0, knew_ref, k_buf, sems.at[1])
        _load_pages(pool_ref, cfg["layer"], pt_ref, b, LANES, vnew_ref, v_buf, sems.at[2])
        get_i = lambda k0: i_buf[pl.ds(k0, ck), :]
        get_k = lambda k0: k_buf[pl.ds(k0, ck), :]
        get_v = lambda k0: v_buf[pl.ds(k0, ck), :]
    else:
        (q_ref, iq_ref, g_ref, ik_ref, k_ref, v_ref, o_ref,
         qz_scr, iq_scr, iw_scr, key_scr, low_scr, m_scr, l_scr, acc_scr) = refs
        get_i = lambda k0: ik_ref[0, pl.ds(k0, ck), :]
        get_k = lambda k0: k_ref[0, pl.ds(k0, ck), :]
        get_v = lambda k0: v_ref[0, pl.ds(k0, ck), :]
    qbase = cfg["q0"] + pl.program_id(1) * tq
    qpos = qbase + lax.broadcasted_iota(I32, (tq, 1), 0)
    hi_chunk = jnp.minimum((qbase + tq - 1) // ck + 1, lp // ck)
    for h in range(N_HEADS):
        qz_scr[h * tq:(h + 1) * tq, :] = _qz_block(q_ref, h, SCALE)
        pair = iq_ref[0, :, (h // 2) * LANES:(h // 2 + 1) * LANES]
        if h % 2:
            pair = pltpu.roll(pair, DH, 1)
        iq_scr[h * tq:(h + 1) * tq, :] = pair[:, :DH].astype(BF16)
        iw_scr[h * tq:(h + 1) * tq, :] = jnp.broadcast_to(g_ref[0, :, h:h + 1], (tq, LANES))
    low_scr[...] = (lax.broadcasted_iota(I32, (ck, ck), 0)
                    < lax.broadcasted_iota(I32, (ck, ck), 1)).astype(BF16)

    def score_body(c, carry):
        k0 = pl.multiple_of(c * ck, ck)
        s_all = jnp.maximum(_dot_nt(iq_scr[...], get_i(k0).astype(BF16)), 0.0)
        sc = s_all[0:tq, :] * _lanes(iw_scr[0:tq, :], ck)
        for h in range(1, N_HEADS):
            sc = sc + s_all[h * tq:(h + 1) * tq, :] * _lanes(iw_scr[h * tq:(h + 1) * tq, :], ck)
        kpos = k0 + lax.broadcasted_iota(I32, (1, ck), 1)
        key_scr[c] = _sort_key(jnp.where(kpos <= qpos, sc, NEG))
        return carry

    lax.fori_loop(0, hi_chunk, score_body, 0)

    def count(pred, thr):
        thr_w = _lanes(thr, ck)

        def body(c, a):
            hit = pred(key_scr[c], thr_w).astype(F32)
            for j in range(ck // LANES):
                a = a + hit[:, j * LANES:(j + 1) * LANES]
            return a
        a = lax.fori_loop(0, hi_chunk, body, jnp.zeros((tq, LANES), F32))
        return jnp.sum(a, axis=-1, keepdims=True)

    def bit_step(it, t):
        cand = t + lax.shift_left(jnp.int32(1), 31 - it)
        return jnp.where(count(lambda kk, th: kk >= th, cand) >= n_top, cand, t)

    t = lax.fori_loop(0, 32, bit_step, jnp.full((tq, LANES), INT_MIN, I32))
    need = n_top - count(lambda kk, th: kk > th, t)
    t_w = _lanes(t, ck)
    _flash_init(m_scr, l_scr, acc_scr)

    def attn_body(c, before):
        k0 = pl.multiple_of(c * ck, ck)
        kk = key_scr[c]
        eq = kk == t_w
        rank_eq = before + _dot(eq.astype(BF16), low_scr[...])
        mask = ((kk > t_w) | (eq & (rank_eq < need))) & (kk > _HALF_NEG_KEY)
        s_all = _dot_nt(qz_scr[...], get_k(k0).astype(BF16))
        _flash_chunk(s_all, mask, get_v(k0).astype(BF16), tq, m_scr, l_scr, acc_scr)
        return before + jnp.sum(eq.astype(F32), axis=-1, keepdims=True)

    lax.fori_loop(0, hi_chunk, attn_body, jnp.zeros((tq, 1), F32))
    o_ref[0] = _assemble_heads(_flash_heads(l_scr, acc_scr, tq))


def dsa_attention(q, iq, misc, ik, kv, *, tq, ck, q0, n_top, lp=None, layer=None, page_table=None,
                  ik_new=None, kv_new=None):
    b, t, dq = q.shape
    paged = page_table is not None
    lp = lp if paged else kv.shape[1]
    cfg = dict(tq=tq, ck=ck, q0=q0, n_top=n_top, lp=lp, paged=paged, layer=layer)
    im = (lambda bi, i, pt: (bi, i, 0)) if paged else (lambda bi, i: (bi, i, 0))
    qspec = pl.BlockSpec((1, tq, dq), im)
    i_specs, i_args = _kv_specs(paged, ik, (0,), ik_new)
    kv_specs, kv_args = _kv_specs(paged, kv, (0, 1), kv_new)
    rows = N_HEADS * tq
    scratch = [pltpu.VMEM((rows, LANES), BF16), pltpu.VMEM((rows, DH), BF16), pltpu.VMEM((rows, LANES), F32),
               pltpu.VMEM((lp // ck, tq, ck), I32), pltpu.VMEM((ck, ck), BF16),
               pltpu.VMEM((rows, LANES), F32), pltpu.VMEM((rows, LANES), F32), pltpu.VMEM((rows, LANES), F32)]
    if paged:
        scratch += [pltpu.VMEM((lp, DH), F32), pltpu.VMEM((lp, LANES), F32), pltpu.VMEM((lp, LANES), F32),
                    pltpu.SemaphoreType.DMA((3, page_table.shape[1]))]
    return _paged_call(
        functools.partial(_dsa_kernel, cfg), paged, (b, t // tq),
        [qspec, qspec, pl.BlockSpec((1, tq, LANES), im)] + i_specs + kv_specs,
        qspec, jax.ShapeDtypeStruct((b, t, dq), F32), scratch, "dsa_attention",
        [q, iq, misc] + i_args + kv_args, page_table)


def _moba_select_kernel(cfg, *refs):
    tq, nblk, nbp, n_top, lp = cfg["tq"], cfg["nblk"], cfg["nbp"], cfg["n_top"], cfg["lp"]
    if cfg["paged"]:
        pt_ref, q_ref, pool_ref, knew_ref, bm_ref, km_scr, k_buf, sems = refs
        _load_pages(pool_ref, cfg["layer"], pt_ref, pl.program_id(0), 0, knew_ref, k_buf, sems)
        get_rows = lambda r0, r1: k_buf[r0:r1, :]
    else:
        q_ref, k_ref, bm_ref, km_scr = refs
        get_rows = lambda r0, r1: k_ref[0, r0:r1, :]
    qpos = cfg["q0"] + pl.program_id(1) * tq + lax.broadcasted_iota(I32, (tq, 1), 0)
    km_scr[...] = jnp.zeros(km_scr.shape, F32)
    for j in range(nblk):
        r1 = min((j + 1) * MOBA_BLOCK, lp)
        km_scr[j:j + 1, :] = jnp.sum(get_rows(j * MOBA_BLOCK, r1), axis=0, keepdims=True) * (1.0 / MOBA_BLOCK)
    km = km_scr[...].astype(BF16)
    own = jnp.concatenate([qpos // MOBA_BLOCK] * N_HEADS, axis=0)
    j = lax.broadcasted_iota(I32, (1, nbp), 1)
    s = _dot_nt(jnp.concatenate([_qz_block(q_ref, h, 1.0) for h in range(N_HEADS)], axis=0), km)
    s = jnp.where(j < own, s, NEG)
    sel = (j == own)
    if n_top > 0:
        sel = sel | (_top_few_mask(s, n_top) & (s > NEG / 2))
    sel = sel.astype(BF16)
    for h in range(N_HEADS):
        bm_ref[0, :, h * nbp:(h + 1) * nbp] = sel[h * tq:(h + 1) * tq, :]


def moba_select(q, kv, *, tq, q0, nblk, nbp, lp=None, layer=None, page_table=None, new=None):
    b, t, dq = q.shape
    paged = page_table is not None
    lp = lp if paged else kv.shape[1]
    cfg = dict(tq=tq, q0=q0, nblk=nblk, nbp=nbp, n_top=min(MOBA_TOPK, nblk - 1), lp=lp, paged=paged, layer=layer)
    im = (lambda bi, i, pt: (bi, i, 0)) if paged else (lambda bi, i: (bi, i, 0))
    kv_specs, kv_args = _kv_specs(paged, kv, (0,), new)
    scratch = [pltpu.VMEM((nbp, LANES), F32)]
    if paged:
        scratch += [pltpu.VMEM((lp, LANES), F32), pltpu.SemaphoreType.DMA((page_table.shape[1],))]
    return _paged_call(
        functools.partial(_moba_select_kernel, cfg), paged, (b, t // tq),
        [pl.BlockSpec((1, tq, dq), im)] + kv_specs,
        pl.BlockSpec((1, tq, N_HEADS * nbp), im), jax.ShapeDtypeStruct((b, t, N_HEADS * nbp), BF16),
        scratch, "moba_select", [q] + kv_args, page_table)


def _log_sigmoid(x):
    return jnp.minimum(x, 0.0) - jnp.log(1.0 + jnp.exp(-jnp.abs(x)))


def _dot3_rhs(a_bf, x):
    hi, mid, lo = _split3(x)
    return _dot(a_bf, hi) + _dot(a_bf, mid) + _dot(a_bf, lo)


def _dot3_lhs(x, b_bf):
    hi, mid, lo = _split3(x)
    return _dot(hi, b_bf) + _dot(mid, b_bf) + _dot(lo, b_bf)


def _mlstm_kernel(cfg, u_ref, v_ref, og_ref, g_ref, cin_ref, ct0_ref, n0_ref, m0_ref,
                  cw_ref, cb_ref, wq_ref, wk_ref, fb_ref, ng_ref,
                  h_out, ct_out, n_out, m_out,
                  ubuf, q_scr, k_scr, v_scr, g_scr, h_scr, ct_scr, n_scr, m_scr, hm_scr):
    tc, tcp, t_valid = cfg["tc"], cfg["tcp"], cfg["t_valid"]
    d_c = N_HEADS * DH
    i = pl.program_id(1)

    @pl.when(i == 0)
    def _():
        ubuf[0:8, :] = cin_ref[0]
        ct_scr[...] = ct0_ref[0]
        n_scr[...] = n0_ref[0]
        m_scr[...] = m0_ref[0]
        hm_scr[...] = (lax.broadcasted_iota(I32, (d_c, d_c), 0) // DH
                       == lax.broadcasted_iota(I32, (d_c, d_c), 1) // DH).astype(F32)

    if tc < tcp:
        ubuf[8:, :] = jnp.zeros((tcp, d_c), F32)
        v_scr[...] = jnp.zeros((tcp, d_c), F32)
        g_scr[...] = jnp.zeros((tcp, LANES), F32)
    ubuf[8:8 + tc, :] = u_ref[0]
    v_scr[0:tc, :] = v_ref[0]
    g_scr[0:tc, :] = g_ref[0]
    conv = ubuf[pl.ds(CONV_W + 1, tcp), :] * cw_ref[0:1, :]
    for j in range(1, CONV_W):
        conv = conv + ubuf[pl.ds(CONV_W + 1 + j, tcp), :] * cw_ref[j:j + 1, :]
    conv = conv + cb_ref[...]
    uc = (conv * jax.nn.sigmoid(conv)).astype(BF16)
    q_scr[...] = _dot(uc, wq_ref[...])
    k_scr[...] = _dot(uc, wk_ref[...]) * SCALE

    lane = lax.broadcasted_iota(I32, (1, LANES), 1)
    head_lane = lane < N_HEADS
    t_io = lax.broadcasted_iota(I32, (CHUNK, 1), 0)
    causal = lane <= t_io
    tri = (lax.broadcasted_iota(I32, (CHUNK, CHUNK), 1)
           <= lax.broadcasted_iota(I32, (CHUNK, CHUNK), 0)).astype(BF16)
    tri_t = (lax.broadcasted_iota(I32, (LANES, LANES), 0)
             <= lax.broadcasted_iota(I32, (LANES, LANES), 1)).astype(BF16)
    expand = (lax.broadcasted_iota(I32, (LANES, d_c), 0)
              == lax.broadcasted_iota(I32, (LANES, d_c), 1) // DH).astype(BF16)
    zeros_gate = jnp.zeros((CHUNK, LANES), F32)
    zeros_feat = jnp.zeros((CHUNK, d_c), F32)

    def chunk_body(c, carry):
        r0 = pl.multiple_of(c * CHUNK, CHUNK)
        hm = hm_scr[...]
        g = g_scr[pl.ds(r0, CHUNK), :]
        valid = (i * tc + r0 + t_io) < t_valid
        ig = jnp.where(head_lane, jnp.where(valid, g, NEG), 0.0)
        lf = pltpu.roll(_log_sigmoid(g + fb_ref[...]), LANES - N_HEADS, 1)
        lf = jnp.where(head_lane & valid, lf, 0.0)
        b_col = _dot3_rhs(tri, lf)
        ig_t = jnp.concatenate([ig, zeros_gate], axis=0).T[0:8, :]
        lf_t = jnp.concatenate([lf, zeros_gate], axis=0).T[0:8, :]
        rowterm = ig_t - _dot3_lhs(lf_t, tri_t)
        m_row = m_scr[...]
        dws, iws, emts = [], [], []
        for h in range(N_HEADS):
            bc = b_col[:, h:h + 1]
            dlog = jnp.where(causal, bc + rowterm[h:h + 1, :], NEG)
            inter = bc + m_row[:, h:h + 1]
            m_t = jnp.maximum(inter, jnp.max(dlog, axis=-1, keepdims=True))
            dws.append(jnp.exp(dlog - m_t))
            iws.append(jnp.exp(inter - m_t))
            emts.append(jnp.exp(-m_t))
        dw = jnp.concatenate(dws, axis=0)
        iw = jnp.concatenate(iws, axis=0)
        emt = jnp.concatenate(emts, axis=0)
        q_c = q_scr[pl.ds(r0, CHUNK), :]
        k_c = k_scr[pl.ds(r0, CHUNK), :]
        v_c = v_scr[pl.ds(r0, CHUNK), :]
        qz = jnp.concatenate([q_c] * N_HEADS, axis=0) * hm
        qz_bf = qz.astype(BF16)
        k_pad = jnp.concatenate([k_c, zeros_feat], axis=0)
        v_pad = jnp.concatenate([v_c, zeros_feat], axis=0).astype(BF16)
        qkw = _dot_nt(qz_bf, k_pad.astype(BF16)) * dw
        intra = _dot(qkw.astype(BF16), v_pad)
        inter_z = _dot(qz_bf, ct_scr[...].astype(BF16))
        num = iw * inter_z + intra * hm
        den = iw * jnp.sum(qz * n_scr[...], axis=-1, keepdims=True) + jnp.sum(qkw, axis=-1, keepdims=True)
        hz = num / jnp.maximum(jnp.abs(den), emt)
        h_c = hz[0:CHUNK, :]
        for h in range(1, N_HEADS):
            h_c = h_c + hz[h * CHUNK:(h + 1) * CHUNK, :]
        h_scr[pl.ds(r0, CHUNK), :] = h_c
        b_last = b_col[CHUNK - 1:CHUNK, :]
        m_new = jnp.maximum(b_last + m_row, jnp.max(b_last - b_col + ig, axis=0, keepdims=True))
        decay = jnp.where(head_lane, jnp.exp(b_last + m_row - m_new), 0.0)
        ws = jnp.where(head_lane, jnp.exp(b_last - b_col + ig - m_new), 0.0)
        wide = _dot3_lhs(jnp.concatenate([ws, jnp.broadcast_to(decay, (8, LANES))], axis=0), expand)
        w8, decay_w = wide[0:CHUNK, :], wide[CHUNK:CHUNK + 1, :]
        vw_pad = jnp.concatenate([v_c * w8, zeros_feat], axis=0).astype(BF16)
        k_t = jnp.concatenate([k_pad[:, j * LANES:(j + 1) * LANES].T for j in range(d_c // LANES)], axis=0)
        ct_scr[...] = decay_w * ct_scr[...] + _dot(k_t.astype(BF16), vw_pad) * hm
        n_scr[...] = decay_w * n_scr[...] + jnp.sum(k_c * w8, axis=0, keepdims=True)
        m_scr[...] = m_new
        return carry

    lax.fori_loop(0, tcp // CHUNK, chunk_body, 0)
    h_all = h_scr[...]
    hi, mid, lo = _split3(h_all * h_all)
    hm_bf = hm_scr[...].astype(BF16)
    ms = (_dot(hi, hm_bf) + _dot(mid, hm_bf) + _dot(lo, hm_bf)) * (1.0 / DH)
    hc = h_all * lax.rsqrt(ms + EPS) * ng_ref[...]
    h_out[0] = jax.nn.sigmoid(og_ref[0]) * hc[0:tc, :]
    ubuf[0:8, :] = ubuf[tc:tc + 8, :]

    @pl.when(i == pl.num_programs(1) - 1)
    def _():
        ct_out[0] = ct_scr[...]
        n_out[0] = n_scr[...]
        m_out[0] = m_scr[...]


def mlstm(u, v, og, misc, conv_in, ct0, n0, m0, conv_w, conv_b, wq_bd, wk_bd, fb_row, norm_g, *, tc, t_valid):
    b, t, d_c = u.shape
    tcp = max(tc, CHUNK)
    cfg = dict(tc=tc, tcp=tcp, t_valid=t_valid)
    row = pl.BlockSpec((1, tc, d_c), lambda bi, i: (bi, i, 0))
    const = lambda shape: pl.BlockSpec(shape, lambda bi, i: (0,) * len(shape))
    per_b = lambda shape: pl.BlockSpec((1,) + shape, lambda bi, i: (bi,) + (0,) * len(shape))
    return pl.pallas_call(
        functools.partial(_mlstm_kernel, cfg),
        grid=(b, t // tc),
        in_specs=[row, row, row, pl.BlockSpec((1, tc, LANES), lambda bi, i: (bi, i, 0)),
                  per_b((8, d_c)), per_b((d_c, d_c)), per_b((1, d_c)), per_b((1, LANES)),
                  const((CONV_W, d_c)), const((1, d_c)), const((d_c, d_c)), const((d_c, d_c)),
                  const((1, LANES)), const((1, d_c))],
        out_specs=[row, per_b((d_c, d_c)), per_b((1, d_c)), per_b((1, LANES))],
        out_shape=[jax.ShapeDtypeStruct((b, t, d_c), F32),
                   jax.ShapeDtypeStruct((b, d_c, d_c), F32),
                   jax.ShapeDtypeStruct((b, 1, d_c), F32),
                   jax.ShapeDtypeStruct((b, 1, LANES), F32)],
        scratch_shapes=[pltpu.VMEM((tcp + 8, d_c), F32),
                        pltpu.VMEM((tcp, d_c), F32), pltpu.VMEM((tcp, d_c), F32), pltpu.VMEM((tcp, d_c), F32),
                        pltpu.VMEM((tcp, LANES), F32), pltpu.VMEM((tcp, d_c), F32),
                        pltpu.VMEM((d_c, d_c), F32), pltpu.VMEM((1, d_c), F32), pltpu.VMEM((1, LANES), F32),
                        pltpu.VMEM((d_c, d_c), F32)],
        compiler_params=_cparams(("arbitrary", "arbitrary")),
        name="mlstm",
    )(u, v, og, misc, conv_in, ct0, n0, m0, conv_w, conv_b, wq_bd, wk_bd, fb_row, norm_g)


def _regroup_columns(w, b, pieces):
    n_src = w.shape[-1]
    idx = []
    for s, wd, wp in pieces:
        idx += list(range(s, s + wd)) + [n_src] * (wp - wd)
    idx = np.asarray(idx, np.int32)
    w_ext = jnp.concatenate([w, jnp.zeros(w.shape[:-1] + (1,), w.dtype)], axis=-1)
    b_ext = jnp.concatenate([b, jnp.zeros(b.shape[:-1] + (1,), b.dtype)], axis=-1)
    return jnp.take(w_ext, idx, axis=-1).astype(BF16), jnp.take(b_ext, idx, axis=-1)


def _block_diag(w):
    h, a, b = w.shape[-3:]
    eye = jnp.eye(h, dtype=w.dtype)
    out = w[..., :, :, None, :] * eye[:, None, :, None]
    return out.reshape(w.shape[:-3] + (h * a, h * b))


_EVEN_SRC = [(0, 512, 512), (512, 512, 512), (1024, 256, 256), (1304, 512, 512), (1816, 256, 256),
             (2072, 512, 512), (2584, 64, 128), (2648, 8, 8), (1280, 24, 120)]
_EVEN_GROUPS = [(0, 512, 512), (512, 512, 512), (1024, 256, 256), (1280, 512, 512), (1792, 256, 256),
                (2048, 512, 512), (2560, 128, 64), (2688, 128, 128)]
_ODD_SRC = [(0, 512, 512), (512, 512, 512), (1024, 512, 512), (1552, 512, 512), (2064, 256, 256),
            (1536, 16, 128)]
_ODD_GROUPS = [(0, 512, 512), (512, 512, 512), (1024, 512, 512), (1536, 512, 512), (2048, 256, 256),
               (2304, 128, 128)]
_GATE_COL = N_HEADS


def _pick_chunk(n, limit):
    return max(c for c in range(LANES, limit + 1, LANES) if n % c == 0)


def _pad_rows(a, rows):
    return jnp.pad(a, ((0, 0), (0, rows - a.shape[1])) + ((0, 0),) * (a.ndim - 2))


def _new_rows(a, t_real):
    keep = (jnp.arange(a.shape[1]) < t_real)[None, :, None]
    return jnp.where(keep, a, 0.0)[:, :8]


def _even_mixer(p, i, outs, bsz, t, t_real, past, page_table, q0):
    a_q, kv4, win, b_q, b_kv, b_iq, b_ik, misc = [o.reshape(bsz, t, -1) for o in outs]
    decode = past is not None
    if decode:
        page = past["nsa"].shape[2]
        lk = page_table.shape[1] * page + t_real
        lp = (page_table.shape[1] + 1) * page
        win_buf = past["win"][i]
        ckw = -(-(win_buf.shape[1] + t_real) // LANES) * LANES
        win_all = _pad_rows(jnp.concatenate([win_buf, win[:, :t_real]], axis=1), ckw)
        koff = q0 - win_buf.shape[1]
        tq, ck = t, _pick_chunk(lp, 640)
        pg = dict(lp=lp, layer=i, page_table=page_table)
        nsa_src, dsa_src, idx_src = past["nsa"], past["dsa"], past["idx"]
        nsa_new = dict(new=_new_rows(kv4, t_real))
        dsa_new = dict(ik_new=_new_rows(b_ik, t_real), kv_new=_new_rows(b_kv, t_real))
    else:
        lk, lp, koff, win_all = t, t, 0, win
        tq, ck, ckw = 128, 512, WINDOW + 128
        pg, nsa_new, dsa_new = {}, {}, {}
        nsa_src, dsa_src, idx_src = kv4, b_kv, b_ik
    n16 = -(-lk // CMP_STRIDE)
    ncp = -(-(lp // CMP_STRIDE) // LANES) * LANES
    nsp = -(-(-(-lk // SEL_BLOCK)) // LANES) * LANES
    cmp_kv = nsa_compress(nsa_src, p["pe2"][i], p["cmp_w"][i], ncp, **pg, **nsa_new)
    o_cmp, bmask = nsa_select(a_q, cmp_kv, misc, tq=tq, q0=q0, nc=n16 - 1, nsp=nsp, gate_col=_GATE_COL)
    o_sel = block_attention(a_q, nsa_src, 2, 3, misc, bmask, tq=tq, ck=ck, q0=q0, nbp=nsp, bshift=6,
                            mask_group=GROUP, gate_col=_GATE_COL + N_HEADS, **pg, **nsa_new)
    o_win = window_attention(a_q, win_all, misc, tq=tq, ckw=ckw, q0=q0, koff=koff,
                             gate_col=_GATE_COL + 2 * N_HEADS)
    o_dsa = dsa_attention(b_q, b_iq, misc, idx_src, dsa_src, tq=tq, ck=ck, q0=q0,
                          n_top=min(DSA_TOPK, lk // 4), **pg, **dsa_new)
    if decode:
        win_state = jnp.concatenate([win_buf, win[:, :t_real]], axis=1)[:, -win_buf.shape[1]:]
    else:
        win_state = win[:, -min(WINDOW, t):]
    state = (kv4[:, :t_real].reshape(bsz, t_real, 4, HKV, DH),
             win_state.reshape(bsz, -1, 2, HKV, DH),
             b_kv[:, :t_real].reshape(bsz, t_real, 2, HKV, DH),
             b_ik[:, :t_real])
    return [o_cmp, o_sel, o_win], o_dsa, state


def _odd_mixer(p, i, outs, bsz, t, t_real, past, page_table, q0):
    u, c_v, c_o, d_q, d_kv, misc = [o.reshape(bsz, t, -1) for o in outs]
    d_c = N_HEADS * DH
    decode = past is not None
    idx = np.arange(N_HEADS)
    if decode:
        page = past["moba"].shape[2]
        lk = page_table.shape[1] * page + t_real
        lp = (page_table.shape[1] + 1) * page
        pg = dict(lp=lp, layer=i, page_table=page_table, new=_new_rows(d_kv, t_real))
        moba_src = past["moba"]
        c_t = jnp.swapaxes(past["c"][i], -1, -2)
        ct0 = jnp.zeros((bsz, N_HEADS, DH, N_HEADS, DH), F32).at[:, idx, :, idx, :].set(
            jnp.moveaxis(c_t, 1, 0)).reshape(bsz, d_c, d_c)
        n0 = past["n"][i].reshape(bsz, 1, d_c)
        m0 = jnp.pad(past["m"][i], ((0, 0), (0, LANES - N_HEADS)))[:, None, :]
        conv_prev = past["conv"][i]
        tq, ck, tc = t, _pick_chunk(lp, 640), t
    else:
        moba_src, lk, lp, pg = d_kv, t, t, {}
        ct0 = jnp.zeros((bsz, d_c, d_c), F32)
        n0 = jnp.zeros((bsz, 1, d_c), F32)
        m0 = jnp.zeros((bsz, 1, LANES), F32)
        conv_prev = jnp.zeros((bsz, CONV_W - 1, d_c), F32)
        tq, ck, tc = 128, 512, 512
    conv_in = jnp.concatenate([jnp.zeros((bsz, 8 - (CONV_W - 1), d_c), F32), conv_prev], axis=1)
    hc, ct1, n1, m1 = mlstm(u, c_v, c_o, misc, conv_in, ct0, n0, m0, p["conv_w"][i], p["conv_b"][i],
                            p["wq_bd"][i], p["wk_bd"][i], p["fb_row"][i], p["norm_g"][i], tc=tc, t_valid=t_real)
    nblk = -(-lk // MOBA_BLOCK)
    bmask = moba_select(d_q, moba_src, tq=tq, q0=q0, nblk=nblk, nbp=LANES, **pg)
    o_d = block_attention(d_q, moba_src, 0, 1, misc, bmask, tq=tq, ck=ck, q0=q0, nbp=LANES,
                          bshift=8, mask_group=1, gate_col=None, **pg)
    c1 = jnp.swapaxes(jnp.moveaxis(ct1.reshape(bsz, N_HEADS, DH, N_HEADS, DH)[:, idx, :, idx, :], 0, 1), -1, -2)
    conv_state = jnp.concatenate([conv_prev, u[:, :t_real]], axis=1)[:, -(CONV_W - 1):]
    state = (c1, n1.reshape(bsz, N_HEADS, DH), m1[:, 0, :N_HEADS], conv_state,
             d_kv[:, :t_real].reshape(bsz, t_real, 2, HKV, DH))
    return [hc], o_d, state


def _run_group(p, x, mod, t_real, past, page_table, q0, per_row):
    bsz, t, d = x.shape
    m = bsz * t
    tm = min(512, m)
    tmm = min(1024, m)
    tiles = max(t // tm, 1)
    tiles_mlp = max(t // tmm, 1)
    x2d = x.reshape(m, d)
    n_layers = p["mlp_w1"].shape[0]
    ev_states, od_states = [], []
    for l in range(n_layers):
        i = l // 2
        if per_row:
            mod_l = jnp.moveaxis(jnp.repeat(mod[l], t, axis=0), 1, 0)
        else:
            mod_l = mod[l]
        if l % 2 == 0:
            outs = k_in(x2d, mod_l, p["norm1_g"][l], p["ev_w"][i], p["ev_b"][i], _EVEN_GROUPS, tm, tiles, per_row)
            a_list, b_o, st = _even_mixer(p, i, outs, bsz, t, t_real, past, page_table, q0)
            ev_states.append(st)
            w_out = p["ev_w_out"][i]
        else:
            outs = k_in(x2d, mod_l, p["norm1_g"][l], p["od_w"][i], p["od_b"][i], _ODD_GROUPS, tm, tiles, per_row)
            a_list, b_o, st = _odd_mixer(p, i, outs, bsz, t, t_real, past, page_table, q0)
            od_states.append(st)
            w_out = p["od_w_out"][i]
        x2d = k_out([a.reshape(m, -1) for a in a_list], b_o.reshape(m, -1), x2d, mod_l, w_out, tm, tiles, per_row)
        x2d = k_mlp(x2d, mod_l, p["norm2_g"][l], p["final_g"], p["mlp_w1"][l], p["mlp_w2"][l], tmm,
                    min(1024, p["mlp_w1"].shape[2]), tiles_mlp, per_row, final=(l == n_layers - 1))
    stack = lambda states: tuple(jnp.stack(a) for a in zip(*states))
    return x2d.reshape(bsz, t, d)[:, :t_real], stack(ev_states), stack(od_states)


def kernel(x_prompt, x_sample, cache_nsa_kv, state_nsa_win, cache_dsa_kv, cache_dsa_idx, state_mlstm_c, state_mlstm_n, state_mlstm_m, state_mlstm_conv, cache_moba_kv, page_table, c_prompt, c_sample, ada_w, ada_b, norm1_g, norm2_g, ev_w_in, ev_b_in, ev_w_out, nsa_cmp_pe, nsa_cmp_w, od_w_in, od_b_in, od_w_out, ml_conv_w, ml_conv_b, ml_wq, ml_wk, ml_f_bias, ml_norm_g, mlp_w1, mlp_w2, final_g):
    n_even, n_odd = ev_w_in.shape[0], od_w_in.shape[0]
    d = x_prompt.shape[-1]
    bp, bs = x_prompt.shape[0], x_sample.shape[0]
    t_dec = x_sample.shape[1]
    t_pad = -(-t_dec // 8) * 8
    n_phys, page = cache_nsa_kv.shape[1], cache_nsa_kv.shape[2]

    ev_w, ev_b = _regroup_columns(ev_w_in, ev_b_in, _EVEN_SRC)
    od_w, od_b = _regroup_columns(od_w_in, od_b_in, _ODD_SRC)
    d_c = N_HEADS * DH
    p = dict(
        norm1_g=norm1_g, norm2_g=norm2_g, final_g=final_g,
        ev_w=ev_w, ev_b=ev_b, od_w=od_w, od_b=od_b,
        ev_w_out=ev_w_out.astype(BF16), od_w_out=od_w_out.astype(BF16),
        mlp_w1=mlp_w1.astype(BF16), mlp_w2=mlp_w2.astype(BF16),
        pe2=jnp.tile(nsa_cmp_pe, (1, 1, 1, HKV)),
        cmp_w=_block_diag(jnp.broadcast_to(nsa_cmp_w[:, :, :, None], nsa_cmp_w.shape[:3] + (HKV, DH, DH))).astype(BF16),
        conv_w=ml_conv_w, conv_b=ml_conv_b.reshape(n_odd, 1, d_c),
        wq_bd=_block_diag(ml_wq).astype(BF16), wk_bd=_block_diag(ml_wk).astype(BF16),
        fb_row=jnp.pad(ml_f_bias, ((0, 0), (N_HEADS, LANES - 2 * N_HEADS))).reshape(n_odd, 1, LANES),
        norm_g=ml_norm_g.reshape(n_odd, 1, d_c),
    )
    n_rows = -(-(bs + bp) // 8) * 8
    c_all = jnp.pad(jnp.concatenate([c_sample, c_prompt], axis=0), ((0, n_rows - bs - bp), (0, 0)))
    mod = ada_mod(c_all, ada_w.astype(BF16), ada_b).reshape(ada_w.shape[0], n_rows, 6, d)
    mod_s, mod_p = mod[:, :bs], mod[:, bs:bs + bp]

    y_p, ev_p, od_p = _run_group(p, x_prompt, mod_p, x_prompt.shape[1], None, None, 0, False)

    past = dict(
        nsa=cache_nsa_kv.reshape(n_even, n_phys, page, -1), win=state_nsa_win.reshape(state_nsa_win.shape[:3] + (-1,)),
        dsa=cache_dsa_kv.reshape(n_even, n_phys, page, -1), idx=cache_dsa_idx,
        c=state_mlstm_c, n=state_mlstm_n, m=state_mlstm_m, conv=state_mlstm_conv,
        moba=cache_moba_kv.reshape(n_odd, n_phys, page, -1))
    x_s = _pad_rows(x_sample, t_pad)
    y_s, ev_s, od_s = _run_group(p, x_s, mod_s, t_dec, past, page_table, page_table.shape[1] * page, True)

    nsa_kv_p, nsa_win_p, dsa_kv_p, dsa_idx_p = ev_p
    nsa_kv_s, nsa_win_s, dsa_kv_s, dsa_idx_s = ev_s
    ml_c_p, ml_n_p, ml_m_p, ml_conv_p, moba_kv_p = od_p
    ml_c_s, ml_n_s, ml_m_s, ml_conv_s, moba_kv_s = od_s
    return (y_p, y_s,
            nsa_kv_p, nsa_kv_s, nsa_win_p, nsa_win_s, dsa_kv_p, dsa_kv_s, dsa_idx_p, dsa_idx_s,
            ml_c_p, ml_c_s, ml_n_p, ml_n_s, ml_m_p, ml_m_s, ml_conv_p, ml_conv_s, moba_kv_p, moba_kv_s)
```

```python
import functools

import numpy as np
import jax
import jax.numpy as jnp
from jax import lax
from jax.experimental import pallas as pl
from jax.experimental.pallas import tpu as pltpu

F32 = jnp.float32
BF16 = jnp.bfloat16
I32 = jnp.int32

DH = 64
N_HEADS = 8
HKV = 2
GROUP = N_HEADS // HKV
CMP_LEN = 32
CMP_STRIDE = 16
SEL_BLOCK = 64
N_SEL = 16
WINDOW = 512
DSA_TOPK = 256
CONV_W = 4
CHUNK = 64
MOBA_BLOCK = 256
MOBA_TOPK = 3
EPS = 1e-6
NEG = -1e30
TINY = 1e-30
FORCE = 1e4
SCALE = DH ** -0.5
LANES = 128
INT_MIN = -2 ** 31

VMEM_LIMIT = 56 * 1024 * 1024


def _cparams(sem):
    return pltpu.CompilerParams(dimension_semantics=sem, vmem_limit_bytes=VMEM_LIMIT)


def _dot(a, b):
    return jnp.dot(a, b, preferred_element_type=F32)


def _dot_nt(a, b):
    return lax.dot_general(a, b, (((1,), (1,)), ((), ())), preferred_element_type=F32)


def _split3(x):
    hi = x.astype(BF16)
    r = x - hi.astype(F32)
    mid = r.astype(BF16)
    lo = (r - mid.astype(F32)).astype(BF16)
    return hi, mid, lo


def _ada_kernel(c_ref, w_ref, b_ref, o_ref):
    c = c_ref[...]
    cs = (c * jax.nn.sigmoid(c)).astype(BF16)
    o_ref[0] = _dot(cs, w_ref[0]) + b_ref[0]


def ada_mod(c_all, ada_w_bf, ada_b):
    n_layers, d, n = ada_w_bf.shape
    r = c_all.shape[0]
    tn = 1536
    return pl.pallas_call(
        _ada_kernel,
        grid=(n_layers, n // tn),
        in_specs=[pl.BlockSpec((r, d), lambda l, j: (0, 0)),
                  pl.BlockSpec((1, d, tn), lambda l, j: (l, 0, j)),
                  pl.BlockSpec((1, 1, tn), lambda l, j: (l, 0, j))],
        out_specs=pl.BlockSpec((1, r, tn), lambda l, j: (l, 0, j)),
        out_shape=jax.ShapeDtypeStruct((n_layers, r, n), F32),
        compiler_params=_cparams(("arbitrary", "arbitrary")),
        name="ada_mod",
    )(c_all, ada_w_bf, ada_b.reshape(n_layers, 1, n))


def _mod_chunk(mod_ref, k, per_row):
    return mod_ref[k] if per_row else mod_ref[0, k:k + 1, :]


def _mod_spec(per_row, tm, d, tiles_per_batch):
    if per_row:
        return pl.BlockSpec((6, tm, d), lambda i, *_: (0, i, 0))
    return pl.BlockSpec((1, 6, d), lambda i, *_: (i // tiles_per_batch, 0, 0))


def _norm_mod(x, g, shift, scale):
    y = x * lax.rsqrt(jnp.mean(x * x, axis=-1, keepdims=True) + EPS) * g
    return y * (1.0 + scale) + shift


def _kin_kernel(per_row, groups, x_ref, mod_ref, g_ref, w_ref, b_ref, *refs):
    outs, h_scr = refs[:-1], refs[-1]
    h_scr[...] = _norm_mod(x_ref[...], g_ref[...], _mod_chunk(mod_ref, 0, per_row),
                           _mod_chunk(mod_ref, 1, per_row)).astype(BF16)
    for (c0, wpad, wout), o_ref in zip(groups, outs):
        z = _dot(h_scr[...], w_ref[:, c0:c0 + wpad]) + b_ref[:, c0:c0 + wpad]
        o_ref[...] = z[:, :wout]


def k_in(x2d, mod, g, w_bf, b, groups, tm, tiles_per_batch, per_row):
    m, d = x2d.shape
    wp = w_bf.shape[1]
    return pl.pallas_call(
        functools.partial(_kin_kernel, per_row, groups),
        grid=(m // tm,),
        in_specs=[pl.BlockSpec((tm, d), lambda i: (i, 0)),
                  _mod_spec(per_row, tm, d, tiles_per_batch),
                  pl.BlockSpec((1, d), lambda i: (0, 0)),
                  pl.BlockSpec((d, wp), lambda i: (0, 0)),
                  pl.BlockSpec((1, wp), lambda i: (0, 0))],
        out_specs=[pl.BlockSpec((tm, wout), lambda i: (i, 0)) for _, _, wout in groups],
        out_shape=[jax.ShapeDtypeStruct((m, wout), F32) for _, _, wout in groups],
        scratch_shapes=[pltpu.VMEM((tm, d), BF16)],
        compiler_params=_cparams(("arbitrary",)),
        name="k_in",
    )(x2d, mod, g.reshape(1, d), w_bf, b.reshape(1, wp))


def _kout_kernel(per_row, n_a, *refs):
    a_refs = refs[:n_a]
    b_ref, x_ref, mod_ref, w_ref, o_ref = refs[n_a:]
    a = a_refs[0][...]
    for r in a_refs[1:]:
        a = a + r[...]
    half = a.shape[1]
    y = _dot(a.astype(BF16), w_ref[:half, :]) + _dot(b_ref[...].astype(BF16), w_ref[half:, :])
    o_ref[...] = x_ref[...] + _mod_chunk(mod_ref, 2, per_row) * y


def k_out(a_list, b2d, x2d, mod, w_bf, tm, tiles_per_batch, per_row):
    m, d = x2d.shape
    half = b2d.shape[1]
    n_a = len(a_list)
    row_spec = pl.BlockSpec((tm, half), lambda i: (i, 0))
    return pl.pallas_call(
        functools.partial(_kout_kernel, per_row, n_a),
        grid=(m // tm,),
        in_specs=[row_spec] * (n_a + 1) + [
            pl.BlockSpec((tm, d), lambda i: (i, 0)),
            _mod_spec(per_row, tm, d, tiles_per_batch),
            pl.BlockSpec((2 * half, d), lambda i: (0, 0))],
        out_specs=pl.BlockSpec((tm, d), lambda i: (i, 0)),
        out_shape=jax.ShapeDtypeStruct((m, d), F32),
        compiler_params=_cparams(("arbitrary",)),
        name="k_out",
    )(*a_list, b2d, x2d, mod, w_bf)


def _mlp_kernel(per_row, final, x_ref, mod_ref, g_ref, fg_ref, w1_ref, w2_ref, o_ref, h_scr, acc_scr):
    f = pl.program_id(1)

    @pl.when(f == 0)
    def _():
        h_scr[...] = _norm_mod(x_ref[...], g_ref[...], _mod_chunk(mod_ref, 3, per_row),
                               _mod_chunk(mod_ref, 4, per_row)).astype(BF16)
        acc_scr[...] = jnp.zeros_like(acc_scr)

    a = jnp.maximum(_dot(h_scr[...], w1_ref[...]), 0.0)
    acc_scr[...] += _dot((a * a).astype(BF16), w2_ref[...])

    @pl.when(f == pl.num_programs(1) - 1)
    def _():
        xn = x_ref[...] + _mod_chunk(mod_ref, 5, per_row) * acc_scr[...]
        if final:
            xn = xn * lax.rsqrt(jnp.mean(xn * xn, axis=-1, keepdims=True) + EPS) * fg_ref[...]
        o_ref[...] = xn


def k_mlp(x2d, mod, g, final_g, w1_bf, w2_bf, tm, tf, tiles_per_batch, per_row, final):
    m, d = x2d.shape
    dff = w1_bf.shape[1]
    return pl.pallas_call(
        functools.partial(_mlp_kernel, per_row, final),
        grid=(m // tm, dff // tf),
        in_specs=[pl.BlockSpec((tm, d), lambda i, f: (i, 0)),
                  _mod_spec(per_row, tm, d, tiles_per_batch),
                  pl.BlockSpec((1, d), lambda i, f: (0, 0)),
                  pl.BlockSpec((1, d), lambda i, f: (0, 0)),
                  pl.BlockSpec((d, tf), lambda i, f: (0, f)),
                  pl.BlockSpec((tf, d), lambda i, f: (f, 0))],
        out_specs=pl.BlockSpec((tm, d), lambda i, f: (i, 0)),
        out_shape=jax.ShapeDtypeStruct((m, d), F32),
        scratch_shapes=[pltpu.VMEM((tm, d), BF16), pltpu.VMEM((tm, d), F32)],
        compiler_params=_cparams(("arbitrary", "arbitrary")),
        name="k_mlp",
    )(x2d, mod, g.reshape(1, d), final_g.reshape(1, d), w1_bf, w2_bf)


MASK_BIG = 2e30


def _qz_block(q_ref, h, scale):
    pair = q_ref[0, :, (h // 2) * LANES:(h // 2 + 1) * LANES]
    if (h % 2) != (h // GROUP):
        pair = pltpu.roll(pair, DH, 1)
    lane = lax.broadcasted_iota(I32, pair.shape, 1)
    keep = (lane < DH) if h // GROUP == 0 else (lane >= DH)
    return jnp.where(keep, pair * scale, 0.0).astype(BF16)


def _assemble_heads(o_list):
    lane = lax.broadcasted_iota(I32, o_list[0].shape, 1)
    pairs = []
    for p in range(N_HEADS // 2):
        a, b = o_list[2 * p], o_list[2 * p + 1]
        if (2 * p) // GROUP != 0:
            a = pltpu.roll(a, DH, 1)
        if (2 * p + 1) // GROUP != 1:
            b = pltpu.roll(b, DH, 1)
        pairs.append(jnp.where(lane < DH, a, b))
    return jnp.concatenate(pairs, axis=1)


def _gate(g_ref, col):
    return jax.nn.sigmoid(g_ref[0, :, col:col + 1])


def _lanes(x, n):
    return x if n == LANES else jnp.concatenate([x] * (n // LANES), axis=1)


def _flash_init(m_scr, l_scr, acc_scr):
    m_scr[...] = jnp.full(m_scr.shape, NEG, F32)
    l_scr[...] = jnp.zeros(l_scr.shape, F32)
    acc_scr[...] = jnp.zeros(acc_scr.shape, F32)


def _flash_chunk(s_all, keep, v_bf, tq, m_scr, l_scr, acc_scr):
    ck = s_all.shape[1]
    ps, alphas = [], []
    for h in range(N_HEADS):
        r0 = h * tq
        s = s_all[r0:r0 + tq, :]
        if keep is not None:
            s = jnp.where(keep, s, -MASK_BIG)
        m_old = m_scr[r0:r0 + tq, :]
        m_new = jnp.maximum(m_old, jnp.max(s, axis=-1, keepdims=True))
        p = jnp.exp(s - _lanes(m_new, ck))
        alpha = jnp.exp(m_old - m_new)
        l_scr[r0:r0 + tq, :] = alpha * l_scr[r0:r0 + tq, :] + jnp.sum(p, axis=-1, keepdims=True)
        m_scr[r0:r0 + tq, :] = m_new
        ps.append(p.astype(BF16))
        alphas.append(alpha)
    acc_scr[...] = jnp.concatenate(alphas, axis=0) * acc_scr[...] + _dot(jnp.concatenate(ps, axis=0), v_bf)


def _flash_heads(l_scr, acc_scr, tq):
    return [acc_scr[h * tq:(h + 1) * tq, :] / jnp.maximum(l_scr[h * tq:(h + 1) * tq, :], TINY)
            for h in range(N_HEADS)]


def _sort_key(x):
    key = pltpu.bitcast(x + 0.0, I32)
    return jnp.where(key < 0, key ^ jnp.int32(0x7FFFFFFF), key)


_HALF_NEG_KEY = int(np.float32(NEG / 2).view(np.int32) ^ 0x7FFFFFFF)


def _topk_mask(s, k):
    r, n = s.shape
    key = _sort_key(s)

    def step(it, t):
        shift = 30 - 2 * it
        digit = jnp.zeros((r, LANES), I32)
        for c in (1, 2, 3):
            cand = t + lax.shift_left(jnp.int32(c), shift)
            cnt = jnp.sum((key >= _lanes(cand, n)).astype(F32), axis=-1, keepdims=True)
            digit = digit + (cnt >= k).astype(I32)
        return t + lax.shift_left(digit, shift)

    t = _lanes(lax.fori_loop(0, 16, step, jnp.full((r, LANES), INT_MIN, I32)), n)
    gt = key > t
    eq = key == t
    need = k - jnp.sum(gt.astype(F32), axis=-1, keepdims=True)
    lower = (lax.broadcasted_iota(I32, (n, n), 0) < lax.broadcasted_iota(I32, (n, n), 1))
    before = _dot(eq.astype(BF16), lower.astype(BF16))
    return gt | (eq & (before < need))


def _top_few_mask(s, k):
    idx = lax.broadcasted_iota(I32, s.shape, 1).astype(F32)
    sel = jnp.zeros(s.shape, jnp.bool_)
    cur = s
    for _ in range(k):
        m = jnp.max(cur, axis=-1, keepdims=True)
        first = jnp.min(jnp.where(cur == m, idx, 3e38), axis=-1, keepdims=True)
        pick = idx == first
        sel = sel | pick
        cur = jnp.where(pick, -3e38, cur)
    return sel


def _battn_kernel(cfg, q_ref, k_ref, v_ref, bm_ref, g_ref, o_ref, qa_scr, m_scr, l_scr, acc_scr):
    tq, ck, nbp, mg = cfg["tq"], cfg["ck"], cfg["nbp"], cfg["mask_group"]
    lp = k_ref.shape[1]
    get_k = lambda k0: k_ref[0, pl.ds(k0, ck), :]
    get_v = lambda k0: v_ref[0, pl.ds(k0, ck), :]
    qbase = cfg["q0"] + pl.program_id(1) * tq
    for h in range(N_HEADS):
        g = h // mg
        qa_scr[h * tq:(h + 1) * tq, 0:LANES] = _qz_block(q_ref, h, SCALE)
        qa_scr[h * tq:(h + 1) * tq, LANES:LANES + nbp] = bm_ref[0, :, g * nbp:(g + 1) * nbp] - 1
    _flash_init(m_scr, l_scr, acc_scr)
    qpos = qbase + lax.broadcasted_iota(I32, (tq, 1), 0)
    last = jnp.minimum((qbase + tq - 1) // ck, lp // ck - 1)

    def chunk(c, causal):
        k0 = pl.multiple_of(c * ck, ck)
        blk = lax.shift_right_logical(k0 + lax.broadcasted_iota(I32, (ck, 1), 0), cfg["bshift"])
        bias = jnp.where(lax.broadcasted_iota(I32, (ck, nbp), 1) == blk, MASK_BIG, 0.0).astype(BF16)
        k_aug = jnp.concatenate([get_k(k0).astype(BF16), bias], axis=1)
        s_all = _dot_nt(qa_scr[...], k_aug)
        keep = ((k0 + lax.broadcasted_iota(I32, (1, ck), 1)) <= qpos) if causal else None
        _flash_chunk(s_all, keep, get_v(k0).astype(BF16), tq, m_scr, l_scr, acc_scr)

    def body(c, carry):
        chunk(c, False)
        return carry

    lax.fori_loop(0, last, body, 0)
    chunk(last, True)
    heads = _flash_heads(l_scr, acc_scr, tq)
    if cfg["gate_col"] is not None:
        heads = [o * _gate(g_ref, cfg["gate_col"] + h) for h, o in enumerate(heads)]
    o_ref[0] = _assemble_heads(heads)


def block_attention(q, kv, k_col, v_col, gates, bmask, *, tq, ck, q0, nbp, bshift, mask_group, gate_col):
    b, t, dq = q.shape
    lp = kv.shape[1]
    assert ck % tq == 0 and q0 % tq == 0 and lp % ck == 0
    cfg = dict(tq=tq, ck=ck, q0=q0, nbp=nbp, bshift=bshift, mask_group=mask_group, gate_col=gate_col)
    rows = N_HEADS * tq
    im = lambda bi, i: (bi, i, 0)
    return pl.pallas_call(
        functools.partial(_battn_kernel, cfg),
        grid=(b, t // tq),
        in_specs=[pl.BlockSpec((1, tq, dq), im),
                  pl.BlockSpec((1, lp, LANES), lambda bi, i: (bi, 0, k_col)),
                  pl.BlockSpec((1, lp, LANES), lambda bi, i: (bi, 0, v_col)),
                  pl.BlockSpec((1, tq, bmask.shape[2]), im), pl.BlockSpec((1, tq, LANES), im)],
        out_specs=pl.BlockSpec((1, tq, dq), im),
        out_shape=jax.ShapeDtypeStruct((b, t, dq), F32),
        scratch_shapes=[pltpu.VMEM((rows, LANES + nbp), BF16), pltpu.VMEM((rows, LANES), F32),
                        pltpu.VMEM((rows, LANES), F32), pltpu.VMEM((rows, LANES), F32)],
        compiler_params=_cparams(("arbitrary", "arbitrary")),
        name="attn_block",
    )(q, kv, kv, bmask, gates)


def _win_kernel(cfg, q_ref, k_ref, v_ref, g_ref, o_ref):
    tq, ckw, koff = cfg["tq"], cfg["ckw"], cfg["koff"]
    lp = k_ref.shape[1]
    qbase = cfg["q0"] + pl.program_id(1) * tq
    start = pl.multiple_of(jnp.clip(qbase - koff - WINDOW, 0, lp - ckw), 8)
    k_bf = k_ref[0, pl.ds(start, ckw), :].astype(BF16)
    v_bf = v_ref[0, pl.ds(start, ckw), :].astype(BF16)
    kpos = koff + start + lax.broadcasted_iota(I32, (1, ckw), 1)
    qpos = qbase + lax.broadcasted_iota(I32, (tq, 1), 0)
    mask = (kpos <= qpos) & (qpos - kpos < WINDOW) & (kpos >= 0)
    s_all = _dot_nt(jnp.concatenate([_qz_block(q_ref, h, SCALE) for h in range(N_HEADS)], axis=0), k_bf)
    es, ls = [], []
    for h in range(N_HEADS):
        s = jnp.where(mask, s_all[h * tq:(h + 1) * tq, :], NEG)
        e = jnp.where(mask, jnp.exp(s - jnp.max(s, axis=-1, keepdims=True)), 0.0)
        ls.append(jnp.maximum(jnp.sum(e, axis=-1, keepdims=True), TINY))
        es.append(e.astype(BF16))
    o_all = _dot(jnp.concatenate(es, axis=0), v_bf)
    heads = [o_all[h * tq:(h + 1) * tq, :] / ls[h] * _gate(g_ref, cfg["gate_col"] + h) for h in range(N_HEADS)]
    o_ref[0] = _assemble_heads(heads)


def window_attention(q, kv, gates, *, tq, ckw, q0, koff, gate_col):
    b, t, dq = q.shape
    lp = kv.shape[1]
    cfg = dict(tq=tq, ckw=ckw, q0=q0, koff=koff, gate_col=gate_col)
    return pl.pallas_call(
        functools.partial(_win_kernel, cfg),
        grid=(b, t // tq),
        in_specs=[pl.BlockSpec((1, tq, dq), lambda bi, i: (bi, i, 0)),
                  pl.BlockSpec((1, lp, LANES), lambda bi, i: (bi, 0, 0)),
                  pl.BlockSpec((1, lp, LANES), lambda bi, i: (bi, 0, 1)),
                  pl.BlockSpec((1, tq, LANES), lambda bi, i: (bi, i, 0))],
        out_specs=pl.BlockSpec((1, tq, dq), lambda bi, i: (bi, i, 0)),
        out_shape=jax.ShapeDtypeStruct((b, t, dq), F32),
        compiler_params=_cparams(("arbitrary", "arbitrary")),
        name="attn_window",
    )(q, kv, kv, gates)


def _compress_kernel(n_groups, x_ref, pe_ref, w_ref, o_ref, a_scr, b_scr):
    acc_a = jnp.zeros((n_groups, LANES), F32)
    acc_b = jnp.zeros((n_groups, LANES), F32)
    for l in range(CMP_STRIDE):
        x = x_ref[0, pl.ds(l, n_groups, stride=CMP_STRIDE), :]
        acc_a += _dot((x + pe_ref[0, l:l + 1, :]).astype(BF16), w_ref[0, l])
        acc_b += _dot((x + pe_ref[0, CMP_STRIDE + l:CMP_STRIDE + l + 1, :]).astype(BF16),
                      w_ref[0, CMP_STRIDE + l])
    a_scr[...] = acc_a
    b_scr[0:n_groups, :] = acc_b
    b_scr[n_groups:n_groups + 8, :] = jnp.zeros((8, LANES), F32)
    o_ref[0] = jnp.zeros(o_ref.shape[1:], F32)
    o_ref[0, 0:n_groups, :] = a_scr[...] + b_scr[pl.ds(1, n_groups), :]


def nsa_compress(kv, pe2, w_bd, ncp):
    b, lp, _ = kv.shape
    n_groups = lp // CMP_STRIDE
    return pl.pallas_call(
        functools.partial(_compress_kernel, n_groups),
        grid=(b, 2),
        in_specs=[pl.BlockSpec((1, lp, LANES), lambda bi, j: (bi, 0, j)),
                  pl.BlockSpec((1, CMP_LEN, LANES), lambda bi, j: (j, 0, 0)),
                  pl.BlockSpec((1, CMP_LEN, LANES, LANES), lambda bi, j: (j, 0, 0, 0))],
        out_specs=pl.BlockSpec((1, ncp, LANES), lambda bi, j: (bi, 0, j)),
        out_shape=jax.ShapeDtypeStruct((b, ncp, 2 * LANES), F32),
        scratch_shapes=[pltpu.VMEM((n_groups, LANES), F32), pltpu.VMEM((n_groups + 8, LANES), F32)],
        compiler_params=_cparams(("arbitrary", "arbitrary")),
        name="nsa_compress",
    )(kv, pe2, w_bd)


def _nsa_select_kernel(cfg, q_ref, c_ref, g_ref, o_ref, bm_ref):
    tq, nc, nsp = cfg["tq"], cfg["nc"], cfg["nsp"]
    ncp = c_ref.shape[1]
    qbase = cfg["q0"] + pl.program_id(1) * tq
    qpos = qbase + lax.broadcasted_iota(I32, (tq, 1), 0)
    kc = c_ref[0, :, 0:LANES].astype(BF16)
    vc = c_ref[0, :, LANES:2 * LANES].astype(BF16)
    n = lax.broadcasted_iota(I32, (1, ncp), 1)
    mask = (n * CMP_STRIDE + (CMP_LEN - 1) <= qpos) & (n < nc)
    s_all = _dot_nt(jnp.concatenate([_qz_block(q_ref, h, SCALE) for h in range(N_HEADS)], axis=0), kc)
    ps, psum = [], [None] * HKV
    for h in range(N_HEADS):
        s = jnp.where(mask, s_all[h * tq:(h + 1) * tq, :], NEG)
        e = jnp.where(mask, jnp.exp(s - jnp.max(s, axis=-1, keepdims=True)), 0.0)
        p = e / jnp.maximum(jnp.sum(e, axis=-1, keepdims=True), TINY)
        ps.append(p.astype(BF16))
        psum[h // GROUP] = p if psum[h // GROUP] is None else psum[h // GROUP] + p
    o_all = _dot(jnp.concatenate(ps, axis=0), vc)
    o_ref[0] = _assemble_heads([o_all[h * tq:(h + 1) * tq, :] * _gate(g_ref, cfg["gate_col"] + h)
                                for h in range(N_HEADS)])
    cs = lax.broadcasted_iota(I32, (ncp, nsp), 0) * CMP_STRIDE
    ss = lax.broadcasted_iota(I32, (ncp, nsp), 1) * SEL_BLOCK
    overlap = ((cs < ss + SEL_BLOCK) & (cs + CMP_LEN > ss)).astype(BF16)
    hi, mid, lo = _split3(jnp.concatenate(psum, axis=0))
    imp = _dot(hi, overlap) + _dot(mid, overlap) + _dot(lo, overlap)
    j = lax.broadcasted_iota(I32, (1, nsp), 1)
    blk = jnp.concatenate([qpos // SEL_BLOCK] * HKV, axis=0)
    forced = (j == 0) | (j == blk) | (j == blk - 1)
    score = jnp.where(j <= blk, imp + jnp.where(forced, FORCE, 0.0), NEG)
    sel = (_topk_mask(score, N_SEL) & (score > NEG / 2)).astype(BF16)
    for g in range(HKV):
        bm_ref[0, :, g * nsp:(g + 1) * nsp] = sel[g * tq:(g + 1) * tq, :]


def nsa_select(q, cmp_kv, gates, *, tq, q0, nc, nsp, gate_col):
    b, t, dq = q.shape
    ncp = cmp_kv.shape[1]
    cfg = dict(tq=tq, q0=q0, nc=nc, nsp=nsp, gate_col=gate_col)
    return pl.pallas_call(
        functools.partial(_nsa_select_kernel, cfg),
        grid=(b, t // tq),
        in_specs=[pl.BlockSpec((1, tq, dq), lambda bi, i: (bi, i, 0)),
                  pl.BlockSpec((1, ncp, 2 * LANES), lambda bi, i: (bi, 0, 0)),
                  pl.BlockSpec((1, tq, LANES), lambda bi, i: (bi, i, 0))],
        out_specs=[pl.BlockSpec((1, tq, dq), lambda bi, i: (bi, i, 0)),
                   pl.BlockSpec((1, tq, HKV * nsp), lambda bi, i: (bi, i, 0))],
        out_shape=[jax.ShapeDtypeStruct((b, t, dq), F32),
                   jax.ShapeDtypeStruct((b, t, HKV * nsp), BF16)],
        compiler_params=_cparams(("arbitrary", "arbitrary")),
        name="nsa_select",
    )(q, cmp_kv, gates)


def _dsa_kernel(cfg, q_ref, iq_ref, g_ref, ik_ref, k_ref, v_ref, o_ref,
                qz_scr, iq_scr, iw_scr, key_scr, low_scr, m_scr, l_scr, acc_scr):
    tq, ck, n_top = cfg["tq"], cfg["ck"], cfg["n_top"]
    lp = k_ref.shape[1]
    get_i = lambda k0: ik_ref[0, pl.ds(k0, ck), :]
    get_k = lambda k0: k_ref[0, pl.ds(k0, ck), :]
    get_v = lambda k0: v_ref[0, pl.ds(k0, ck), :]
    qbase = cfg["q0"] + pl.program_id(1) * tq
    qpos = qbase + lax.broadcasted_iota(I32, (tq, 1), 0)
    hi_chunk = jnp.minimum((qbase + tq - 1) // ck + 1, lp // ck)
    for h in range(N_HEADS):
        qz_scr[h * tq:(h + 1) * tq, :] = _qz_block(q_ref, h, SCALE)
        pair = iq_ref[0, :, (h // 2) * LANES:(h // 2 + 1) * LANES]
        if h % 2:
            pair = pltpu.roll(pair, DH, 1)
        iq_scr[h * tq:(h + 1) * tq, :] = pair[:, :DH].astype(BF16)
        iw_scr[h * tq:(h + 1) * tq, :] = jnp.broadcast_to(g_ref[0, :, h:h + 1], (tq, LANES))
    low_scr[...] = (lax.broadcasted_iota(I32, (ck, ck), 0)
                    < lax.broadcasted_iota(I32, (ck, ck), 1)).astype(BF16)

    def score_body(c, carry):
        k0 = pl.multiple_of(c * ck, ck)
        s_all = jnp.maximum(_dot_nt(iq_scr[...], get_i(k0).astype(BF16)), 0.0)
        sc = s_all[0:tq, :] * _lanes(iw_scr[0:tq, :], ck)
        for h in range(1, N_HEADS):
            sc = sc + s_all[h * tq:(h + 1) * tq, :] * _lanes(iw_scr[h * tq:(h + 1) * tq, :], ck)
        kpos = k0 + lax.broadcasted_iota(I32, (1, ck), 1)
        key_scr[c] = _sort_key(jnp.where(kpos <= qpos, sc, NEG))
        return carry

    lax.fori_loop(0, hi_chunk, score_body, 0)

    def count(pred, thr):
        thr_w = _lanes(thr, ck)

        def body(c, a):
            hit = pred(key_scr[c], thr_w).astype(F32)
            for j in range(ck // LANES):
                a = a + hit[:, j * LANES:(j + 1) * LANES]
            return a
        a = lax.fori_loop(0, hi_chunk, body, jnp.zeros((tq, LANES), F32))
        return jnp.sum(a, axis=-1, keepdims=True)

    def bit_step(it, t):
        cand = t + lax.shift_left(jnp.int32(1), 31 - it)
        return jnp.where(count(lambda kk, th: kk >= th, cand) >= n_top, cand, t)

    t = lax.fori_loop(0, 32, bit_step, jnp.full((tq, LANES), INT_MIN, I32))
    need = n_top - count(lambda kk, th: kk > th, t)
    t_w = _lanes(t, ck)
    _flash_init(m_scr, l_scr, acc_scr)

    def attn_body(c, before):
        k0 = pl.multiple_of(c * ck, ck)
        kk = key_scr[c]
        eq = kk == t_w
        rank_eq = before + _dot(eq.astype(BF16), low_scr[...])
        mask = ((kk > t_w) | (eq & (rank_eq < need))) & (kk > _HALF_NEG_KEY)
        s_all = _dot_nt(qz_scr[...], get_k(k0).astype(BF16))
        _flash_chunk(s_all, mask, get_v(k0).astype(BF16), tq, m_scr, l_scr, acc_scr)
        return before + jnp.sum(eq.astype(F32), axis=-1, keepdims=True)

    lax.fori_loop(0, hi_chunk, attn_body, jnp.zeros((tq, 1), F32))
    o_ref[0] = _assemble_heads(_flash_heads(l_scr, acc_scr, tq))


def dsa_attention(q, iq, misc, ik, kv, *, tq, ck, q0, n_top):
    b, t, dq = q.shape
    lp = kv.shape[1]
    cfg = dict(tq=tq, ck=ck, q0=q0, n_top=n_top)
    qspec = pl.BlockSpec((1, tq, dq), lambda bi, i: (bi, i, 0))
    rows = N_HEADS * tq
    return pl.pallas_call(
        functools.partial(_dsa_kernel, cfg),
        grid=(b, t // tq),
        in_specs=[qspec, qspec,
                  pl.BlockSpec((1, tq, LANES), lambda bi, i: (bi, i, 0)),
                  pl.BlockSpec((1, lp, DH), lambda bi, i: (bi, 0, 0)),
                  pl.BlockSpec((1, lp, LANES), lambda bi, i: (bi, 0, 0)),
                  pl.BlockSpec((1, lp, LANES), lambda bi, i: (bi, 0, 1))],
        out_specs=qspec,
        out_shape=jax.ShapeDtypeStruct((b, t, dq), F32),
        scratch_shapes=[pltpu.VMEM((rows, LANES), BF16), pltpu.VMEM((rows, DH), BF16),
                        pltpu.VMEM((rows, LANES), F32), pltpu.VMEM((lp // ck, tq, ck), I32),
                        pltpu.VMEM((ck, ck), BF16), pltpu.VMEM((rows, LANES), F32),
                        pltpu.VMEM((rows, LANES), F32), pltpu.VMEM((rows, LANES), F32)],
        compiler_params=_cparams(("arbitrary", "arbitrary")),
        name="dsa_attention",
    )(q, iq, misc, ik, kv, kv)


def _moba_select_kernel(cfg, q_ref, k_ref, bm_ref, km_scr):
    tq, nblk, nbp, n_top = cfg["tq"], cfg["nblk"], cfg["nbp"], cfg["n_top"]
    lp = k_ref.shape[1]
    qpos = cfg["q0"] + pl.program_id(1) * tq + lax.broadcasted_iota(I32, (tq, 1), 0)
    km_scr[...] = jnp.zeros(km_scr.shape, F32)
    for j in range(nblk):
        r1 = min((j + 1) * MOBA_BLOCK, lp)
        km_scr[j:j + 1, :] = jnp.sum(k_ref[0, j * MOBA_BLOCK:r1, :], axis=0, keepdims=True) * (1.0 / MOBA_BLOCK)
    km = km_scr[...].astype(BF16)
    own = jnp.concatenate([qpos // MOBA_BLOCK] * N_HEADS, axis=0)
    j = lax.broadcasted_iota(I32, (1, nbp), 1)
    s = _dot_nt(jnp.concatenate([_qz_block(q_ref, h, 1.0) for h in range(N_HEADS)], axis=0), km)
    s = jnp.where(j < own, s, NEG)
    sel = (j == own)
    if n_top > 0:
        sel = sel | (_top_few_mask(s, n_top) & (s > NEG / 2))
    sel = sel.astype(BF16)
    for h in range(N_HEADS):
        bm_ref[0, :, h * nbp:(h + 1) * nbp] = sel[h * tq:(h + 1) * tq, :]


def moba_select(q, kv, *, tq, q0, nblk, nbp):
    b, t, dq = q.shape
    lp = kv.shape[1]
    cfg = dict(tq=tq, q0=q0, nblk=nblk, nbp=nbp, n_top=min(MOBA_TOPK, nblk - 1))
    return pl.pallas_call(
        functools.partial(_moba_select_kernel, cfg),
        grid=(b, t // tq),
        in_specs=[pl.BlockSpec((1, tq, dq), lambda bi, i: (bi, i, 0)),
                  pl.BlockSpec((1, lp, LANES), lambda bi, i: (bi, 0, 0))],
        out_specs=pl.BlockSpec((1, tq, N_HEADS * nbp), lambda bi, i: (bi, i, 0)),
        out_shape=jax.ShapeDtypeStruct((b, t, N_HEADS * nbp), BF16),
        scratch_shapes=[pltpu.VMEM((nbp, LANES), F32)],
        compiler_params=_cparams(("arbitrary", "arbitrary")),
        name="moba_select",
    )(q, kv)


def _log_sigmoid(x):
    return jnp.minimum(x, 0.0) - jnp.log(1.0 + jnp.exp(-jnp.abs(x)))


def _dot3_rhs(a_bf, x):
    hi, mid, lo = _split3(x)
    return _dot(a_bf, hi) + _dot(a_bf, mid) + _dot(a_bf, lo)


def _dot3_lhs(x, b_bf):
    hi, mid, lo = _split3(x)
    return _dot(hi, b_bf) + _dot(mid, b_bf) + _dot(lo, b_bf)


def _mlstm_kernel(cfg, u_ref, v_ref, og_ref, g_ref, cin_ref, ct0_ref, n0_ref, m0_ref,
                  cw_ref, cb_ref, wq_ref, wk_ref, fb_ref, ng_ref,
                  h_out, ct_out, n_out, m_out,
                  ubuf, q_scr, k_scr, v_scr, g_scr, h_scr, ct_scr, n_scr, m_scr, hm_scr):
    tc, tcp, t_valid = cfg["tc"], cfg["tcp"], cfg["t_valid"]
    d_c = N_HEADS * DH
    i = pl.program_id(1)

    @pl.when(i == 0)
    def _():
        ubuf[0:8, :] = cin_ref[0]
        ct_scr[...] = ct0_ref[0]
        n_scr[...] = n0_ref[0]
        m_scr[...] = m0_ref[0]
        hm_scr[...] = (lax.broadcasted_iota(I32, (d_c, d_c), 0) // DH
                       == lax.broadcasted_iota(I32, (d_c, d_c), 1) // DH).astype(F32)

    if tc < tcp:
        ubuf[8:, :] = jnp.zeros((tcp, d_c), F32)
        v_scr[...] = jnp.zeros((tcp, d_c), F32)
        g_scr[...] = jnp.zeros((tcp, LANES), F32)
    ubuf[8:8 + tc, :] = u_ref[0]
    v_scr[0:tc, :] = v_ref[0]
    g_scr[0:tc, :] = g_ref[0]
    conv = ubuf[pl.ds(CONV_W + 1, tcp), :] * cw_ref[0:1, :]
    for j in range(1, CONV_W):
        conv = conv + ubuf[pl.ds(CONV_W + 1 + j, tcp), :] * cw_ref[j:j + 1, :]
    conv = conv + cb_ref[...]
    uc = (conv * jax.nn.sigmoid(conv)).astype(BF16)
    q_scr[...] = _dot(uc, wq_ref[...])
    k_scr[...] = _dot(uc, wk_ref[...]) * SCALE

    lane = lax.broadcasted_iota(I32, (1, LANES), 1)
    head_lane = lane < N_HEADS
    t_io = lax.broadcasted_iota(I32, (CHUNK, 1), 0)
    causal = lane <= t_io
    tri = (lax.broadcasted_iota(I32, (CHUNK, CHUNK), 1)
           <= lax.broadcasted_iota(I32, (CHUNK, CHUNK), 0)).astype(BF16)
    tri_t = (lax.broadcasted_iota(I32, (LANES, LANES), 0)
             <= lax.broadcasted_iota(I32, (LANES, LANES), 1)).astype(BF16)
    expand = (lax.broadcasted_iota(I32, (LANES, d_c), 0)
              == lax.broadcasted_iota(I32, (LANES, d_c), 1) // DH).astype(BF16)
    zeros_gate = jnp.zeros((CHUNK, LANES), F32)
    zeros_feat = jnp.zeros((CHUNK, d_c), F32)

    def chunk_body(c, carry):
        r0 = pl.multiple_of(c * CHUNK, CHUNK)
        hm = hm_scr[...]
        g = g_scr[pl.ds(r0, CHUNK), :]
        valid = (i * tc + r0 + t_io) < t_valid
        ig = jnp.where(head_lane, jnp.where(valid, g, NEG), 0.0)
        lf = pltpu.roll(_log_sigmoid(g + fb_ref[...]), LANES - N_HEADS, 1)
        lf = jnp.where(head_lane & valid, lf, 0.0)
        b_col = _dot3_rhs(tri, lf)
        ig_t = jnp.concatenate([ig, zeros_gate], axis=0).T[0:8, :]
        lf_t = jnp.concatenate([lf, zeros_gate], axis=0).T[0:8, :]
        rowterm = ig_t - _dot3_lhs(lf_t, tri_t)
        m_row = m_scr[...]
        dws, iws, emts = [], [], []
        for h in range(N_HEADS):
            bc = b_col[:, h:h + 1]
            dlog = jnp.where(causal, bc + rowterm[h:h + 1, :], NEG)
            inter = bc + m_row[:, h:h + 1]
            m_t = jnp.maximum(inter, jnp.max(dlog, axis=-1, keepdims=True))
            dws.append(jnp.exp(dlog - m_t))
            iws.append(jnp.exp(inter - m_t))
            emts.append(jnp.exp(-m_t))
        dw = jnp.concatenate(dws, axis=0)
        iw = jnp.concatenate(iws, axis=0)
        emt = jnp.concatenate(emts, axis=0)
        q_c = q_scr[pl.ds(r0, CHUNK), :]
        k_c = k_scr[pl.ds(r0, CHUNK), :]
        v_c = v_scr[pl.ds(r0, CHUNK), :]
        qz = jnp.concatenate([q_c] * N_HEADS, axis=0) * hm
        qz_bf = qz.astype(BF16)
        k_pad = jnp.concatenate([k_c, zeros_feat], axis=0)
        v_pad = jnp.concatenate([v_c, zeros_feat], axis=0).astype(BF16)
        qkw = _dot_nt(qz_bf, k_pad.astype(BF16)) * dw
        intra = _dot(qkw.astype(BF16), v_pad)
        inter_z = _dot(qz_bf, ct_scr[...].astype(BF16))
        num = iw * inter_z + intra * hm
        den = iw * jnp.sum(qz * n_scr[...], axis=-1, keepdims=True) + jnp.sum(qkw, axis=-1, keepdims=True)
        hz = num / jnp.maximum(jnp.abs(den), emt)
        h_c = hz[0:CHUNK, :]
        for h in range(1, N_HEADS):
            h_c = h_c + hz[h * CHUNK:(h + 1) * CHUNK, :]
        h_scr[pl.ds(r0, CHUNK), :] = h_c
        b_last = b_col[CHUNK - 1:CHUNK, :]
        m_new = jnp.maximum(b_last + m_row, jnp.max(b_last - b_col + ig, axis=0, keepdims=True))
        decay = jnp.where(head_lane, jnp.exp(b_last + m_row - m_new), 0.0)
        ws = jnp.where(head_lane, jnp.exp(b_last - b_col + ig - m_new), 0.0)
        wide = _dot3_lhs(jnp.concatenate([ws, jnp.broadcast_to(decay, (8, LANES))], axis=0), expand)
        w8, decay_w = wide[0:CHUNK, :], wide[CHUNK:CHUNK + 1, :]
        vw_pad = jnp.concatenate([v_c * w8, zeros_feat], axis=0).astype(BF16)
        k_t = jnp.concatenate([k_pad[:, j * LANES:(j + 1) * LANES].T for j in range(d_c // LANES)], axis=0)
        ct_scr[...] = decay_w * ct_scr[...] + _dot(k_t.astype(BF16), vw_pad) * hm
        n_scr[...] = decay_w * n_scr[...] + jnp.sum(k_c * w8, axis=0, keepdims=True)
        m_scr[...] = m_new
        return carry

    lax.fori_loop(0, tcp // CHUNK, chunk_body, 0)
    h_all = h_scr[...]
    hi, mid, lo = _split3(h_all * h_all)
    hm_bf = hm_scr[...].astype(BF16)
    ms = (_dot(hi, hm_bf) + _dot(mid, hm_bf) + _dot(lo, hm_bf)) * (1.0 / DH)
    hc = h_all * lax.rsqrt(ms + EPS) * ng_ref[...]
    h_out[0] = jax.nn.sigmoid(og_ref[0]) * hc[0:tc, :]
    ubuf[0:8, :] = ubuf[tc:tc + 8, :]

    @pl.when(i == pl.num_programs(1) - 1)
    def _():
        ct_out[0] = ct_scr[...]
        n_out[0] = n_scr[...]
        m_out[0] = m_scr[...]


def mlstm(u, v, og, misc, conv_in, ct0, n0, m0, conv_w, conv_b, wq_bd, wk_bd, fb_row, norm_g, *, tc, t_valid):
    b, t, d_c = u.shape
    tcp = max(tc, CHUNK)
    cfg = dict(tc=tc, tcp=tcp, t_valid=t_valid)
    row = pl.BlockSpec((1, tc, d_c), lambda bi, i: (bi, i, 0))
    const = lambda shape: pl.BlockSpec(shape, lambda bi, i: (0,) * len(shape))
    per_b = lambda shape: pl.BlockSpec((1,) + shape, lambda bi, i: (bi,) + (0,) * len(shape))
    return pl.pallas_call(
        functools.partial(_mlstm_kernel, cfg),
        grid=(b, t // tc),
        in_specs=[row, row, row, pl.BlockSpec((1, tc, LANES), lambda bi, i: (bi, i, 0)),
                  per_b((8, d_c)), per_b((d_c, d_c)), per_b((1, d_c)), per_b((1, LANES)),
                  const((CONV_W, d_c)), const((1, d_c)), const((d_c, d_c)), const((d_c, d_c)),
                  const((1, LANES)), const((1, d_c))],
        out_specs=[row, per_b((d_c, d_c)), per_b((1, d_c)), per_b((1, LANES))],
        out_shape=[jax.ShapeDtypeStruct((b, t, d_c), F32),
                   jax.ShapeDtypeStruct((b, d_c, d_c), F32),
                   jax.ShapeDtypeStruct((b, 1, d_c), F32),
                   jax.ShapeDtypeStruct((b, 1, LANES), F32)],
        scratch_shapes=[pltpu.VMEM((tcp + 8, d_c), F32),
                        pltpu.VMEM((tcp, d_c), F32), pltpu.VMEM((tcp, d_c), F32), pltpu.VMEM((tcp, d_c), F32),
                        pltpu.VMEM((tcp, LANES), F32), pltpu.VMEM((tcp, d_c), F32),
                        pltpu.VMEM((d_c, d_c), F32), pltpu.VMEM((1, d_c), F32), pltpu.VMEM((1, LANES), F32),
                        pltpu.VMEM((d_c, d_c), F32)],
        compiler_params=_cparams(("arbitrary", "arbitrary")),
        name="mlstm",
    )(u, v, og, misc, conv_in, ct0, n0, m0, conv_w, conv_b, wq_bd, wk_bd, fb_row, norm_g)


def _page_copies(pool_ref, layer, pt_ref, b, row0, buf_ref, slot, sems):
    n_pages, page = pt_ref.shape[1], pool_ref.shape[3]
    nrows = buf_ref.shape[1]
    return [pltpu.make_async_copy(pool_ref.at[layer, pt_ref[b, p], pl.ds(row0, nrows), :],
                                  buf_ref.at[slot, :, pl.ds(p * page, page)], sems.at[slot, p])
            for p in range(n_pages)]


def _prefetch_pages(streams, pt_ref, layer):
    b = pl.program_id(0)
    slot = b % 2

    def start(seq, to_slot):
        for pool_ref, row0, buf_ref, sems, _ in streams:
            for cp in _page_copies(pool_ref, layer, pt_ref, seq, row0, buf_ref, to_slot, sems):
                cp.start()

    @pl.when(b == 0)
    def _():
        start(0, 0)

    @pl.when(b + 1 < pl.num_programs(0))
    def _():
        start(b + 1, 1 - slot)

    for pool_ref, row0, buf_ref, sems, new_ref in streams:
        n_tok = pt_ref.shape[1] * pool_ref.shape[3]
        buf_ref[slot, :, n_tok:] = new_ref[0]
        for cp in _page_copies(pool_ref, layer, pt_ref, b, row0, buf_ref, slot, sems):
            cp.wait()
    return slot


def _stack_qpos(qpos, n):
    return jnp.concatenate([qpos] * n, axis=0)


def _flash_rows(s, keep, v_t, m_scr, l_scr, acc_scr):
    ck = s.shape[1]
    if keep is not None:
        s = jnp.where(keep, s, -MASK_BIG)
    m_old = m_scr[...]
    m_new = jnp.maximum(m_old, jnp.max(s, axis=-1, keepdims=True))
    p = jnp.exp(s - _lanes(m_new, ck))
    alpha = jnp.exp(m_old - m_new)
    l_scr[...] = alpha * l_scr[...] + jnp.sum(p, axis=-1, keepdims=True)
    m_scr[...] = m_new
    acc_scr[...] = alpha * acc_scr[...] + _dot_nt(p.astype(BF16), v_t)


def _dec_battn_kernel(cfg, pt_ref, q_ref, pool_ref, knew_ref, vnew_ref, bm_ref, g_ref, o_ref,
                      qa_scr, m_scr, l_scr, acc_scr, k_buf, v_buf, sems):
    tq, ck, nbp, mg, lp = cfg["tq"], cfg["ck"], cfg["nbp"], cfg["mask_group"], cfg["lp"]
    slot = _prefetch_pages([(pool_ref, cfg["k_col"] * LANES, k_buf, sems.at[0], knew_ref),
                            (pool_ref, cfg["v_col"] * LANES, v_buf, sems.at[1], vnew_ref)], pt_ref, cfg["layer"])
    qbase = cfg["q0"]
    for h in range(N_HEADS):
        g = h // mg
        qa_scr[h * tq:(h + 1) * tq, 0:LANES] = _qz_block(q_ref, h, SCALE)
        qa_scr[h * tq:(h + 1) * tq, LANES:LANES + nbp] = bm_ref[0, :, g * nbp:(g + 1) * nbp] - 1
    _flash_init(m_scr, l_scr, acc_scr)
    qpos = _stack_qpos(qbase + lax.broadcasted_iota(I32, (tq, 1), 0), N_HEADS)
    last = min((qbase + tq - 1) // ck, lp // ck - 1)

    for c in range(last + 1):
        k0 = c * ck
        kpos = k0 + lax.broadcasted_iota(I32, (1, ck), 1)
        bias = jnp.where(lax.broadcasted_iota(I32, (nbp, ck), 0) == lax.shift_right_logical(kpos, cfg["bshift"]),
                         MASK_BIG, 0.0).astype(BF16)
        k_aug = jnp.concatenate([k_buf[slot, :, k0:k0 + ck].astype(BF16), bias], axis=0)
        keep = (kpos <= qpos) if c == last else None
        _flash_rows(_dot(qa_scr[...], k_aug), keep, v_buf[slot, :, k0:k0 + ck].astype(BF16), m_scr, l_scr, acc_scr)
    heads = _flash_heads(l_scr, acc_scr, tq)
    if cfg["gate_col"] is not None:
        heads = [o * _gate(g_ref, cfg["gate_col"] + h) for h, o in enumerate(heads)]
    o_ref[0] = _assemble_heads(heads)


def dec_block_attention(q, pool_t, k_col, v_col, gates, bmask, new_t, *, layer, page_table, ck, q0, nbp, bshift,
                        mask_group, gate_col):
    b, tq, dq = q.shape
    n_pages, page = page_table.shape[1], pool_t.shape[3]
    lp = (n_pages + 1) * page
    assert lp % ck == 0 and (q0 + tq - 1) // ck == q0 // ck
    cfg = dict(tq=tq, ck=ck, q0=q0, nbp=nbp, bshift=bshift, mask_group=mask_group, gate_col=gate_col,
               lp=lp, layer=layer, k_col=k_col, v_col=v_col)
    rows = N_HEADS * tq
    im = lambda bi, pt: (bi, 0, 0)
    return pl.pallas_call(
        functools.partial(_dec_battn_kernel, cfg),
        grid_spec=pltpu.PrefetchScalarGridSpec(
            num_scalar_prefetch=1, grid=(b,),
            in_specs=[pl.BlockSpec((1, tq, dq), im), pl.BlockSpec(memory_space=pl.ANY),
                      pl.BlockSpec((1, LANES, page), lambda bi, pt: (bi, k_col, 0)),
                      pl.BlockSpec((1, LANES, page), lambda bi, pt: (bi, v_col, 0)),
                      pl.BlockSpec((1, tq, bmask.shape[2]), im), pl.BlockSpec((1, tq, LANES), im)],
            out_specs=pl.BlockSpec((1, tq, dq), im),
            scratch_shapes=[pltpu.VMEM((rows, LANES + nbp), BF16), pltpu.VMEM((rows, LANES), F32),
                            pltpu.VMEM((rows, LANES), F32), pltpu.VMEM((rows, LANES), F32),
                            pltpu.VMEM((2, LANES, lp), F32), pltpu.VMEM((2, LANES, lp), F32),
                            pltpu.SemaphoreType.DMA((2, 2, n_pages))]),
        out_shape=jax.ShapeDtypeStruct((b, tq, dq), F32),
        compiler_params=_cparams(("arbitrary",)), name="dec_attn_block",
    )(page_table, q, pool_t, new_t, new_t, bmask, gates)


def _dec_compress_kernel(cfg, pt_ref, pool_ref, knew_ref, vnew_ref, pe_ref, w_ref, o_ref,
                         a_scr, b_scr, x_scr, k_buf, v_buf, sems):
    n_groups, lp = cfg["n_groups"], cfg["lp"]
    slot = _prefetch_pages([(pool_ref, 0, k_buf, sems.at[0], knew_ref),
                            (pool_ref, LANES, v_buf, sems.at[1], vnew_ref)], pt_ref, cfg["layer"])
    o_ref[0] = jnp.zeros(o_ref.shape[1:], F32)
    for kind, buf in enumerate((k_buf, v_buf)):
        for p in range(lp // LANES):
            x_scr[p * LANES:(p + 1) * LANES, :] = buf[slot, :, p * LANES:(p + 1) * LANES].T
        acc_a = jnp.zeros((n_groups, LANES), F32)
        acc_b = jnp.zeros((n_groups, LANES), F32)
        for l in range(CMP_STRIDE):
            x = x_scr[pl.ds(l, n_groups, stride=CMP_STRIDE), :]
            acc_a += _dot((x + pe_ref[kind, l:l + 1, :]).astype(BF16), w_ref[kind, l])
            acc_b += _dot((x + pe_ref[kind, CMP_STRIDE + l:CMP_STRIDE + l + 1, :]).astype(BF16),
                          w_ref[kind, CMP_STRIDE + l])
        a_scr[...] = acc_a
        b_scr[0:n_groups, :] = acc_b
        b_scr[n_groups:n_groups + 8, :] = jnp.zeros((8, LANES), F32)
        o_ref[0, 0:n_groups, kind * LANES:(kind + 1) * LANES] = a_scr[...] + b_scr[pl.ds(1, n_groups), :]


def dec_compress(pool_t, pe2, w_bd, ncp, new_t, *, layer, page_table):
    b = new_t.shape[0]
    n_pages, page = page_table.shape[1], pool_t.shape[3]
    lp = (n_pages + 1) * page
    n_groups = lp // CMP_STRIDE
    cfg = dict(n_groups=n_groups, lp=lp, layer=layer)
    return pl.pallas_call(
        functools.partial(_dec_compress_kernel, cfg),
        grid_spec=pltpu.PrefetchScalarGridSpec(
            num_scalar_prefetch=1, grid=(b,),
            in_specs=[pl.BlockSpec(memory_space=pl.ANY),
                      pl.BlockSpec((1, LANES, page), lambda bi, pt: (bi, 0, 0)),
                      pl.BlockSpec((1, LANES, page), lambda bi, pt: (bi, 1, 0)),
                      pl.BlockSpec((2, CMP_LEN, LANES), lambda bi, pt: (0, 0, 0)),
                      pl.BlockSpec((2, CMP_LEN, LANES, LANES), lambda bi, pt: (0, 0, 0, 0))],
            out_specs=pl.BlockSpec((1, ncp, 2 * LANES), lambda bi, pt: (bi, 0, 0)),
            scratch_shapes=[pltpu.VMEM((n_groups, LANES), F32), pltpu.VMEM((n_groups + 8, LANES), F32),
                            pltpu.VMEM((lp, LANES), F32),
                            pltpu.VMEM((2, LANES, lp), F32), pltpu.VMEM((2, LANES, lp), F32),
                            pltpu.SemaphoreType.DMA((2, 2, n_pages))]),
        out_shape=jax.ShapeDtypeStruct((b, ncp, 2 * LANES), F32),
        compiler_params=_cparams(("arbitrary",)), name="dec_compress",
    )(page_table, pool_t, new_t, new_t, pe2, w_bd)


def _dec_dsa_kernel(cfg, pt_ref, q_ref, iq_ref, g_ref, ipool_ref, inew_ref, pool_ref, knew_ref, vnew_ref, o_ref,
                    qz_scr, iq_scr, key_scr, low_scr, m_scr, l_scr, acc_scr, i_buf, k_buf, v_buf, sems):
    tq, ck, n_top, lp = cfg["tq"], cfg["ck"], cfg["n_top"], cfg["lp"]
    slot = _prefetch_pages([(ipool_ref, 0, i_buf, sems.at[0], inew_ref),
                            (pool_ref, 0, k_buf, sems.at[1], knew_ref),
                            (pool_ref, LANES, v_buf, sems.at[2], vnew_ref)], pt_ref, cfg["layer"])
    qbase = cfg["q0"]
    qpos = qbase + lax.broadcasted_iota(I32, (tq, 1), 0)
    n_chunks = min((qbase + tq - 1) // ck + 1, lp // ck)
    for h in range(N_HEADS):
        qz_scr[h * tq:(h + 1) * tq, :] = _qz_block(q_ref, h, SCALE)
        pair = iq_ref[0, :, (h // 2) * LANES:(h // 2 + 1) * LANES]
        if h % 2:
            pair = pltpu.roll(pair, DH, 1)
        iq_scr[h * tq:(h + 1) * tq, :] = pair[:, :DH].astype(BF16)
    low_scr[...] = (lax.broadcasted_iota(I32, (ck, ck), 0)
                    < lax.broadcasted_iota(I32, (ck, ck), 1)).astype(BF16)
    iw = [jnp.broadcast_to(g_ref[0, :, h:h + 1], (tq, LANES)) for h in range(N_HEADS)]

    for c in range(n_chunks):
        k0 = c * ck
        s_all = jnp.maximum(_dot(iq_scr[...], i_buf[slot, :, k0:k0 + ck].astype(BF16)), 0.0)
        sc = s_all[0:tq, :] * _lanes(iw[0], ck)
        for h in range(1, N_HEADS):
            sc = sc + s_all[h * tq:(h + 1) * tq, :] * _lanes(iw[h], ck)
        kpos = k0 + lax.broadcasted_iota(I32, (1, ck), 1)
        key_scr[c] = _sort_key(jnp.where(kpos <= qpos, sc, NEG))

    def count(pred, thr):
        thr_w = _lanes(thr, ck)

        def body(c, a):
            hit = pred(key_scr[c], thr_w).astype(F32)
            for j in range(ck // LANES):
                a = a + hit[:, j * LANES:(j + 1) * LANES]
            return a
        a = lax.fori_loop(0, n_chunks, body, jnp.zeros((tq, LANES), F32))
        return jnp.sum(a, axis=-1, keepdims=True)

    def bit_step(it, t):
        cand = t + lax.shift_left(jnp.int32(1), 31 - it)
        return jnp.where(count(lambda kk, th: kk >= th, cand) >= n_top, cand, t)

    t = lax.fori_loop(0, 32, bit_step, jnp.full((tq, LANES), INT_MIN, I32))
    need = n_top - count(lambda kk, th: kk > th, t)
    t_w = _lanes(t, ck)
    _flash_init(m_scr, l_scr, acc_scr)

    before = jnp.zeros((tq, 1), F32)
    for c in range(n_chunks):
        k0 = c * ck
        kk = key_scr[c]
        eq = kk == t_w
        rank_eq = before + _dot(eq.astype(BF16), low_scr[...])
        mask = ((kk > t_w) | (eq & (rank_eq < need))) & (kk > _HALF_NEG_KEY)
        keep = jnp.concatenate([mask.astype(F32)] * N_HEADS, axis=0) > 0.5
        s = _dot(qz_scr[...], k_buf[slot, :, k0:k0 + ck].astype(BF16))
        _flash_rows(s, keep, v_buf[slot, :, k0:k0 + ck].astype(BF16), m_scr, l_scr, acc_scr)
        before = before + jnp.sum(eq.astype(F32), axis=-1, keepdims=True)
    o_ref[0] = _assemble_heads(_flash_heads(l_scr, acc_scr, tq))


def dec_dsa_attention(q, iq, misc, ipool_t, pool_t, inew_t, new_t, *, layer, page_table, ck, q0, n_top):
    b, tq, dq = q.shape
    n_pages, page = page_table.shape[1], pool_t.shape[3]
    lp = (n_pages + 1) * page
    cfg = dict(tq=tq, ck=ck, q0=q0, n_top=n_top, lp=lp, layer=layer)
    rows = N_HEADS * tq
    im = lambda bi, pt: (bi, 0, 0)
    qspec = pl.BlockSpec((1, tq, dq), im)
    return pl.pallas_call(
        functools.partial(_dec_dsa_kernel, cfg),
        grid_spec=pltpu.PrefetchScalarGridSpec(
            num_scalar_prefetch=1, grid=(b,),
            in_specs=[qspec, qspec, pl.BlockSpec((1, tq, LANES), im),
                      pl.BlockSpec(memory_space=pl.ANY), pl.BlockSpec((1, DH, page), im),
                      pl.BlockSpec(memory_space=pl.ANY),
                      pl.BlockSpec((1, LANES, page), lambda bi, pt: (bi, 0, 0)),
                      pl.BlockSpec((1, LANES, page), lambda bi, pt: (bi, 1, 0))],
            out_specs=qspec,
            scratch_shapes=[pltpu.VMEM((rows, LANES), BF16), pltpu.VMEM((rows, DH), BF16),
                            pltpu.VMEM((lp // ck, tq, ck), I32), pltpu.VMEM((ck, ck), BF16),
                            pltpu.VMEM((rows, LANES), F32), pltpu.VMEM((rows, LANES), F32),
                            pltpu.VMEM((rows, LANES), F32),
                            pltpu.VMEM((2, DH, lp), F32), pltpu.VMEM((2, LANES, lp), F32),
                            pltpu.VMEM((2, LANES, lp), F32), pltpu.SemaphoreType.DMA((3, 2, n_pages))]),
        out_shape=jax.ShapeDtypeStruct((b, tq, dq), F32),
        compiler_params=_cparams(("arbitrary",)), name="dec_dsa",
    )(page_table, q, iq, misc, ipool_t, inew_t, pool_t, new_t, new_t)


def _dec_moba_select_kernel(cfg, pt_ref, q_ref, pool_ref, knew_ref, bm_ref, k_buf, sems):
    tq, nblk, nbp, n_top, lp = cfg["tq"], cfg["nblk"], cfg["nbp"], cfg["n_top"], cfg["lp"]
    slot = _prefetch_pages([(pool_ref, 0, k_buf, sems, knew_ref)], pt_ref, cfg["layer"])
    qpos = cfg["q0"] + lax.broadcasted_iota(I32, (tq, 1), 0)
    lane = lax.broadcasted_iota(I32, (1, nbp), 1)
    km_t = jnp.zeros((LANES, nbp), F32)
    for j in range(nblk):
        r1 = min((j + 1) * MOBA_BLOCK, lp)
        col = jnp.sum(k_buf[slot, :, j * MOBA_BLOCK:r1], axis=1, keepdims=True) * (1.0 / MOBA_BLOCK)
        km_t = jnp.where(lane == j, col, km_t)
    own = _stack_qpos(qpos // MOBA_BLOCK, N_HEADS)
    s = _dot(jnp.concatenate([_qz_block(q_ref, h, 1.0) for h in range(N_HEADS)], axis=0), km_t.astype(BF16))
    s = jnp.where(lane < own, s, NEG)
    sel = (lane == own)
    if n_top > 0:
        sel = sel | (_top_few_mask(s, n_top) & (s > NEG / 2))
    sel = sel.astype(BF16)
    for h in range(N_HEADS):
        bm_ref[0, :, h * nbp:(h + 1) * nbp] = sel[h * tq:(h + 1) * tq, :]


def dec_moba_select(q, pool_t, new_t, *, layer, page_table, q0, nblk, nbp):
    b, tq, dq = q.shape
    n_pages, page = page_table.shape[1], pool_t.shape[3]
    lp = (n_pages + 1) * page
    cfg = dict(tq=tq, q0=q0, nblk=nblk, nbp=nbp, n_top=min(MOBA_TOPK, nblk - 1), lp=lp, layer=layer)
    im = lambda bi, pt: (bi, 0, 0)
    return pl.pallas_call(
        functools.partial(_dec_moba_select_kernel, cfg),
        grid_spec=pltpu.PrefetchScalarGridSpec(
            num_scalar_prefetch=1, grid=(b,),
            in_specs=[pl.BlockSpec((1, tq, dq), im), pl.BlockSpec(memory_space=pl.ANY),
                      pl.BlockSpec((1, LANES, page), im)],
            out_specs=pl.BlockSpec((1, tq, N_HEADS * nbp), im),
            scratch_shapes=[pltpu.VMEM((2, LANES, lp), F32), pltpu.SemaphoreType.DMA((2, n_pages))]),
        out_shape=jax.ShapeDtypeStruct((b, tq, N_HEADS * nbp), BF16),
        compiler_params=_cparams(("arbitrary",)), name="dec_moba_select",
    )(page_table, q, pool_t, new_t)


def _dec_win_kernel(cfg, q_ref, kv_ref, g_ref, o_ref):
    tq, koff = cfg["tq"], cfg["koff"]
    lw = kv_ref.shape[2]
    kpos = koff + lax.broadcasted_iota(I32, (1, lw), 1)
    qpos = _stack_qpos(cfg["q0"] + lax.broadcasted_iota(I32, (tq, 1), 0), N_HEADS)
    mask = (kpos <= qpos) & (qpos - kpos < WINDOW) & (kpos >= 0)
    qz = jnp.concatenate([_qz_block(q_ref, h, SCALE) for h in range(N_HEADS)], axis=0)
    s = jnp.where(mask, _dot(qz, kv_ref[0, 0:LANES, :].astype(BF16)), NEG)
    e = jnp.where(mask, jnp.exp(s - jnp.max(s, axis=-1, keepdims=True)), 0.0)
    l = jnp.maximum(jnp.sum(e, axis=-1, keepdims=True), TINY)
    o_all = _dot_nt(e.astype(BF16), kv_ref[0, LANES:2 * LANES, :].astype(BF16)) / l
    o_ref[0] = _assemble_heads([o_all[h * tq:(h + 1) * tq, :] * _gate(g_ref, cfg["gate_col"] + h)
                                for h in range(N_HEADS)])


def dec_window_attention(q, kv_t, gates, *, q0, koff, gate_col):
    b, tq, dq = q.shape
    lw = kv_t.shape[2]
    cfg = dict(tq=tq, q0=q0, koff=koff, gate_col=gate_col)
    return pl.pallas_call(
        functools.partial(_dec_win_kernel, cfg),
        grid=(b,),
        in_specs=[pl.BlockSpec((1, tq, dq), lambda bi: (bi, 0, 0)),
                  pl.BlockSpec((1, 2 * LANES, lw), lambda bi: (bi, 0, 0)),
                  pl.BlockSpec((1, tq, LANES), lambda bi: (bi, 0, 0))],
        out_specs=pl.BlockSpec((1, tq, dq), lambda bi: (bi, 0, 0)),
        out_shape=jax.ShapeDtypeStruct((b, tq, dq), F32),
        compiler_params=_cparams(("arbitrary",)), name="dec_attn_window",
    )(q, kv_t, gates)


def _regroup_columns(w, b, pieces):
    n_src = w.shape[-1]
    idx = []
    for s, wd, wp in pieces:
        idx += list(range(s, s + wd)) + [n_src] * (wp - wd)
    idx = np.asarray(idx, np.int32)
    w_ext = jnp.concatenate([w, jnp.zeros(w.shape[:-1] + (1,), w.dtype)], axis=-1)
    b_ext = jnp.concatenate([b, jnp.zeros(b.shape[:-1] + (1,), b.dtype)], axis=-1)
    return jnp.take(w_ext, idx, axis=-1).astype(BF16), jnp.take(b_ext, idx, axis=-1)


def _block_diag(w):
    h, a, b = w.shape[-3:]
    eye = jnp.eye(h, dtype=w.dtype)
    out = w[..., :, :, None, :] * eye[:, None, :, None]
    return out.reshape(w.shape[:-3] + (h * a, h * b))


_EVEN_SRC = [(0, 512, 512), (512, 512, 512), (1024, 256, 256), (1304, 512, 512), (1816, 256, 256),
             (2072, 512, 512), (2584, 64, 128), (2648, 8, 8), (1280, 24, 120)]
_EVEN_GROUPS = [(0, 512, 512), (512, 512, 512), (1024, 256, 256), (1280, 512, 512), (1792, 256, 256),
                (2048, 512, 512), (2560, 128, 64), (2688, 128, 128)]
_ODD_SRC = [(0, 512, 512), (512, 512, 512), (1024, 512, 512), (1552, 512, 512), (2064, 256, 256),
            (1536, 16, 128)]
_ODD_GROUPS = [(0, 512, 512), (512, 512, 512), (1024, 512, 512), (1536, 512, 512), (2048, 256, 256),
               (2304, 128, 128)]
_GATE_COL = N_HEADS


def _pick_chunk(n, limit):
    return max(c for c in range(LANES, limit + 1, LANES) if n % c == 0)


def _pad_rows(a, rows):
    return jnp.pad(a, ((0, 0), (0, rows - a.shape[1])) + ((0, 0),) * (a.ndim - 2))


def _new_cols(a, t_real, width):
    keep = (jnp.arange(a.shape[1]) < t_real)[None, :, None]
    return jnp.swapaxes(_pad_rows(jnp.where(keep, a, 0.0), width), 1, 2)


def _even_mixer(p, i, outs, bsz, t, t_real, past, page_table, q0):
    a_q, kv4, win, b_q, b_kv, b_iq, b_ik, misc = [o.reshape(bsz, t, -1) for o in outs]
    lk = t if past is None else q0 + t_real
    n16 = -(-lk // CMP_STRIDE)
    nsp = -(-(-(-lk // SEL_BLOCK)) // LANES) * LANES
    gates = dict(cmp=_GATE_COL, sel=_GATE_COL + N_HEADS, win=_GATE_COL + 2 * N_HEADS)
    if past is None:
        tq, ck = 128, 512
        cmp_kv = nsa_compress(kv4, p["pe2"][i], p["cmp_w"][i], -(-(t // CMP_STRIDE) // LANES) * LANES)
        o_cmp, bmask = nsa_select(a_q, cmp_kv, misc, tq=tq, q0=q0, nc=n16 - 1, nsp=nsp, gate_col=gates["cmp"])
        o_sel = block_attention(a_q, kv4, 2, 3, misc, bmask, tq=tq, ck=ck, q0=q0, nbp=nsp, bshift=6,
                                mask_group=GROUP, gate_col=gates["sel"])
        o_win = window_attention(a_q, win, misc, tq=tq, ckw=WINDOW + tq, q0=q0, koff=0, gate_col=gates["win"])
        o_dsa = dsa_attention(b_q, b_iq, misc, b_ik, b_kv, tq=tq, ck=ck, q0=q0, n_top=min(DSA_TOPK, lk // 4))
        win_state = win[:, -min(WINDOW, t):].reshape(bsz, -1, 2, HKV, DH)
    else:
        page = past["nsa"].shape[3]
        pg = dict(layer=i, page_table=page_table)
        lp = (page_table.shape[1] + 1) * page
        ck = _pick_chunk(lp, 640)
        nsa_new = _new_cols(kv4, t_real, page)
        cmp_kv = dec_compress(past["nsa"], p["pe2"][i], p["cmp_w"][i], -(-(lp // CMP_STRIDE) // LANES) * LANES,
                              nsa_new, **pg)
        o_cmp, bmask = nsa_select(a_q, cmp_kv, misc, tq=t, q0=q0, nc=n16 - 1, nsp=nsp, gate_col=gates["cmp"])
        o_sel = dec_block_attention(a_q, past["nsa"], 2, 3, misc, bmask, nsa_new, ck=ck, q0=q0, nbp=nsp, bshift=6,
                                    mask_group=GROUP, gate_col=gates["sel"], **pg)
        win_buf = past["win"][i]
        n_buf = win_buf.shape[2]
        win_t = jnp.concatenate([win_buf, _new_cols(win, t_real, -(-t_real // LANES) * LANES)], axis=2)
        o_win = dec_window_attention(a_q, win_t, misc, q0=q0, koff=q0 - n_buf, gate_col=gates["win"])
        o_dsa = dec_dsa_attention(b_q, b_iq, misc, past["idx"], past["dsa"], _new_cols(b_ik, t_real, page),
                                  _new_cols(b_kv, t_real, page), ck=ck, q0=q0, n_top=min(DSA_TOPK, lk // 4), **pg)
        win_state = jnp.swapaxes(win_t[:, :, t_real:t_real + n_buf], 1, 2).reshape(bsz, n_buf, 2, HKV, DH)
    state = (kv4[:, :t_real].reshape(bsz, t_real, 4, HKV, DH), win_state,
             b_kv[:, :t_real].reshape(bsz, t_real, 2, HKV, DH), b_ik[:, :t_real])
    return [o_cmp, o_sel, o_win], o_dsa, state


def _odd_mixer(p, i, outs, bsz, t, t_real, past, page_table, q0):
    u, c_v, c_o, d_q, d_kv, misc = [o.reshape(bsz, t, -1) for o in outs]
    d_c = N_HEADS * DH
    idx = np.arange(N_HEADS)
    lk = t if past is None else q0 + t_real
    nblk = -(-lk // MOBA_BLOCK)
    if past is None:
        ct0 = jnp.zeros((bsz, d_c, d_c), F32)
        n0 = jnp.zeros((bsz, 1, d_c), F32)
        m0 = jnp.zeros((bsz, 1, LANES), F32)
        conv_prev = jnp.zeros((bsz, CONV_W - 1, d_c), F32)
        tc = 512
        bmask = moba_select(d_q, d_kv, tq=128, q0=q0, nblk=nblk, nbp=LANES)
        o_d = block_attention(d_q, d_kv, 0, 1, misc, bmask, tq=128, ck=512, q0=q0, nbp=LANES, bshift=8,
                              mask_group=1, gate_col=None)
    else:
        page = past["moba"].shape[3]
        pg = dict(layer=i, page_table=page_table)
        c_t = jnp.swapaxes(past["c"][i], -1, -2)
        ct0 = jnp.zeros((bsz, N_HEADS, DH, N_HEADS, DH), F32).at[:, idx, :, idx, :].set(
            jnp.moveaxis(c_t, 1, 0)).reshape(bsz, d_c, d_c)
        n0 = past["n"][i].reshape(bsz, 1, d_c)
        m0 = jnp.pad(past["m"][i], ((0, 0), (0, LANES - N_HEADS)))[:, None, :]
        conv_prev = past["conv"][i]
        tc = t
        moba_new = _new_cols(d_kv, t_real, page)
        bmask = dec_moba_select(d_q, past["moba"], moba_new, q0=q0, nblk=nblk, nbp=LANES, **pg)
        o_d = dec_block_attention(d_q, past["moba"], 0, 1, misc, bmask, moba_new,
                                  ck=_pick_chunk((page_table.shape[1] + 1) * page, 640), q0=q0, nbp=LANES,
                                  bshift=8, mask_group=1, gate_col=None, **pg)
    conv_in = jnp.concatenate([jnp.zeros((bsz, 8 - (CONV_W - 1), d_c), F32), conv_prev], axis=1)
    hc, ct1, n1, m1 = mlstm(u, c_v, c_o, misc, conv_in, ct0, n0, m0, p["conv_w"][i], p["conv_b"][i],
                            p["wq_bd"][i], p["wk_bd"][i], p["fb_row"][i], p["norm_g"][i], tc=tc, t_valid=t_real)
    c1 = jnp.swapaxes(jnp.moveaxis(ct1.reshape(bsz, N_HEADS, DH, N_HEADS, DH)[:, idx, :, idx, :], 0, 1), -1, -2)
    conv_state = jnp.concatenate([conv_prev, u[:, :t_real]], axis=1)[:, -(CONV_W - 1):]
    state = (c1, n1.reshape(bsz, N_HEADS, DH), m1[:, 0, :N_HEADS], conv_state,
             d_kv[:, :t_real].reshape(bsz, t_real, 2, HKV, DH))
    return [hc], o_d, state


def _run_group(p, x, mod, t_real, past, page_table, q0, per_row):
    bsz, t, d = x.shape
    m = bsz * t
    tm = min(512, m)
    tmm = min(1024, m)
    tiles = max(t // tm, 1)
    tiles_mlp = max(t // tmm, 1)
    x2d = x.reshape(m, d)
    n_layers = p["mlp_w1"].shape[0]
    ev_states, od_states = [], []
    for l in range(n_layers):
        i = l // 2
        if per_row:
            mod_l = jnp.moveaxis(jnp.repeat(mod[l], t, axis=0), 1, 0)
        else:
            mod_l = mod[l]
        if l % 2 == 0:
            outs = k_in(x2d, mod_l, p["norm1_g"][l], p["ev_w"][i], p["ev_b"][i], _EVEN_GROUPS, tm, tiles, per_row)
            a_list, b_o, st = _even_mixer(p, i, outs, bsz, t, t_real, past, page_table, q0)
            ev_states.append(st)
            w_out = p["ev_w_out"][i]
        else:
            outs = k_in(x2d, mod_l, p["norm1_g"][l], p["od_w"][i], p["od_b"][i], _ODD_GROUPS, tm, tiles, per_row)
            a_list, b_o, st = _odd_mixer(p, i, outs, bsz, t, t_real, past, page_table, q0)
            od_states.append(st)
            w_out = p["od_w_out"][i]
        x2d = k_out([a.reshape(m, -1) for a in a_list], b_o.reshape(m, -1), x2d, mod_l, w_out, tm, tiles, per_row)
        x2d = k_mlp(x2d, mod_l, p["norm2_g"][l], p["final_g"], p["mlp_w1"][l], p["mlp_w2"][l], tmm,
                    min(1024, p["mlp_w1"].shape[2]), tiles_mlp, per_row, final=(l == n_layers - 1))
    stack = lambda states: tuple(jnp.stack(a) for a in zip(*states))
    return x2d.reshape(bsz, t, d)[:, :t_real], stack(ev_states), stack(od_states)


def kernel(x_prompt, x_sample, cache_nsa_kv, state_nsa_win, cache_dsa_kv, cache_dsa_idx, state_mlstm_c, state_mlstm_n, state_mlstm_m, state_mlstm_conv, cache_moba_kv, page_table, c_prompt, c_sample, ada_w, ada_b, norm1_g, norm2_g, ev_w_in, ev_b_in, ev_w_out, nsa_cmp_pe, nsa_cmp_w, od_w_in, od_b_in, od_w_out, ml_conv_w, ml_conv_b, ml_wq, ml_wk, ml_f_bias, ml_norm_g, mlp_w1, mlp_w2, final_g):
    n_even, n_odd = ev_w_in.shape[0], od_w_in.shape[0]
    d = x_prompt.shape[-1]
    bp, bs = x_prompt.shape[0], x_sample.shape[0]
    t_dec = x_sample.shape[1]
    t_pad = -(-t_dec // 8) * 8
    page = cache_nsa_kv.shape[2]

    ev_w, ev_b = _regroup_columns(ev_w_in, ev_b_in, _EVEN_SRC)
    od_w, od_b = _regroup_columns(od_w_in, od_b_in, _ODD_SRC)
    d_c = N_HEADS * DH
    p = dict(
        norm1_g=norm1_g, norm2_g=norm2_g, final_g=final_g,
        ev_w=ev_w, ev_b=ev_b, od_w=od_w, od_b=od_b,
        ev_w_out=ev_w_out.astype(BF16), od_w_out=od_w_out.astype(BF16),
        mlp_w1=mlp_w1.astype(BF16), mlp_w2=mlp_w2.astype(BF16),
        pe2=jnp.tile(nsa_cmp_pe, (1, 1, 1, HKV)),
        cmp_w=_block_diag(jnp.broadcast_to(nsa_cmp_w[:, :, :, None], nsa_cmp_w.shape[:3] + (HKV, DH, DH))).astype(BF16),
        conv_w=ml_conv_w, conv_b=ml_conv_b.reshape(n_odd, 1, d_c),
        wq_bd=_block_diag(ml_wq).astype(BF16), wk_bd=_block_diag(ml_wk).astype(BF16),
        fb_row=jnp.pad(ml_f_bias, ((0, 0), (N_HEADS, LANES - 2 * N_HEADS))).reshape(n_odd, 1, LANES),
        norm_g=ml_norm_g.reshape(n_odd, 1, d_c),
    )
    n_rows = -(-(bs + bp) // 8) * 8
    c_all = jnp.pad(jnp.concatenate([c_sample, c_prompt], axis=0), ((0, n_rows - bs - bp), (0, 0)))
    mod = ada_mod(c_all, ada_w.astype(BF16), ada_b).reshape(ada_w.shape[0], n_rows, 6, d)
    mod_s, mod_p = mod[:, :bs], mod[:, bs:bs + bp]

    y_p, ev_p, od_p = _run_group(p, x_prompt, mod_p, x_prompt.shape[1], None, None, 0, False)

    def feat_major(a):
        nd = a.ndim
        a = jnp.transpose(a, (0, 1) + tuple(range(3, nd)) + (2,))
        return a.reshape(a.shape[:2] + (-1, a.shape[-1]))
    past = dict(nsa=feat_major(cache_nsa_kv), win=feat_major(state_nsa_win), dsa=feat_major(cache_dsa_kv),
                idx=feat_major(cache_dsa_idx), moba=feat_major(cache_moba_kv),
                c=state_mlstm_c, n=state_mlstm_n, m=state_mlstm_m, conv=state_mlstm_conv)
    x_s = _pad_rows(x_sample, t_pad)
    y_s, ev_s, od_s = _run_group(p, x_s, mod_s, t_dec, past, page_table, page_table.shape[1] * page, True)

    nsa_kv_p, nsa_win_p, dsa_kv_p, dsa_idx_p = ev_p
    nsa_kv_s, nsa_win_s, dsa_kv_s, dsa_idx_s = ev_s
    ml_c_p, ml_n_p, ml_m_p, ml_conv_p, moba_kv_p = od_p
    ml_c_s, ml_n_s, ml_m_s, ml_conv_s, moba_kv_s = od_s
    return (y_p, y_s,
            nsa_kv_p, nsa_kv_s, nsa_win_p, nsa_win_s, dsa_kv_p, dsa_kv_s, dsa_idx_p, dsa_idx_s,
            ml_c_p, ml_c_s, ml_n_p, ml_n_s, ml_m_p, ml_m_s, ml_conv_p, ml_conv_s, moba_kv_p, moba_kv_s)
```

```python
import functools

import numpy as np
import jax
import jax.numpy as jnp
from jax import lax
from jax.experimental import pallas as pl
from jax.experimental.pallas import tpu as pltpu

F32 = jnp.float32
BF16 = jnp.bfloat16
I32 = jnp.int32

DH = 64
N_HEADS = 8
HKV = 2
GROUP = N_HEADS // HKV
CMP_LEN = 32
CMP_STRIDE = 16
SEL_BLOCK = 64
N_SEL = 16
WINDOW = 512
DSA_TOPK = 256
CONV_W = 4
CHUNK = 64
MOBA_BLOCK = 256
MOBA_TOPK = 3
EPS = 1e-6
NEG = -1e30
TINY = 1e-30
FORCE = 1e4
SCALE = DH ** -0.5
LANES = 128
INT_MIN = -2 ** 31

VMEM_LIMIT = 56 * 1024 * 1024


def _cparams(sem):
    return pltpu.CompilerParams(dimension_semantics=sem, vmem_limit_bytes=VMEM_LIMIT)


def _dot(a, b):
    return jnp.dot(a, b, preferred_element_type=F32)


def _dot_nt(a, b):
    return lax.dot_general(a, b, (((1,), (1,)), ((), ())), preferred_element_type=F32)


def _split3(x):
    hi = x.astype(BF16)
    r = x - hi.astype(F32)
    mid = r.astype(BF16)
    lo = (r - mid.astype(F32)).astype(BF16)
    return hi, mid, lo


def _ada_kernel(c_ref, w_ref, b_ref, o_ref):
    c = c_ref[...]
    cs = (c * jax.nn.sigmoid(c)).astype(BF16)
    o_ref[0] = _dot(cs, w_ref[0]) + b_ref[0]


def ada_mod(c_all, ada_w_bf, ada_b):
    n_layers, d, n = ada_w_bf.shape
    r = c_all.shape[0]
    tn = 1536
    return pl.pallas_call(
        _ada_kernel,
        grid=(n_layers, n // tn),
        in_specs=[pl.BlockSpec((r, d), lambda l, j: (0, 0)),
                  pl.BlockSpec((1, d, tn), lambda l, j: (l, 0, j)),
                  pl.BlockSpec((1, 1, tn), lambda l, j: (l, 0, j))],
        out_specs=pl.BlockSpec((1, r, tn), lambda l, j: (l, 0, j)),
        out_shape=jax.ShapeDtypeStruct((n_layers, r, n), F32),
        compiler_params=_cparams(("arbitrary", "arbitrary")),
        name="ada_mod",
    )(c_all, ada_w_bf, ada_b.reshape(n_layers, 1, n))


def _mod_chunk(mod_ref, k, per_row):
    return mod_ref[k] if per_row else mod_ref[0, k:k + 1, :]


def _mod_spec(per_row, tm, d, tiles_per_batch):
    if per_row:
        return pl.BlockSpec((6, tm, d), lambda i, *_: (0, i, 0))
    return pl.BlockSpec((1, 6, d), lambda i, *_: (i // tiles_per_batch, 0, 0))


def _norm_mod(x, g, shift, scale):
    y = x * lax.rsqrt(jnp.mean(x * x, axis=-1, keepdims=True) + EPS) * g
    return y * (1.0 + scale) + shift


def _kin_kernel(per_row, groups, x_ref, mod_ref, g_ref, w_ref, b_ref, *refs):
    outs, h_scr = refs[:-1], refs[-1]
    h_scr[...] = _norm_mod(x_ref[...], g_ref[...], _mod_chunk(mod_ref, 0, per_row),
                           _mod_chunk(mod_ref, 1, per_row)).astype(BF16)
    for (c0, wpad, wout), o_ref in zip(groups, outs):
        z = _dot(h_scr[...], w_ref[:, c0:c0 + wpad]) + b_ref[:, c0:c0 + wpad]
        o_ref[...] = z[:, :wout]


def k_in(x2d, mod, g, w_bf, b, groups, tm, tiles_per_batch, per_row):
    m, d = x2d.shape
    wp = w_bf.shape[1]
    return pl.pallas_call(
        functools.partial(_kin_kernel, per_row, groups),
        grid=(m // tm,),
        in_specs=[pl.BlockSpec((tm, d), lambda i: (i, 0)),
                  _mod_spec(per_row, tm, d, tiles_per_batch),
                  pl.BlockSpec((1, d), lambda i: (0, 0)),
                  pl.BlockSpec((d, wp), lambda i: (0, 0)),
                  pl.BlockSpec((1, wp), lambda i: (0, 0))],
        out_specs=[pl.BlockSpec((tm, wout), lambda i: (i, 0)) for _, _, wout in groups],
        out_shape=[jax.ShapeDtypeStruct((m, wout), F32) for _, _, wout in groups],
        scratch_shapes=[pltpu.VMEM((tm, d), BF16)],
        compiler_params=_cparams(("arbitrary",)),
        name="k_in",
    )(x2d, mod, g.reshape(1, d), w_bf, b.reshape(1, wp))


def _kout_kernel(per_row, n_a, *refs):
    a_refs = refs[:n_a]
    b_ref, x_ref, mod_ref, w_ref, o_ref = refs[n_a:]
    a = a_refs[0][...]
    for r in a_refs[1:]:
        a = a + r[...]
    half = a.shape[1]
    y = _dot(a.astype(BF16), w_ref[:half, :]) + _dot(b_ref[...].astype(BF16), w_ref[half:, :])
    o_ref[...] = x_ref[...] + _mod_chunk(mod_ref, 2, per_row) * y


def k_out(a_list, b2d, x2d, mod, w_bf, tm, tiles_per_batch, per_row):
    m, d = x2d.shape
    half = b2d.shape[1]
    n_a = len(a_list)
    row_spec = pl.BlockSpec((tm, half), lambda i: (i, 0))
    return pl.pallas_call(
        functools.partial(_kout_kernel, per_row, n_a),
        grid=(m // tm,),
        in_specs=[row_spec] * (n_a + 1) + [
            pl.BlockSpec((tm, d), lambda i: (i, 0)),
            _mod_spec(per_row, tm, d, tiles_per_batch),
            pl.BlockSpec((2 * half, d), lambda i: (0, 0))],
        out_specs=pl.BlockSpec((tm, d), lambda i: (i, 0)),
        out_shape=jax.ShapeDtypeStruct((m, d), F32),
        compiler_params=_cparams(("arbitrary",)),
        name="k_out",
    )(*a_list, b2d, x2d, mod, w_bf)


def _mlp_kernel(per_row, final, x_ref, mod_ref, g_ref, fg_ref, w1_ref, w2_ref, o_ref, h_scr, acc_scr):
    f = pl.program_id(1)

    @pl.when(f == 0)
    def _():
        h_scr[...] = _norm_mod(x_ref[...], g_ref[...], _mod_chunk(mod_ref, 3, per_row),
                               _mod_chunk(mod_ref, 4, per_row)).astype(BF16)
        acc_scr[...] = jnp.zeros_like(acc_scr)

    a = jnp.maximum(_dot(h_scr[...], w1_ref[...]), 0.0)
    acc_scr[...] += _dot((a * a).astype(BF16), w2_ref[...])

    @pl.when(f == pl.num_programs(1) - 1)
    def _():
        xn = x_ref[...] + _mod_chunk(mod_ref, 5, per_row) * acc_scr[...]
        if final:
            xn = xn * lax.rsqrt(jnp.mean(xn * xn, axis=-1, keepdims=True) + EPS) * fg_ref[...]
        o_ref[...] = xn


def k_mlp(x2d, mod, g, final_g, w1_bf, w2_bf, tm, tf, tiles_per_batch, per_row, final):
    m, d = x2d.shape
    dff = w1_bf.shape[1]
    return pl.pallas_call(
        functools.partial(_mlp_kernel, per_row, final),
        grid=(m // tm, dff // tf),
        in_specs=[pl.BlockSpec((tm, d), lambda i, f: (i, 0)),
                  _mod_spec(per_row, tm, d, tiles_per_batch),
                  pl.BlockSpec((1, d), lambda i, f: (0, 0)),
                  pl.BlockSpec((1, d), lambda i, f: (0, 0)),
                  pl.BlockSpec((d, tf), lambda i, f: (0, f)),
                  pl.BlockSpec((tf, d), lambda i, f: (f, 0))],
        out_specs=pl.BlockSpec((tm, d), lambda i, f: (i, 0)),
        out_shape=jax.ShapeDtypeStruct((m, d), F32),
        scratch_shapes=[pltpu.VMEM((tm, d), BF16), pltpu.VMEM((tm, d), F32)],
        compiler_params=_cparams(("arbitrary", "arbitrary")),
        name="k_mlp",
    )(x2d, mod, g.reshape(1, d), final_g.reshape(1, d), w1_bf, w2_bf)


MASK_BIG = 2e30


def _qz_block(q_ref, h, scale):
    pair = q_ref[0, :, (h // 2) * LANES:(h // 2 + 1) * LANES]
    if (h % 2) != (h // GROUP):
        pair = pltpu.roll(pair, DH, 1)
    lane = lax.broadcasted_iota(I32, pair.shape, 1)
    keep = (lane < DH) if h // GROUP == 0 else (lane >= DH)
    return jnp.where(keep, pair * scale, 0.0).astype(BF16)


def _assemble_heads(o_list):
    lane = lax.broadcasted_iota(I32, o_list[0].shape, 1)
    pairs = []
    for p in range(N_HEADS // 2):
        a, b = o_list[2 * p], o_list[2 * p + 1]
        if (2 * p) // GROUP != 0:
            a = pltpu.roll(a, DH, 1)
        if (2 * p + 1) // GROUP != 1:
            b = pltpu.roll(b, DH, 1)
        pairs.append(jnp.where(lane < DH, a, b))
    return jnp.concatenate(pairs, axis=1)


def _gate(g_ref, col):
    return jax.nn.sigmoid(g_ref[0, :, col:col + 1])


def _lanes(x, n):
    return x if n == LANES else jnp.concatenate([x] * (n // LANES), axis=1)


def _flash_init(m_scr, l_scr, acc_scr):
    m_scr[...] = jnp.full(m_scr.shape, NEG, F32)
    l_scr[...] = jnp.zeros(l_scr.shape, F32)
    acc_scr[...] = jnp.zeros(acc_scr.shape, F32)


def _flash_chunk(s_all, keep, v_bf, tq, m_scr, l_scr, acc_scr):
    ck = s_all.shape[1]
    ps, alphas = [], []
    for h in range(N_HEADS):
        r0 = h * tq
        s = s_all[r0:r0 + tq, :]
        if keep is not None:
            s = jnp.where(keep, s, -MASK_BIG)
        m_old = m_scr[r0:r0 + tq, :]
        m_new = jnp.maximum(m_old, jnp.max(s, axis=-1, keepdims=True))
        p = jnp.exp(s - _lanes(m_new, ck))
        alpha = jnp.exp(m_old - m_new)
        l_scr[r0:r0 + tq, :] = alpha * l_scr[r0:r0 + tq, :] + jnp.sum(p, axis=-1, keepdims=True)
        m_scr[r0:r0 + tq, :] = m_new
        ps.append(p.astype(BF16))
        alphas.append(alpha)
    acc_scr[...] = jnp.concatenate(alphas, axis=0) * acc_scr[...] + _dot(jnp.concatenate(ps, axis=0), v_bf)


def _flash_heads(l_scr, acc_scr, tq):
    return [acc_scr[h * tq:(h + 1) * tq, :] / jnp.maximum(l_scr[h * tq:(h + 1) * tq, :], TINY)
            for h in range(N_HEADS)]


def _sort_key(x):
    key = pltpu.bitcast(x + 0.0, I32)
    return jnp.where(key < 0, key ^ jnp.int32(0x7FFFFFFF), key)


_HALF_NEG_KEY = int(np.float32(NEG / 2).view(np.int32) ^ 0x7FFFFFFF)


def _topk_mask(s, k):
    r, n = s.shape
    key = _sort_key(s)

    def step(it, t):
        shift = 30 - 2 * it
        digit = jnp.zeros((r, LANES), I32)
        for c in (1, 2, 3):
            cand = t + lax.shift_left(jnp.int32(c), shift)
            cnt = jnp.sum((key >= _lanes(cand, n)).astype(F32), axis=-1, keepdims=True)
            digit = digit + (cnt >= k).astype(I32)
        return t + lax.shift_left(digit, shift)

    t = _lanes(lax.fori_loop(0, 16, step, jnp.full((r, LANES), INT_MIN, I32)), n)
    gt = key > t
    eq = key == t
    need = k - jnp.sum(gt.astype(F32), axis=-1, keepdims=True)
    lower = (lax.broadcasted_iota(I32, (n, n), 0) < lax.broadcasted_iota(I32, (n, n), 1))
    before = _dot(eq.astype(BF16), lower.astype(BF16))
    return gt | (eq & (before < need))


def _rank_topk_mask(s, k, n_valid):
    nv = -(-n_valid // 8) * 8
    st = s.T
    rows = [st[8 * v:8 * v + 8, :] for v in range(nv // 8)]
    sub = lax.broadcasted_iota(I32, (8, 1), 0)
    rank = [jnp.zeros(rows[0].shape, F32) for _ in rows]
    for i in range(nv):
        cand = rows[i // 8][i % 8:i % 8 + 1, :]
        for v in range(nv // 8):
            ge = jnp.where(cand >= rows[v], 1.0, 0.0)
            gt = jnp.where(cand > rows[v], 1.0, 0.0)
            if v > i // 8:
                rank[v] = rank[v] + ge
            elif v < i // 8:
                rank[v] = rank[v] + gt
            else:
                rank[v] = rank[v] + jnp.where(sub > i % 8, ge, gt)
    sel_t = jnp.concatenate([jnp.where(r < k, 1.0, 0.0) for r in rank]
                            + [jnp.zeros((s.shape[1] - nv, s.shape[0]), F32)], axis=0)
    return sel_t.T > 0.5


def _top_few_mask(s, k):
    idx = lax.broadcasted_iota(I32, s.shape, 1).astype(F32)
    sel = jnp.zeros(s.shape, jnp.bool_)
    cur = s
    for _ in range(k):
        m = jnp.max(cur, axis=-1, keepdims=True)
        first = jnp.min(jnp.where(cur == m, idx, 3e38), axis=-1, keepdims=True)
        pick = idx == first
        sel = sel | pick
        cur = jnp.where(pick, -3e38, cur)
    return sel


def _battn_kernel(cfg, q_ref, k_ref, v_ref, bm_ref, g_ref, o_ref, qa_scr, m_scr, l_scr, acc_scr):
    tq, ck, nbp, mg = cfg["tq"], cfg["ck"], cfg["nbp"], cfg["mask_group"]
    lp = k_ref.shape[1]
    get_k = lambda k0: k_ref[0, pl.ds(k0, ck), :]
    get_v = lambda k0: v_ref[0, pl.ds(k0, ck), :]
    qbase = cfg["q0"] + pl.program_id(1) * tq
    for h in range(N_HEADS):
        g = h // mg
        qa_scr[h * tq:(h + 1) * tq, 0:LANES] = _qz_block(q_ref, h, SCALE)
        qa_scr[h * tq:(h + 1) * tq, LANES:LANES + nbp] = bm_ref[0, :, g * nbp:(g + 1) * nbp] - 1
    _flash_init(m_scr, l_scr, acc_scr)
    qpos = qbase + lax.broadcasted_iota(I32, (tq, 1), 0)
    last = jnp.minimum((qbase + tq - 1) // ck, lp // ck - 1)

    def chunk(c, causal):
        k0 = pl.multiple_of(c * ck, ck)
        blk = lax.shift_right_logical(k0 + lax.broadcasted_iota(I32, (ck, 1), 0), cfg["bshift"])
        bias = jnp.where(lax.broadcasted_iota(I32, (ck, nbp), 1) == blk, MASK_BIG, 0.0).astype(BF16)
        k_aug = jnp.concatenate([get_k(k0).astype(BF16), bias], axis=1)
        s_all = _dot_nt(qa_scr[...], k_aug)
        keep = ((k0 + lax.broadcasted_iota(I32, (1, ck), 1)) <= qpos) if causal else None
        _flash_chunk(s_all, keep, get_v(k0).astype(BF16), tq, m_scr, l_scr, acc_scr)

    def body(c, carry):
        chunk(c, False)
        return carry

    lax.fori_loop(0, last, body, 0)
    chunk(last, True)
    heads = _flash_heads(l_scr, acc_scr, tq)
    if cfg["gate_col"] is not None:
        heads = [o * _gate(g_ref, cfg["gate_col"] + h) for h, o in enumerate(heads)]
    o_ref[0] = _assemble_heads(heads)


def block_attention(q, kv, k_col, v_col, gates, bmask, *, tq, ck, q0, nbp, bshift, mask_group, gate_col):
    b, t, dq = q.shape
    lp = kv.shape[1]
    assert ck % tq == 0 and q0 % tq == 0 and lp % ck == 0
    cfg = dict(tq=tq, ck=ck, q0=q0, nbp=nbp, bshift=bshift, mask_group=mask_group, gate_col=gate_col)
    rows = N_HEADS * tq
    im = lambda bi, i: (bi, i, 0)
    return pl.pallas_call(
        functools.partial(_battn_kernel, cfg),
        grid=(b, t // tq),
        in_specs=[pl.BlockSpec((1, tq, dq), im),
                  pl.BlockSpec((1, lp, LANES), lambda bi, i: (bi, 0, k_col)),
                  pl.BlockSpec((1, lp, LANES), lambda bi, i: (bi, 0, v_col)),
                  pl.BlockSpec((1, tq, bmask.shape[2]), im), pl.BlockSpec((1, tq, LANES), im)],
        out_specs=pl.BlockSpec((1, tq, dq), im),
        out_shape=jax.ShapeDtypeStruct((b, t, dq), F32),
        scratch_shapes=[pltpu.VMEM((rows, LANES + nbp), BF16), pltpu.VMEM((rows, LANES), F32),
                        pltpu.VMEM((rows, LANES), F32), pltpu.VMEM((rows, LANES), F32)],
        compiler_params=_cparams(("arbitrary", "arbitrary")),
        name="attn_block",
    )(q, kv, kv, bmask, gates)


def _win_kernel(cfg, q_ref, k_ref, v_ref, g_ref, o_ref):
    tq, ckw, koff = cfg["tq"], cfg["ckw"], cfg["koff"]
    lp = k_ref.shape[1]
    qbase = cfg["q0"] + pl.program_id(1) * tq
    start = pl.multiple_of(jnp.clip(qbase - koff - WINDOW, 0, lp - ckw), 8)
    k_bf = k_ref[0, pl.ds(start, ckw), :].astype(BF16)
    v_bf = v_ref[0, pl.ds(start, ckw), :].astype(BF16)
    kpos = koff + start + lax.broadcasted_iota(I32, (1, ckw), 1)
    qpos = qbase + lax.broadcasted_iota(I32, (tq, 1), 0)
    mask = (kpos <= qpos) & (qpos - kpos < WINDOW) & (kpos >= 0)
    s_all = _dot_nt(jnp.concatenate([_qz_block(q_ref, h, SCALE) for h in range(N_HEADS)], axis=0), k_bf)
    es, ls = [], []
    for h in range(N_HEADS):
        s = jnp.where(mask, s_all[h * tq:(h + 1) * tq, :], NEG)
        e = jnp.where(mask, jnp.exp(s - jnp.max(s, axis=-1, keepdims=True)), 0.0)
        ls.append(jnp.maximum(jnp.sum(e, axis=-1, keepdims=True), TINY))
        es.append(e.astype(BF16))
    o_all = _dot(jnp.concatenate(es, axis=0), v_bf)
    heads = [o_all[h * tq:(h + 1) * tq, :] / ls[h] * _gate(g_ref, cfg["gate_col"] + h) for h in range(N_HEADS)]
    o_ref[0] = _assemble_heads(heads)


def window_attention(q, kv, gates, *, tq, ckw, q0, koff, gate_col):
    b, t, dq = q.shape
    lp = kv.shape[1]
    cfg = dict(tq=tq, ckw=ckw, q0=q0, koff=koff, gate_col=gate_col)
    return pl.pallas_call(
        functools.partial(_win_kernel, cfg),
        grid=(b, t // tq),
        in_specs=[pl.BlockSpec((1, tq, dq), lambda bi, i: (bi, i, 0)),
                  pl.BlockSpec((1, lp, LANES), lambda bi, i: (bi, 0, 0)),
                  pl.BlockSpec((1, lp, LANES), lambda bi, i: (bi, 0, 1)),
                  pl.BlockSpec((1, tq, LANES), lambda bi, i: (bi, i, 0))],
        out_specs=pl.BlockSpec((1, tq, dq), lambda bi, i: (bi, i, 0)),
        out_shape=jax.ShapeDtypeStruct((b, t, dq), F32),
        compiler_params=_cparams(("arbitrary", "arbitrary")),
        name="attn_window",
    )(q, kv, kv, gates)


def _compress_kernel(n_groups, x_ref, pe_ref, w_ref, o_ref, a_scr, b_scr):
    acc_a = jnp.zeros((n_groups, LANES), F32)
    acc_b = jnp.zeros((n_groups, LANES), F32)
    for l in range(CMP_STRIDE):
        x = x_ref[0, pl.ds(l, n_groups, stride=CMP_STRIDE), :]
        acc_a += _dot((x + pe_ref[0, l:l + 1, :]).astype(BF16), w_ref[0, l])
        acc_b += _dot((x + pe_ref[0, CMP_STRIDE + l:CMP_STRIDE + l + 1, :]).astype(BF16),
                      w_ref[0, CMP_STRIDE + l])
    a_scr[...] = acc_a
    b_scr[0:n_groups, :] = acc_b
    b_scr[n_groups:n_groups + 8, :] = jnp.zeros((8, LANES), F32)
    o_ref[0] = jnp.zeros(o_ref.shape[1:], F32)
    o_ref[0, 0:n_groups, :] = a_scr[...] + b_scr[pl.ds(1, n_groups), :]


def nsa_compress(kv, pe2, w_bd, ncp):
    b, lp, _ = kv.shape
    n_groups = lp // CMP_STRIDE
    return pl.pallas_call(
        functools.partial(_compress_kernel, n_groups),
        grid=(b, 2),
        in_specs=[pl.BlockSpec((1, lp, LANES), lambda bi, j: (bi, 0, j)),
                  pl.BlockSpec((1, CMP_LEN, LANES), lambda bi, j: (j, 0, 0)),
                  pl.BlockSpec((1, CMP_LEN, LANES, LANES), lambda bi, j: (j, 0, 0, 0))],
        out_specs=pl.BlockSpec((1, ncp, LANES), lambda bi, j: (bi, 0, j)),
        out_shape=jax.ShapeDtypeStruct((b, ncp, 2 * LANES), F32),
        scratch_shapes=[pltpu.VMEM((n_groups, LANES), F32), pltpu.VMEM((n_groups + 8, LANES), F32)],
        compiler_params=_cparams(("arbitrary", "arbitrary")),
        name="nsa_compress",
    )(kv, pe2, w_bd)


def _nsa_select_kernel(cfg, q_ref, c_ref, g_ref, o_ref, bm_ref):
    tq, nc, nsp = cfg["tq"], cfg["nc"], cfg["nsp"]
    ncp = c_ref.shape[1]
    qbase = cfg["q0"] + pl.program_id(1) * tq
    qpos = qbase + lax.broadcasted_iota(I32, (tq, 1), 0)
    kc = c_ref[0, :, 0:LANES].astype(BF16)
    vc = c_ref[0, :, LANES:2 * LANES].astype(BF16)
    n = lax.broadcasted_iota(I32, (1, ncp), 1)
    mask = (n * CMP_STRIDE + (CMP_LEN - 1) <= qpos) & (n < nc)
    s_all = _dot_nt(jnp.concatenate([_qz_block(q_ref, h, SCALE) for h in range(N_HEADS)], axis=0), kc)
    ps, psum = [], [None] * HKV
    for h in range(N_HEADS):
        s = jnp.where(mask, s_all[h * tq:(h + 1) * tq, :], NEG)
        e = jnp.where(mask, jnp.exp(s - jnp.max(s, axis=-1, keepdims=True)), 0.0)
        p = e / jnp.maximum(jnp.sum(e, axis=-1, keepdims=True), TINY)
        ps.append(p.astype(BF16))
        psum[h // GROUP] = p if psum[h // GROUP] is None else psum[h // GROUP] + p
    o_all = _dot(jnp.concatenate(ps, axis=0), vc)
    o_ref[0] = _assemble_heads([o_all[h * tq:(h + 1) * tq, :] * _gate(g_ref, cfg["gate_col"] + h)
                                for h in range(N_HEADS)])
    cs = lax.broadcasted_iota(I32, (ncp, nsp), 0) * CMP_STRIDE
    ss = lax.broadcasted_iota(I32, (ncp, nsp), 1) * SEL_BLOCK
    overlap = ((cs < ss + SEL_BLOCK) & (cs + CMP_LEN > ss)).astype(BF16)
    hi, mid, lo = _split3(jnp.concatenate(psum, axis=0))
    imp = _dot(hi, overlap) + _dot(mid, overlap) + _dot(lo, overlap)
    j = lax.broadcasted_iota(I32, (1, nsp), 1)
    blk = jnp.concatenate([qpos // SEL_BLOCK] * HKV, axis=0)
    forced = (j == 0) | (j == blk) | (j == blk - 1)
    score = jnp.where(j <= blk, imp + jnp.where(forced, FORCE, 0.0), NEG)
    if tq == LANES and nsp == LANES:
        top = jnp.concatenate([_rank_topk_mask(score[g * tq:(g + 1) * tq, :], N_SEL, cfg["ns"])
                               for g in range(HKV)], axis=0)
    else:
        top = _topk_mask(score, N_SEL)
    sel = jnp.where(top & (score > NEG / 2), 1.0, 0.0).astype(BF16)
    for g in range(HKV):
        bm_ref[0, :, g * nsp:(g + 1) * nsp] = sel[g * tq:(g + 1) * tq, :]


def nsa_select(q, cmp_kv, gates, *, tq, q0, nc, ns, nsp, gate_col):
    b, t, dq = q.shape
    ncp = cmp_kv.shape[1]
    cfg = dict(tq=tq, q0=q0, nc=nc, ns=ns, nsp=nsp, gate_col=gate_col)
    return pl.pallas_call(
        functools.partial(_nsa_select_kernel, cfg),
        grid=(b, t // tq),
        in_specs=[pl.BlockSpec((1, tq, dq), lambda bi, i: (bi, i, 0)),
                  pl.BlockSpec((1, ncp, 2 * LANES), lambda bi, i: (bi, 0, 0)),
                  pl.BlockSpec((1, tq, LANES), lambda bi, i: (bi, i, 0))],
        out_specs=[pl.BlockSpec((1, tq, dq), lambda bi, i: (bi, i, 0)),
                   pl.BlockSpec((1, tq, HKV * nsp), lambda bi, i: (bi, i, 0))],
        out_shape=[jax.ShapeDtypeStruct((b, t, dq), F32),
                   jax.ShapeDtypeStruct((b, t, HKV * nsp), BF16)],
        compiler_params=_cparams(("arbitrary", "arbitrary")),
        name="nsa_select",
    )(q, cmp_kv, gates)


def _dsa_kernel(cfg, q_ref, iq_ref, g_ref, ik_ref, k_ref, v_ref, o_ref,
                qz_scr, iq_scr, iw_scr, key_scr, low_scr, m_scr, l_scr, acc_scr):
    tq, ck, n_top = cfg["tq"], cfg["ck"], cfg["n_top"]
    lp = k_ref.shape[1]
    get_i = lambda k0: ik_ref[0, pl.ds(k0, ck), :]
    get_k = lambda k0: k_ref[0, pl.ds(k0, ck), :]
    get_v = lambda k0: v_ref[0, pl.ds(k0, ck), :]
    qbase = cfg["q0"] + pl.program_id(1) * tq
    qpos = qbase + lax.broadcasted_iota(I32, (tq, 1), 0)
    hi_chunk = jnp.minimum((qbase + tq - 1) // ck + 1, lp // ck)
    for h in range(N_HEADS):
        qz_scr[h * tq:(h + 1) * tq, :] = _qz_block(q_ref, h, SCALE)
        pair = iq_ref[0, :, (h // 2) * LANES:(h // 2 + 1) * LANES]
        if h % 2:
            pair = pltpu.roll(pair, DH, 1)
        iq_scr[h * tq:(h + 1) * tq, :] = pair[:, :DH].astype(BF16)
        iw_scr[h * tq:(h + 1) * tq, :] = jnp.broadcast_to(g_ref[0, :, h:h + 1], (tq, LANES))
    low_scr[...] = (lax.broadcasted_iota(I32, (ck, ck), 0)
                    < lax.broadcasted_iota(I32, (ck, ck), 1)).astype(BF16)

    def score_body(c, carry):
        k0 = pl.multiple_of(c * ck, ck)
        s_all = jnp.maximum(_dot_nt(iq_scr[...], get_i(k0).astype(BF16)), 0.0)
        sc = s_all[0:tq, :] * _lanes(iw_scr[0:tq, :], ck)
        for h in range(1, N_HEADS):
            sc = sc + s_all[h * tq:(h + 1) * tq, :] * _lanes(iw_scr[h * tq:(h + 1) * tq, :], ck)
        kpos = k0 + lax.broadcasted_iota(I32, (1, ck), 1)
        key_scr[c] = _sort_key(jnp.where(kpos <= qpos, sc, NEG))
        return carry

    lax.fori_loop(0, hi_chunk, score_body, 0)

    def count(pred, thr):
        thr_w = _lanes(thr, ck)

        def body(c, a):
            hit = pred(key_scr[c], thr_w).astype(F32)
            for j in range(ck // LANES):
                a = a + hit[:, j * LANES:(j + 1) * LANES]
            return a
        a = lax.fori_loop(0, hi_chunk, body, jnp.zeros((tq, LANES), F32))
        return jnp.sum(a, axis=-1, keepdims=True)

    def bit_step(it, t):
        cand = t + lax.shift_left(jnp.int32(1), 31 - it)
        return jnp.where(count(lambda kk, th: kk >= th, cand) >= n_top, cand, t)

    t = lax.fori_loop(0, 32, bit_step, jnp.full((tq, LANES), INT_MIN, I32))
    need = n_top - count(lambda kk, th: kk > th, t)
    t_w = _lanes(t, ck)
    _flash_init(m_scr, l_scr, acc_scr)

    def attn_body(c, before):
        k0 = pl.multiple_of(c * ck, ck)
        kk = key_scr[c]
        eq = kk == t_w
        rank_eq = before + _dot(eq.astype(BF16), low_scr[...])
        mask = ((kk > t_w) | (eq & (rank_eq < need))) & (kk > _HALF_NEG_KEY)
        s_all = _dot_nt(qz_scr[...], get_k(k0).astype(BF16))
        _flash_chunk(s_all, mask, get_v(k0).astype(BF16), tq, m_scr, l_scr, acc_scr)
        return before + jnp.sum(eq.astype(F32), axis=-1, keepdims=True)

    lax.fori_loop(0, hi_chunk, attn_body, jnp.zeros((tq, 1), F32))
    o_ref[0] = _assemble_heads(_flash_heads(l_scr, acc_scr, tq))


def dsa_attention(q, iq, misc, ik, kv, *, tq, ck, q0, n_top):
    b, t, dq = q.shape
    lp = kv.shape[1]
    cfg = dict(tq=tq, ck=ck, q0=q0, n_top=n_top)
    qspec = pl.BlockSpec((1, tq, dq), lambda bi, i: (bi, i, 0))
    rows = N_HEADS * tq
    return pl.pallas_call(
        functools.partial(_dsa_kernel, cfg),
        grid=(b, t // tq),
        in_specs=[qspec, qspec,
                  pl.BlockSpec((1, tq, LANES), lambda bi, i: (bi, i, 0)),
                  pl.BlockSpec((1, lp, DH), lambda bi, i: (bi, 0, 0)),
                  pl.BlockSpec((1, lp, LANES), lambda bi, i: (bi, 0, 0)),
                  pl.BlockSpec((1, lp, LANES), lambda bi, i: (bi, 0, 1))],
        out_specs=qspec,
        out_shape=jax.ShapeDtypeStruct((b, t, dq), F32),
        scratch_shapes=[pltpu.VMEM((rows, LANES), BF16), pltpu.VMEM((rows, DH), BF16),
                        pltpu.VMEM((rows, LANES), F32), pltpu.VMEM((lp // ck, tq, ck), I32),
                        pltpu.VMEM((ck, ck), BF16), pltpu.VMEM((rows, LANES), F32),
                        pltpu.VMEM((rows, LANES), F32), pltpu.VMEM((rows, LANES), F32)],
        compiler_params=_cparams(("arbitrary", "arbitrary")),
        name="dsa_attention",
    )(q, iq, misc, ik, kv, kv)


def _moba_select_kernel(cfg, q_ref, k_ref, bm_ref, km_scr):
    tq, nblk, nbp, n_top = cfg["tq"], cfg["nblk"], cfg["nbp"], cfg["n_top"]
    lp = k_ref.shape[1]
    qpos = cfg["q0"] + pl.program_id(1) * tq + lax.broadcasted_iota(I32, (tq, 1), 0)
    km_scr[...] = jnp.zeros(km_scr.shape, F32)
    for j in range(nblk):
        r1 = min((j + 1) * MOBA_BLOCK, lp)
        km_scr[j:j + 1, :] = jnp.sum(k_ref[0, j * MOBA_BLOCK:r1, :], axis=0, keepdims=True) * (1.0 / MOBA_BLOCK)
    km = km_scr[...].astype(BF16)
    own = jnp.concatenate([qpos // MOBA_BLOCK] * N_HEADS, axis=0)
    j = lax.broadcasted_iota(I32, (1, nbp), 1)
    s = _dot_nt(jnp.concatenate([_qz_block(q_ref, h, 1.0) for h in range(N_HEADS)], axis=0), km)
    s = jnp.where(j < own, s, NEG)
    sel = (j == own)
    if n_top > 0:
        sel = sel | (_top_few_mask(s, n_top) & (s > NEG / 2))
    sel = sel.astype(BF16)
    for h in range(N_HEADS):
        bm_ref[0, :, h * nbp:(h + 1) * nbp] = sel[h * tq:(h + 1) * tq, :]


def moba_select(q, kv, *, tq, q0, nblk, nbp):
    b, t, dq = q.shape
    lp = kv.shape[1]
    cfg = dict(tq=tq, q0=q0, nblk=nblk, nbp=nbp, n_top=min(MOBA_TOPK, nblk - 1))
    return pl.pallas_call(
        functools.partial(_moba_select_kernel, cfg),
        grid=(b, t // tq),
        in_specs=[pl.BlockSpec((1, tq, dq), lambda bi, i: (bi, i, 0)),
                  pl.BlockSpec((1, lp, LANES), lambda bi, i: (bi, 0, 0))],
        out_specs=pl.BlockSpec((1, tq, N_HEADS * nbp), lambda bi, i: (bi, i, 0)),
        out_shape=jax.ShapeDtypeStruct((b, t, N_HEADS * nbp), BF16),
        scratch_shapes=[pltpu.VMEM((nbp, LANES), F32)],
        compiler_params=_cparams(("arbitrary", "arbitrary")),
        name="moba_select",
    )(q, kv)


def _log_sigmoid(x):
    return jnp.minimum(x, 0.0) - jnp.log(1.0 + jnp.exp(-jnp.abs(x)))


def _dot3_rhs(a_bf, x):
    hi, mid, lo = _split3(x)
    return _dot(a_bf, hi) + _dot(a_bf, mid) + _dot(a_bf, lo)


def _dot3_lhs(x, b_bf):
    hi, mid, lo = _split3(x)
    return _dot(hi, b_bf) + _dot(mid, b_bf) + _dot(lo, b_bf)


def _mlstm_kernel(cfg, u_ref, v_ref, og_ref, g_ref, cin_ref, ct0_ref, n0_ref, m0_ref,
                  cw_ref, cb_ref, wq_ref, wk_ref, fb_ref, ng_ref,
                  h_out, ct_out, n_out, m_out,
                  ubuf, q_scr, k_scr, v_scr, g_scr, h_scr, ct_scr, n_scr, m_scr, hm_scr):
    tc, tcp, t_valid = cfg["tc"], cfg["tcp"], cfg["t_valid"]
    d_c = N_HEADS * DH
    i = pl.program_id(1)

    @pl.when(i == 0)
    def _():
        ubuf[0:8, :] = cin_ref[0]
        ct_scr[...] = ct0_ref[0]
        n_scr[...] = n0_ref[0]
        m_scr[...] = m0_ref[0]
        hm_scr[...] = (lax.broadcasted_iota(I32, (d_c, d_c), 0) // DH
                       == lax.broadcasted_iota(I32, (d_c, d_c), 1) // DH).astype(F32)

    if tc < tcp:
        ubuf[8:, :] = jnp.zeros((tcp, d_c), F32)
        v_scr[...] = jnp.zeros((tcp, d_c), F32)
        g_scr[...] = jnp.zeros((tcp, LANES), F32)
    ubuf[8:8 + tc, :] = u_ref[0]
    v_scr[0:tc, :] = v_ref[0]
    g_scr[0:tc, :] = g_ref[0]
    conv = ubuf[pl.ds(CONV_W + 1, tcp), :] * cw_ref[0:1, :]
    for j in range(1, CONV_W):
        conv = conv + ubuf[pl.ds(CONV_W + 1 + j, tcp), :] * cw_ref[j:j + 1, :]
    conv = conv + cb_ref[...]
    uc = (conv * jax.nn.sigmoid(conv)).astype(BF16)
    q_scr[...] = _dot(uc, wq_ref[...])
    k_scr[...] = _dot(uc, wk_ref[...]) * SCALE

    lane = lax.broadcasted_iota(I32, (1, LANES), 1)
    head_lane = lane < N_HEADS
    t_io = lax.broadcasted_iota(I32, (CHUNK, 1), 0)
    causal = lane <= t_io
    tri = (lax.broadcasted_iota(I32, (CHUNK, CHUNK), 1)
           <= lax.broadcasted_iota(I32, (CHUNK, CHUNK), 0)).astype(BF16)
    tri_t = (lax.broadcasted_iota(I32, (LANES, LANES), 0)
             <= lax.broadcasted_iota(I32, (LANES, LANES), 1)).astype(BF16)
    expand = (lax.broadcasted_iota(I32, (LANES, d_c), 0)
              == lax.broadcasted_iota(I32, (LANES, d_c), 1) // DH).astype(BF16)
    zeros_gate = jnp.zeros((CHUNK, LANES), F32)
    zeros_feat = jnp.zeros((CHUNK, d_c), F32)

    def chunk_body(c, carry):
        r0 = pl.multiple_of(c * CHUNK, CHUNK)
        hm = hm_scr[...]
        g = g_scr[pl.ds(r0, CHUNK), :]
        valid = (i * tc + r0 + t_io) < t_valid
        ig = jnp.where(head_lane, jnp.where(valid, g, NEG), 0.0)
        lf = pltpu.roll(_log_sigmoid(g + fb_ref[...]), LANES - N_HEADS, 1)
        lf = jnp.where(head_lane & valid, lf, 0.0)
        b_col = _dot3_rhs(tri, lf)
        ig_t = jnp.concatenate([ig, zeros_gate], axis=0).T[0:8, :]
        lf_t = jnp.concatenate([lf, zeros_gate], axis=0).T[0:8, :]
        rowterm = ig_t - _dot3_lhs(lf_t, tri_t)
        m_row = m_scr[...]
        dws, iws, emts = [], [], []
        for h in range(N_HEADS):
            bc = b_col[:, h:h + 1]
            dlog = jnp.where(causal, bc + rowterm[h:h + 1, :], NEG)
            inter = bc + m_row[:, h:h + 1]
            m_t = jnp.maximum(inter, jnp.max(dlog, axis=-1, keepdims=True))
            dws.append(jnp.exp(dlog - m_t))
            iws.append(jnp.exp(inter - m_t))
            emts.append(jnp.exp(-m_t))
        dw = jnp.concatenate(dws, axis=0)
        iw = jnp.concatenate(iws, axis=0)
        emt = jnp.concatenate(emts, axis=0)
        q_c = q_scr[pl.ds(r0, CHUNK), :]
        k_c = k_scr[pl.ds(r0, CHUNK), :]
        v_c = v_scr[pl.ds(r0, CHUNK), :]
        qz = jnp.concatenate([q_c] * N_HEADS, axis=0) * hm
        qz_bf = qz.astype(BF16)
        k_pad = jnp.concatenate([k_c, zeros_feat], axis=0)
        v_pad = jnp.concatenate([v_c, zeros_feat], axis=0).astype(BF16)
        qkw = _dot_nt(qz_bf, k_pad.astype(BF16)) * dw
        intra = _dot(qkw.astype(BF16), v_pad)
        inter_z = _dot(qz_bf, ct_scr[...].astype(BF16))
        num = iw * inter_z + intra * hm
        den = iw * jnp.sum(qz * n_scr[...], axis=-1, keepdims=True) + jnp.sum(qkw, axis=-1, keepdims=True)
        hz = num / jnp.maximum(jnp.abs(den), emt)
        h_c = hz[0:CHUNK, :]
        for h in range(1, N_HEADS):
            h_c = h_c + hz[h * CHUNK:(h + 1) * CHUNK, :]
        h_scr[pl.ds(r0, CHUNK), :] = h_c
        b_last = b_col[CHUNK - 1:CHUNK, :]
        m_new = jnp.maximum(b_last + m_row, jnp.max(b_last - b_col + ig, axis=0, keepdims=True))
        decay = jnp.where(head_lane, jnp.exp(b_last + m_row - m_new), 0.0)
        ws = jnp.where(head_lane, jnp.exp(b_last - b_col + ig - m_new), 0.0)
        wide = _dot3_lhs(jnp.concatenate([ws, jnp.broadcast_to(decay, (8, LANES))], axis=0), expand)
        w8, decay_w = wide[0:CHUNK, :], wide[CHUNK:CHUNK + 1, :]
        vw_pad = jnp.concatenate([v_c * w8, zeros_feat], axis=0).astype(BF16)
        k_t = jnp.concatenate([k_pad[:, j * LANES:(j + 1) * LANES].T for j in range(d_c // LANES)], axis=0)
        ct_scr[...] = decay_w * ct_scr[...] + _dot(k_t.astype(BF16), vw_pad) * hm
        n_scr[...] = decay_w * n_scr[...] + jnp.sum(k_c * w8, axis=0, keepdims=True)
        m_scr[...] = m_new
        return carry

    lax.fori_loop(0, tcp // CHUNK, chunk_body, 0)
    h_all = h_scr[...]
    hi, mid, lo = _split3(h_all * h_all)
    hm_bf = hm_scr[...].astype(BF16)
    ms = (_dot(hi, hm_bf) + _dot(mid, hm_bf) + _dot(lo, hm_bf)) * (1.0 / DH)
    hc = h_all * lax.rsqrt(ms + EPS) * ng_ref[...]
    h_out[0] = jax.nn.sigmoid(og_ref[0]) * hc[0:tc, :]
    ubuf[0:8, :] = ubuf[tc:tc + 8, :]

    @pl.when(i == pl.num_programs(1) - 1)
    def _():
        ct_out[0] = ct_scr[...]
        n_out[0] = n_scr[...]
        m_out[0] = m_scr[...]


def mlstm(u, v, og, misc, conv_in, ct0, n0, m0, conv_w, conv_b, wq_bd, wk_bd, fb_row, norm_g, *, tc, t_valid):
    b, t, d_c = u.shape
    tcp = max(tc, CHUNK)
    cfg = dict(tc=tc, tcp=tcp, t_valid=t_valid)
    row = pl.BlockSpec((1, tc, d_c), lambda bi, i: (bi, i, 0))
    const = lambda shape: pl.BlockSpec(shape, lambda bi, i: (0,) * len(shape))
    per_b = lambda shape: pl.BlockSpec((1,) + shape, lambda bi, i: (bi,) + (0,) * len(shape))
    return pl.pallas_call(
        functools.partial(_mlstm_kernel, cfg),
        grid=(b, t // tc),
        in_specs=[row, row, row, pl.BlockSpec((1, tc, LANES), lambda bi, i: (bi, i, 0)),
                  per_b((8, d_c)), per_b((d_c, d_c)), per_b((1, d_c)), per_b((1, LANES)),
                  const((CONV_W, d_c)), const((1, d_c)), const((d_c, d_c)), const((d_c, d_c)),
                  const((1, LANES)), const((1, d_c))],
        out_specs=[row, per_b((d_c, d_c)), per_b((1, d_c)), per_b((1, LANES))],
        out_shape=[jax.ShapeDtypeStruct((b, t, d_c), F32),
                   jax.ShapeDtypeStruct((b, d_c, d_c), F32),
                   jax.ShapeDtypeStruct((b, 1, d_c), F32),
                   jax.ShapeDtypeStruct((b, 1, LANES), F32)],
        scratch_shapes=[pltpu.VMEM((tcp + 8, d_c), F32),
                        pltpu.VMEM((tcp, d_c), F32), pltpu.VMEM((tcp, d_c), F32), pltpu.VMEM((tcp, d_c), F32),
                        pltpu.VMEM((tcp, LANES), F32), pltpu.VMEM((tcp, d_c), F32),
                        pltpu.VMEM((d_c, d_c), F32), pltpu.VMEM((1, d_c), F32), pltpu.VMEM((1, LANES), F32),
                        pltpu.VMEM((d_c, d_c), F32)],
        compiler_params=_cparams(("arbitrary", "arbitrary")),
        name="mlstm",
    )(u, v, og, misc, conv_in, ct0, n0, m0, conv_w, conv_b, wq_bd, wk_bd, fb_row, norm_g)


def _page_copies(pool_ref, layer, pt_ref, b, row0, buf_ref, slot, sems):
    n_pages, page = pt_ref.shape[1], pool_ref.shape[3]
    nrows = buf_ref.shape[1]
    return [pltpu.make_async_copy(pool_ref.at[layer, pt_ref[b, p], pl.ds(row0, nrows), :],
                                  buf_ref.at[slot, :, pl.ds(p * page, page)], sems.at[slot, p])
            for p in range(n_pages)]


def _prefetch_pages(streams, pt_ref, layer):
    b = pl.program_id(0)
    slot = b % 2

    def start(seq, to_slot):
        for pool_ref, row0, buf_ref, sems, _ in streams:
            for cp in _page_copies(pool_ref, layer, pt_ref, seq, row0, buf_ref, to_slot, sems):
                cp.start()

    @pl.when(b == 0)
    def _():
        start(0, 0)

    @pl.when(b + 1 < pl.num_programs(0))
    def _():
        start(b + 1, 1 - slot)

    for pool_ref, row0, buf_ref, sems, new_ref in streams:
        n_tok = pt_ref.shape[1] * pool_ref.shape[3]
        buf_ref[slot, :, n_tok:] = new_ref[0]
        for cp in _page_copies(pool_ref, layer, pt_ref, b, row0, buf_ref, slot, sems):
            cp.wait()
    return slot


def _stack_qpos(qpos, n):
    return jnp.concatenate([qpos] * n, axis=0)


def _flash_rows(s, keep, v_t, m_scr, l_scr, acc_scr):
    ck = s.shape[1]
    if keep is not None:
        s = jnp.where(keep, s, -MASK_BIG)
    m_old = m_scr[...]
    m_new = jnp.maximum(m_old, jnp.max(s, axis=-1, keepdims=True))
    p = jnp.exp(s - _lanes(m_new, ck))
    alpha = jnp.exp(m_old - m_new)
    l_scr[...] = alpha * l_scr[...] + jnp.sum(p, axis=-1, keepdims=True)
    m_scr[...] = m_new
    acc_scr[...] = alpha * acc_scr[...] + _dot_nt(p.astype(BF16), v_t)


def _dec_battn_kernel(cfg, pt_ref, q_ref, pool_ref, knew_ref, vnew_ref, bm_ref, g_ref, o_ref,
                      qa_scr, m_scr, l_scr, acc_scr, k_buf, v_buf, sems):
    tq, ck, nbp, mg, lp = cfg["tq"], cfg["ck"], cfg["nbp"], cfg["mask_group"], cfg["lp"]
    slot = _prefetch_pages([(pool_ref, cfg["k_col"] * LANES, k_buf, sems.at[0], knew_ref),
                            (pool_ref, cfg["v_col"] * LANES, v_buf, sems.at[1], vnew_ref)], pt_ref, cfg["layer"])
    qbase = cfg["q0"]
    for h in range(N_HEADS):
        g = h // mg
        qa_scr[h * tq:(h + 1) * tq, 0:LANES] = _qz_block(q_ref, h, SCALE)
        qa_scr[h * tq:(h + 1) * tq, LANES:LANES + nbp] = bm_ref[0, :, g * nbp:(g + 1) * nbp] - 1
    _flash_init(m_scr, l_scr, acc_scr)
    qpos = _stack_qpos(qbase + lax.broadcasted_iota(I32, (tq, 1), 0), N_HEADS)
    last = min((qbase + tq - 1) // ck, lp // ck - 1)

    for c in range(last + 1):
        k0 = c * ck
        kpos = k0 + lax.broadcasted_iota(I32, (1, ck), 1)
        bias = jnp.where(lax.broadcasted_iota(I32, (nbp, ck), 0) == lax.shift_right_logical(kpos, cfg["bshift"]),
                         MASK_BIG, 0.0).astype(BF16)
        k_aug = jnp.concatenate([k_buf[slot, :, k0:k0 + ck].astype(BF16), bias], axis=0)
        keep = (kpos <= qpos) if c == last else None
        _flash_rows(_dot(qa_scr[...], k_aug), keep, v_buf[slot, :, k0:k0 + ck].astype(BF16), m_scr, l_scr, acc_scr)
    heads = _flash_heads(l_scr, acc_scr, tq)
    if cfg["gate_col"] is not None:
        heads = [o * _gate(g_ref, cfg["gate_col"] + h) for h, o in enumerate(heads)]
    o_ref[0] = _assemble_heads(heads)


def dec_block_attention(q, pool_t, k_col, v_col, gates, bmask, new_t, *, layer, page_table, ck, q0, nbp, bshift,
                        mask_group, gate_col):
    b, tq, dq = q.shape
    n_pages, page = page_table.shape[1], pool_t.shape[3]
    lp = (n_pages + 1) * page
    assert lp % ck == 0 and (q0 + tq - 1) // ck == q0 // ck
    cfg = dict(tq=tq, ck=ck, q0=q0, nbp=nbp, bshift=bshift, mask_group=mask_group, gate_col=gate_col,
               lp=lp, layer=layer, k_col=k_col, v_col=v_col)
    rows = N_HEADS * tq
    im = lambda bi, pt: (bi, 0, 0)
    return pl.pallas_call(
        functools.partial(_dec_battn_kernel, cfg),
        grid_spec=pltpu.PrefetchScalarGridSpec(
            num_scalar_prefetch=1, grid=(b,),
            in_specs=[pl.BlockSpec((1, tq, dq), im), pl.BlockSpec(memory_space=pl.ANY),
                      pl.BlockSpec((1, LANES, page), lambda bi, pt: (bi, k_col, 0)),
                      pl.BlockSpec((1, LANES, page), lambda bi, pt: (bi, v_col, 0)),
                      pl.BlockSpec((1, tq, bmask.shape[2]), im), pl.BlockSpec((1, tq, LANES), im)],
            out_specs=pl.BlockSpec((1, tq, dq), im),
            scratch_shapes=[pltpu.VMEM((rows, LANES + nbp), BF16), pltpu.VMEM((rows, LANES), F32),
                            pltpu.VMEM((rows, LANES), F32), pltpu.VMEM((rows, LANES), F32),
                            pltpu.VMEM((2, LANES, lp), F32), pltpu.VMEM((2, LANES, lp), F32),
                            pltpu.SemaphoreType.DMA((2, 2, n_pages))]),
        out_shape=jax.ShapeDtypeStruct((b, tq, dq), F32),
        compiler_params=_cparams(("arbitrary",)), name="dec_attn_block",
    )(page_table, q, pool_t, new_t, new_t, bmask, gates)


def _dec_compress_kernel(cfg, pt_ref, pool_ref, knew_ref, vnew_ref, pe_ref, w_ref, o_ref,
                         a_scr, b_scr, x_scr, k_buf, v_buf, sems):
    n_groups, lp = cfg["n_groups"], cfg["lp"]
    slot = _prefetch_pages([(pool_ref, 0, k_buf, sems.at[0], knew_ref),
                            (pool_ref, LANES, v_buf, sems.at[1], vnew_ref)], pt_ref, cfg["layer"])
    o_ref[0] = jnp.zeros(o_ref.shape[1:], F32)
    for kind, buf in enumerate((k_buf, v_buf)):
        for p in range(lp // LANES):
            x_scr[p * LANES:(p + 1) * LANES, :] = buf[slot, :, p * LANES:(p + 1) * LANES].T
        acc_a = jnp.zeros((n_groups, LANES), F32)
        acc_b = jnp.zeros((n_groups, LANES), F32)
        for l in range(CMP_STRIDE):
            x = x_scr[pl.ds(l, n_groups, stride=CMP_STRIDE), :]
            acc_a += _dot((x + pe_ref[kind, l:l + 1, :]).astype(BF16), w_ref[kind, l])
            acc_b += _dot((x + pe_ref[kind, CMP_STRIDE + l:CMP_STRIDE + l + 1, :]).astype(BF16),
                          w_ref[kind, CMP_STRIDE + l])
        a_scr[...] = acc_a
        b_scr[0:n_groups, :] = acc_b
        b_scr[n_groups:n_groups + 8, :] = jnp.zeros((8, LANES), F32)
        o_ref[0, 0:n_groups, kind * LANES:(kind + 1) * LANES] = a_scr[...] + b_scr[pl.ds(1, n_groups), :]


def dec_compress(pool_t, pe2, w_bd, ncp, new_t, *, layer, page_table):
    b = new_t.shape[0]
    n_pages, page = page_table.shape[1], pool_t.shape[3]
    lp = (n_pages + 1) * page
    n_groups = lp // CMP_STRIDE
    cfg = dict(n_groups=n_groups, lp=lp, layer=layer)
    return pl.pallas_call(
        functools.partial(_dec_compress_kernel, cfg),
        grid_spec=pltpu.PrefetchScalarGridSpec(
            num_scalar_prefetch=1, grid=(b,),
            in_specs=[pl.BlockSpec(memory_space=pl.ANY),
                      pl.BlockSpec((1, LANES, page), lambda bi, pt: (bi, 0, 0)),
                      pl.BlockSpec((1, LANES, page), lambda bi, pt: (bi, 1, 0)),
                      pl.BlockSpec((2, CMP_LEN, LANES), lambda bi, pt: (0, 0, 0)),
                      pl.BlockSpec((2, CMP_LEN, LANES, LANES), lambda bi, pt: (0, 0, 0, 0))],
            out_specs=pl.BlockSpec((1, ncp, 2 * LANES), lambda bi, pt: (bi, 0, 0)),
            scratch_shapes=[pltpu.VMEM((n_groups, LANES), F32), pltpu.VMEM((n_groups + 8, LANES), F32),
                            pltpu.VMEM((lp, LANES), F32),
                            pltpu.VMEM((2, LANES, lp), F32), pltpu.VMEM((2, LANES, lp), F32),
                            pltpu.SemaphoreType.DMA((2, 2, n_pages))]),
        out_shape=jax.ShapeDtypeStruct((b, ncp, 2 * LANES), F32),
        compiler_params=_cparams(("arbitrary",)), name="dec_compress",
    )(page_table, pool_t, new_t, new_t, pe2, w_bd)


def _dec_dsa_kernel(cfg, pt_ref, q_ref, iq_ref, g_ref, ipool_ref, inew_ref, pool_ref, knew_ref, vnew_ref, o_ref,
                    qz_scr, iq_scr, key_scr, low_scr, m_scr, l_scr, acc_scr, i_buf, k_buf, v_buf, sems):
    tq, ck, n_top, lp = cfg["tq"], cfg["ck"], cfg["n_top"], cfg["lp"]
    slot = _prefetch_pages([(ipool_ref, 0, i_buf, sems.at[0], inew_ref),
                            (pool_ref, 0, k_buf, sems.at[1], knew_ref),
                            (pool_ref, LANES, v_buf, sems.at[2], vnew_ref)], pt_ref, cfg["layer"])
    qbase = cfg["q0"]
    qpos = qbase + lax.broadcasted_iota(I32, (tq, 1), 0)
    n_chunks = min((qbase + tq - 1) // ck + 1, lp // ck)
    for h in range(N_HEADS):
        qz_scr[h * tq:(h + 1) * tq, :] = _qz_block(q_ref, h, SCALE)
        pair = iq_ref[0, :, (h // 2) * LANES:(h // 2 + 1) * LANES]
        if h % 2:
            pair = pltpu.roll(pair, DH, 1)
        iq_scr[h * tq:(h + 1) * tq, :] = pair[:, :DH].astype(BF16)
    low_scr[...] = (lax.broadcasted_iota(I32, (ck, ck), 0)
                    < lax.broadcasted_iota(I32, (ck, ck), 1)).astype(BF16)
    iw = [jnp.broadcast_to(g_ref[0, :, h:h + 1], (tq, LANES)) for h in range(N_HEADS)]

    for c in range(n_chunks):
        k0 = c * ck
        s_all = jnp.maximum(_dot(iq_scr[...], i_buf[slot, :, k0:k0 + ck].astype(BF16)), 0.0)
        sc = s_all[0:tq, :] * _lanes(iw[0], ck)
        for h in range(1, N_HEADS):
            sc = sc + s_all[h * tq:(h + 1) * tq, :] * _lanes(iw[h], ck)
        kpos = k0 + lax.broadcasted_iota(I32, (1, ck), 1)
        key_scr[c] = _sort_key(jnp.where(kpos <= qpos, sc, NEG))

    def count(pred, thr):
        thr_w = _lanes(thr, ck)

        def body(c, a):
            hit = pred(key_scr[c], thr_w).astype(F32)
            for j in range(ck // LANES):
                a = a + hit[:, j * LANES:(j + 1) * LANES]
            return a
        a = lax.fori_loop(0, n_chunks, body, jnp.zeros((tq, LANES), F32))
        return jnp.sum(a, axis=-1, keepdims=True)

    def digit_step(it, t):
        shift = 30 - 2 * it
        cands = [_lanes(t + lax.shift_left(jnp.int32(c), shift), ck) for c in (1, 2, 3)]

        def body(c, accs):
            kk = key_scr[c]
            out = []
            for cand, a in zip(cands, accs):
                hit = (kk >= cand).astype(F32)
                for j in range(ck // LANES):
                    a = a + hit[:, j * LANES:(j + 1) * LANES]
                out.append(a)
            return tuple(out)

        accs = lax.fori_loop(0, n_chunks, body, (jnp.zeros((tq, LANES), F32),) * 3)
        digit = jnp.zeros((tq, LANES), I32)
        for a in accs:
            digit = digit + (jnp.sum(a, axis=-1, keepdims=True) >= n_top).astype(I32)
        return t + lax.shift_left(digit, shift)

    t = lax.fori_loop(0, 16, digit_step, jnp.full((tq, LANES), INT_MIN, I32))
    need = n_top - count(lambda kk, th: kk > th, t)
    t_w = _lanes(t, ck)
    _flash_init(m_scr, l_scr, acc_scr)

    before = jnp.zeros((tq, 1), F32)
    for c in range(n_chunks):
        k0 = c * ck
        kk = key_scr[c]
        eq = kk == t_w
        rank_eq = before + _dot(eq.astype(BF16), low_scr[...])
        mask = ((kk > t_w) | (eq & (rank_eq < need))) & (kk > _HALF_NEG_KEY)
        keep = jnp.concatenate([mask.astype(F32)] * N_HEADS, axis=0) > 0.5
        s = _dot(qz_scr[...], k_buf[slot, :, k0:k0 + ck].astype(BF16))
        _flash_rows(s, keep, v_buf[slot, :, k0:k0 + ck].astype(BF16), m_scr, l_scr, acc_scr)
        before = before + jnp.sum(eq.astype(F32), axis=-1, keepdims=True)
    o_ref[0] = _assemble_heads(_flash_heads(l_scr, acc_scr, tq))


def dec_dsa_attention(q, iq, misc, ipool_t, pool_t, inew_t, new_t, *, layer, page_table, ck, q0, n_top):
    b, tq, dq = q.shape
    n_pages, page = page_table.shape[1], pool_t.shape[3]
    lp = (n_pages + 1) * page
    cfg = dict(tq=tq, ck=ck, q0=q0, n_top=n_top, lp=lp, layer=layer)
    rows = N_HEADS * tq
    im = lambda bi, pt: (bi, 0, 0)
    qspec = pl.BlockSpec((1, tq, dq), im)
    return pl.pallas_call(
        functools.partial(_dec_dsa_kernel, cfg),
        grid_spec=pltpu.PrefetchScalarGridSpec(
            num_scalar_prefetch=1, grid=(b,),
            in_specs=[qspec, qspec, pl.BlockSpec((1, tq, LANES), im),
                      pl.BlockSpec(memory_space=pl.ANY), pl.BlockSpec((1, DH, page), im),
                      pl.BlockSpec(memory_space=pl.ANY),
                      pl.BlockSpec((1, LANES, page), lambda bi, pt: (bi, 0, 0)),
                      pl.BlockSpec((1, LANES, page), lambda bi, pt: (bi, 1, 0))],
            out_specs=qspec,
            scratch_shapes=[pltpu.VMEM((rows, LANES), BF16), pltpu.VMEM((rows, DH), BF16),
                            pltpu.VMEM((lp // ck, tq, ck), I32), pltpu.VMEM((ck, ck), BF16),
                            pltpu.VMEM((rows, LANES), F32), pltpu.VMEM((rows, LANES), F32),
                            pltpu.VMEM((rows, LANES), F32),
                            pltpu.VMEM((2, DH, lp), F32), pltpu.VMEM((2, LANES, lp), F32),
                            pltpu.VMEM((2, LANES, lp), F32), pltpu.SemaphoreType.DMA((3, 2, n_pages))]),
        out_shape=jax.ShapeDtypeStruct((b, tq, dq), F32),
        compiler_params=_cparams(("arbitrary",)), name="dec_dsa",
    )(page_table, q, iq, misc, ipool_t, inew_t, pool_t, new_t, new_t)


def _dec_moba_select_kernel(cfg, pt_ref, q_ref, pool_ref, knew_ref, bm_ref, k_buf, sems):
    tq, nblk, nbp, n_top, lp = cfg["tq"], cfg["nblk"], cfg["nbp"], cfg["n_top"], cfg["lp"]
    slot = _prefetch_pages([(pool_ref, 0, k_buf, sems, knew_ref)], pt_ref, cfg["layer"])
    qpos = cfg["q0"] + lax.broadcasted_iota(I32, (tq, 1), 0)
    lane = lax.broadcasted_iota(I32, (1, nbp), 1)
    km_t = jnp.zeros((LANES, nbp), F32)
    for j in range(nblk):
        r1 = min((j + 1) * MOBA_BLOCK, lp)
        col = jnp.sum(k_buf[slot, :, j * MOBA_BLOCK:r1], axis=1, keepdims=True) * (1.0 / MOBA_BLOCK)
        km_t = jnp.where(lane == j, col, km_t)
    own = _stack_qpos(qpos // MOBA_BLOCK, N_HEADS)
    s = _dot(jnp.concatenate([_qz_block(q_ref, h, 1.0) for h in range(N_HEADS)], axis=0), km_t.astype(BF16))
    s = jnp.where(lane < own, s, NEG)
    sel = (lane == own)
    if n_top > 0:
        sel = sel | (_top_few_mask(s, n_top) & (s > NEG / 2))
    sel = sel.astype(BF16)
    for h in range(N_HEADS):
        bm_ref[0, :, h * nbp:(h + 1) * nbp] = sel[h * tq:(h + 1) * tq, :]


def dec_moba_select(q, pool_t, new_t, *, layer, page_table, q0, nblk, nbp):
    b, tq, dq = q.shape
    n_pages, page = page_table.shape[1], pool_t.shape[3]
    lp = (n_pages + 1) * page
    cfg = dict(tq=tq, q0=q0, nblk=nblk, nbp=nbp, n_top=min(MOBA_TOPK, nblk - 1), lp=lp, layer=layer)
    im = lambda bi, pt: (bi, 0, 0)
    return pl.pallas_call(
        functools.partial(_dec_moba_select_kernel, cfg),
        grid_spec=pltpu.PrefetchScalarGridSpec(
            num_scalar_prefetch=1, grid=(b,),
            in_specs=[pl.BlockSpec((1, tq, dq), im), pl.BlockSpec(memory_space=pl.ANY),
                      pl.BlockSpec((1, LANES, page), im)],
            out_specs=pl.BlockSpec((1, tq, N_HEADS * nbp), im),
            scratch_shapes=[pltpu.VMEM((2, LANES, lp), F32), pltpu.SemaphoreType.DMA((2, n_pages))]),
        out_shape=jax.ShapeDtypeStruct((b, tq, N_HEADS * nbp), BF16),
        compiler_params=_cparams(("arbitrary",)), name="dec_moba_select",
    )(page_table, q, pool_t, new_t)


def _dec_win_kernel(cfg, q_ref, kv_ref, g_ref, o_ref):
    tq, koff = cfg["tq"], cfg["koff"]
    lw = kv_ref.shape[2]
    kpos = koff + lax.broadcasted_iota(I32, (1, lw), 1)
    qpos = _stack_qpos(cfg["q0"] + lax.broadcasted_iota(I32, (tq, 1), 0), N_HEADS)
    mask = (kpos <= qpos) & (qpos - kpos < WINDOW) & (kpos >= 0)
    qz = jnp.concatenate([_qz_block(q_ref, h, SCALE) for h in range(N_HEADS)], axis=0)
    s = jnp.where(mask, _dot(qz, kv_ref[0, 0:LANES, :].astype(BF16)), NEG)
    e = jnp.where(mask, jnp.exp(s - jnp.max(s, axis=-1, keepdims=True)), 0.0)
    l = jnp.maximum(jnp.sum(e, axis=-1, keepdims=True), TINY)
    o_all = _dot_nt(e.astype(BF16), kv_ref[0, LANES:2 * LANES, :].astype(BF16)) / l
    o_ref[0] = _assemble_heads([o_all[h * tq:(h + 1) * tq, :] * _gate(g_ref, cfg["gate_col"] + h)
                                for h in range(N_HEADS)])


def dec_window_attention(q, kv_t, gates, *, q0, koff, gate_col):
    b, tq, dq = q.shape
    lw = kv_t.shape[2]
    cfg = dict(tq=tq, q0=q0, koff=koff, gate_col=gate_col)
    return pl.pallas_call(
        functools.partial(_dec_win_kernel, cfg),
        grid=(b,),
        in_specs=[pl.BlockSpec((1, tq, dq), lambda bi: (bi, 0, 0)),
                  pl.BlockSpec((1, 2 * LANES, lw), lambda bi: (bi, 0, 0)),
                  pl.BlockSpec((1, tq, LANES), lambda bi: (bi, 0, 0))],
        out_specs=pl.BlockSpec((1, tq, dq), lambda bi: (bi, 0, 0)),
        out_shape=jax.ShapeDtypeStruct((b, tq, dq), F32),
        compiler_params=_cparams(("arbitrary",)), name="dec_attn_window",
    )(q, kv_t, gates)


def _regroup_columns(w, b, pieces):
    n_src = w.shape[-1]
    idx = []
    for s, wd, wp in pieces:
        idx += list(range(s, s + wd)) + [n_src] * (wp - wd)
    idx = np.asarray(idx, np.int32)
    w_ext = jnp.concatenate([w, jnp.zeros(w.shape[:-1] + (1,), w.dtype)], axis=-1)
    b_ext = jnp.concatenate([b, jnp.zeros(b.shape[:-1] + (1,), b.dtype)], axis=-1)
    return jnp.take(w_ext, idx, axis=-1).astype(BF16), jnp.take(b_ext, idx, axis=-1)


def _block_diag(w):
    h, a, b = w.shape[-3:]
    eye = jnp.eye(h, dtype=w.dtype)
    out = w[..., :, :, None, :] * eye[:, None, :, None]
    return out.reshape(w.shape[:-3] + (h * a, h * b))


_EVEN_SRC = [(0, 512, 512), (512, 512, 512), (1024, 256, 256), (1304, 512, 512), (1816, 256, 256),
             (2072, 512, 512), (2584, 64, 128), (2648, 8, 8), (1280, 24, 120)]
_EVEN_GROUPS = [(0, 512, 512), (512, 512, 512), (1024, 256, 256), (1280, 512, 512), (1792, 256, 256),
                (2048, 512, 512), (2560, 128, 64), (2688, 128, 128)]
_ODD_SRC = [(0, 512, 512), (512, 512, 512), (1024, 512, 512), (1552, 512, 512), (2064, 256, 256),
            (1536, 16, 128)]
_ODD_GROUPS = [(0, 512, 512), (512, 512, 512), (1024, 512, 512), (1536, 512, 512), (2048, 256, 256),
               (2304, 128, 128)]
_GATE_COL = N_HEADS


def _pick_chunk(n, limit):
    return max(c for c in range(LANES, limit + 1, LANES) if n % c == 0)


def _pad_rows(a, rows):
    return jnp.pad(a, ((0, 0), (0, rows - a.shape[1])) + ((0, 0),) * (a.ndim - 2))


def _new_cols(a, t_real, width):
    keep = (jnp.arange(a.shape[1]) < t_real)[None, :, None]
    return jnp.swapaxes(_pad_rows(jnp.where(keep, a, 0.0), width), 1, 2)


def _even_mixer(p, i, outs, bsz, t, t_real, past, page_table, q0):
    a_q, kv4, win, b_q, b_kv, b_iq, b_ik, misc = [o.reshape(bsz, t, -1) for o in outs]
    lk = t if past is None else q0 + t_real
    n16 = -(-lk // CMP_STRIDE)
    ns = -(-lk // SEL_BLOCK)
    nsp = -(-ns // LANES) * LANES
    gates = dict(cmp=_GATE_COL, sel=_GATE_COL + N_HEADS, win=_GATE_COL + 2 * N_HEADS)
    if past is None:
        tq, ck = 128, 512
        cmp_kv = nsa_compress(kv4, p["pe2"][i], p["cmp_w"][i], -(-(t // CMP_STRIDE) // LANES) * LANES)
        o_cmp, bmask = nsa_select(a_q, cmp_kv, misc, tq=tq, q0=q0, nc=n16 - 1, ns=ns, nsp=nsp, gate_col=gates["cmp"])
        o_sel = block_attention(a_q, kv4, 2, 3, misc, bmask, tq=tq, ck=ck, q0=q0, nbp=nsp, bshift=6,
                                mask_group=GROUP, gate_col=gates["sel"])
        o_win = window_attention(a_q, win, misc, tq=tq, ckw=WINDOW + tq, q0=q0, koff=0, gate_col=gates["win"])
        o_dsa = dsa_attention(b_q, b_iq, misc, b_ik, b_kv, tq=tq, ck=ck, q0=q0, n_top=min(DSA_TOPK, lk // 4))
        win_state = win[:, -min(WINDOW, t):].reshape(bsz, -1, 2, HKV, DH)
    else:
        page = past["nsa"].shape[3]
        pg = dict(layer=i, page_table=page_table)
        lp = (page_table.shape[1] + 1) * page
        ck = _pick_chunk(lp, 640)
        nsa_new = _new_cols(kv4, t_real, page)
        cmp_kv = dec_compress(past["nsa"], p["pe2"][i], p["cmp_w"][i], -(-(lp // CMP_STRIDE) // LANES) * LANES,
                              nsa_new, **pg)
        o_cmp, bmask = nsa_select(a_q, cmp_kv, misc, tq=t, q0=q0, nc=n16 - 1, ns=ns, nsp=nsp, gate_col=gates["cmp"])
        o_sel = dec_block_attention(a_q, past["nsa"], 2, 3, misc, bmask, nsa_new, ck=ck, q0=q0, nbp=nsp, bshift=6,
                                    mask_group=GROUP, gate_col=gates["sel"], **pg)
        win_buf = past["win"][i]
        n_buf = win_buf.shape[2]
        win_t = jnp.concatenate([win_buf, _new_cols(win, t_real, -(-t_real // LANES) * LANES)], axis=2)
        o_win = dec_window_attention(a_q, win_t, misc, q0=q0, koff=q0 - n_buf, gate_col=gates["win"])
        o_dsa = dec_dsa_attention(b_q, b_iq, misc, past["idx"], past["dsa"], _new_cols(b_ik, t_real, page),
                                  _new_cols(b_kv, t_real, page), ck=ck, q0=q0, n_top=min(DSA_TOPK, lk // 4), **pg)
        win_state = jnp.swapaxes(win_t[:, :, t_real:t_real + n_buf], 1, 2).reshape(bsz, n_buf, 2, HKV, DH)
    state = (kv4[:, :t_real].reshape(bsz, t_real, 4, HKV, DH), win_state,
             b_kv[:, :t_real].reshape(bsz, t_real, 2, HKV, DH), b_ik[:, :t_real])
    return [o_cmp, o_sel, o_win], o_dsa, state


def _odd_mixer(p, i, outs, bsz, t, t_real, past, page_table, q0):
    u, c_v, c_o, d_q, d_kv, misc = [o.reshape(bsz, t, -1) for o in outs]
    d_c = N_HEADS * DH
    idx = np.arange(N_HEADS)
    lk = t if past is None else q0 + t_real
    nblk = -(-lk // MOBA_BLOCK)
    if past is None:
        ct0 = jnp.zeros((bsz, d_c, d_c), F32)
        n0 = jnp.zeros((bsz, 1, d_c), F32)
        m0 = jnp.zeros((bsz, 1, LANES), F32)
        conv_prev = jnp.zeros((bsz, CONV_W - 1, d_c), F32)
        tc = 512
        bmask = moba_select(d_q, d_kv, tq=128, q0=q0, nblk=nblk, nbp=LANES)
        o_d = block_attention(d_q, d_kv, 0, 1, misc, bmask, tq=128, ck=512, q0=q0, nbp=LANES, bshift=8,
                              mask_group=1, gate_col=None)
    else:
        page = past["moba"].shape[3]
        pg = dict(layer=i, page_table=page_table)
        c_t = jnp.swapaxes(past["c"][i], -1, -2)
        ct0 = jnp.zeros((bsz, N_HEADS, DH, N_HEADS, DH), F32).at[:, idx, :, idx, :].set(
            jnp.moveaxis(c_t, 1, 0)).reshape(bsz, d_c, d_c)
        n0 = past["n"][i].reshape(bsz, 1, d_c)
        m0 = jnp.pad(past["m"][i], ((0, 0), (0, LANES - N_HEADS)))[:, None, :]
        conv_prev = past["conv"][i]
        tc = t
        moba_new = _new_cols(d_kv, t_real, page)
        bmask = dec_moba_select(d_q, past["moba"], moba_new, q0=q0, nblk=nblk, nbp=LANES, **pg)
        o_d = dec_block_attention(d_q, past["moba"], 0, 1, misc, bmask, moba_new,
                                  ck=_pick_chunk((page_table.shape[1] + 1) * page, 640), q0=q0, nbp=LANES,
                                  bshift=8, mask_group=1, gate_col=None, **pg)
    conv_in = jnp.concatenate([jnp.zeros((bsz, 8 - (CONV_W - 1), d_c), F32), conv_prev], axis=1)
    hc, ct1, n1, m1 = mlstm(u, c_v, c_o, misc, conv_in, ct0, n0, m0, p["conv_w"][i], p["conv_b"][i],
                            p["wq_bd"][i], p["wk_bd"][i], p["fb_row"][i], p["norm_g"][i], tc=tc, t_valid=t_real)
    c1 = jnp.swapaxes(jnp.moveaxis(ct1.reshape(bsz, N_HEADS, DH, N_HEADS, DH)[:, idx, :, idx, :], 0, 1), -1, -2)
    conv_state = jnp.concatenate([conv_prev, u[:, :t_real]], axis=1)[:, -(CONV_W - 1):]
    state = (c1, n1.reshape(bsz, N_HEADS, DH), m1[:, 0, :N_HEADS], conv_state,
             d_kv[:, :t_real].reshape(bsz, t_real, 2, HKV, DH))
    return [hc], o_d, state


def _run_group(p, x, mod, t_real, past, page_table, q0, per_row):
    bsz, t, d = x.shape
    m = bsz * t
    tm = min(512, m)
    tmm = min(1024, m)
    tiles = max(t // tm, 1)
    tiles_mlp = max(t // tmm, 1)
    x2d = x.reshape(m, d)
    n_layers = p["mlp_w1"].shape[0]
    ev_states, od_states = [], []
    for l in range(n_layers):
        i = l // 2
        if per_row:
            mod_l = jnp.moveaxis(jnp.repeat(mod[l], t, axis=0), 1, 0)
        else:
            mod_l = mod[l]
        if l % 2 == 0:
            outs = k_in(x2d, mod_l, p["norm1_g"][l], p["ev_w"][i], p["ev_b"][i], _EVEN_GROUPS, tm, tiles, per_row)
            a_list, b_o, st = _even_mixer(p, i, outs, bsz, t, t_real, past, page_table, q0)
            ev_states.append(st)
            w_out = p["ev_w_out"][i]
        else:
            outs = k_in(x2d, mod_l, p["norm1_g"][l], p["od_w"][i], p["od_b"][i], _ODD_GROUPS, tm, tiles, per_row)
            a_list, b_o, st = _odd_mixer(p, i, outs, bsz, t, t_real, past, page_table, q0)
            od_states.append(st)
            w_out = p["od_w_out"][i]
        x2d = k_out([a.reshape(m, -1) for a in a_list], b_o.reshape(m, -1), x2d, mod_l, w_out, tm, tiles, per_row)
        x2d = k_mlp(x2d, mod_l, p["norm2_g"][l], p["final_g"], p["mlp_w1"][l], p["mlp_w2"][l], tmm,
                    min(1024, p["mlp_w1"].shape[2]), tiles_mlp, per_row, final=(l == n_layers - 1))
    stack = lambda states: tuple(jnp.stack(a) for a in zip(*states))
    return x2d.reshape(bsz, t, d)[:, :t_real], stack(ev_states), stack(od_states)


def kernel(x_prompt, x_sample, cache_nsa_kv, state_nsa_win, cache_dsa_kv, cache_dsa_idx, state_mlstm_c, state_mlstm_n, state_mlstm_m, state_mlstm_conv, cache_moba_kv, page_table, c_prompt, c_sample, ada_w, ada_b, norm1_g, norm2_g, ev_w_in, ev_b_in, ev_w_out, nsa_cmp_pe, nsa_cmp_w, od_w_in, od_b_in, od_w_out, ml_conv_w, ml_conv_b, ml_wq, ml_wk, ml_f_bias, ml_norm_g, mlp_w1, mlp_w2, final_g):
    n_even, n_odd = ev_w_in.shape[0], od_w_in.shape[0]
    d = x_prompt.shape[-1]
    bp, bs = x_prompt.shape[0], x_sample.shape[0]
    t_dec = x_sample.shape[1]
    t_pad = -(-t_dec // 8) * 8
    page = cache_nsa_kv.shape[2]

    ev_w, ev_b = _regroup_columns(ev_w_in, ev_b_in, _EVEN_SRC)
    od_w, od_b = _regroup_columns(od_w_in, od_b_in, _ODD_SRC)
    d_c = N_HEADS * DH
    p = dict(
        norm1_g=norm1_g, norm2_g=norm2_g, final_g=final_g,
        ev_w=ev_w, ev_b=ev_b, od_w=od_w, od_b=od_b,
        ev_w_out=ev_w_out.astype(BF16), od_w_out=od_w_out.astype(BF16),
        mlp_w1=mlp_w1.astype(BF16), mlp_w2=mlp_w2.astype(BF16),
        pe2=jnp.tile(nsa_cmp_pe, (1, 1, 1, HKV)),
        cmp_w=_block_diag(jnp.broadcast_to(nsa_cmp_w[:, :, :, None], nsa_cmp_w.shape[:3] + (HKV, DH, DH))).astype(BF16),
        conv_w=ml_conv_w, conv_b=ml_conv_b.reshape(n_odd, 1, d_c),
        wq_bd=_block_diag(ml_wq).astype(BF16), wk_bd=_block_diag(ml_wk).astype(BF16),
        fb_row=jnp.pad(ml_f_bias, ((0, 0), (N_HEADS, LANES - 2 * N_HEADS))).reshape(n_odd, 1, LANES),
        norm_g=ml_norm_g.reshape(n_odd, 1, d_c),
    )
    n_rows = -(-(bs + bp) // 8) * 8
    c_all = jnp.pad(jnp.concatenate([c_sample, c_prompt], axis=0), ((0, n_rows - bs - bp), (0, 0)))
    mod = ada_mod(c_all, ada_w.astype(BF16), ada_b).reshape(ada_w.shape[0], n_rows, 6, d)
    mod_s, mod_p = mod[:, :bs], mod[:, bs:bs + bp]

    y_p, ev_p, od_p = _run_group(p, x_prompt, mod_p, x_prompt.shape[1], None, None, 0, False)

    def feat_major(a):
        nd = a.ndim
        a = jnp.transpose(a, (0, 1) + tuple(range(3, nd)) + (2,))
        return a.reshape(a.shape[:2] + (-1, a.shape[-1]))
    past = dict(nsa=feat_major(cache_nsa_kv), win=feat_major(state_nsa_win), dsa=feat_major(cache_dsa_kv),
                idx=feat_major(cache_dsa_idx), moba=feat_major(cache_moba_kv),
                c=state_mlstm_c, n=state_mlstm_n, m=state_mlstm_m, conv=state_mlstm_conv)
    x_s = _pad_rows(x_sample, t_pad)
    y_s, ev_s, od_s = _run_group(p, x_s, mod_s, t_dec, past, page_table, page_table.shape[1] * page, True)

    nsa_kv_p, nsa_win_p, dsa_kv_p, dsa_idx_p = ev_p
    nsa_kv_s, nsa_win_s, dsa_kv_s, dsa_idx_s = ev_s
    ml_c_p, ml_n_p, ml_m_p, ml_conv_p, moba_kv_p = od_p
    ml_c_s, ml_n_s, ml_m_s, ml_conv_s, moba_kv_s = od_s
    return (y_p, y_s,
            nsa_kv_p, nsa_kv_s, nsa_win_p, nsa_win_s, dsa_kv_p, dsa_kv_s, dsa_idx_p, dsa_idx_s,
            ml_c_p, ml_c_s, ml_n_p, ml_n_s, ml_m_p, ml_m_s, ml_conv_p, ml_conv_s, moba_kv_p, moba_kv_s)
```

```python
import functools

import numpy as np
import jax
import jax.numpy as jnp
from jax import lax
from jax.experimental import pallas as pl
from jax.experimental.pallas import tpu as pltpu

F32 = jnp.float32
BF16 = jnp.bfloat16
I32 = jnp.int32

DH = 64
N_HEADS = 8
HKV = 2
GROUP = N_HEADS // HKV
CMP_LEN = 32
CMP_STRIDE = 16
SEL_BLOCK = 64
N_SEL = 16
WINDOW = 512
DSA_TOPK = 256
CONV_W = 4
CHUNK = 64
MOBA_BLOCK = 256
MOBA_TOPK = 3
EPS = 1e-6
NEG = -1e30
TINY = 1e-30
FORCE = 1e4
SCALE = DH ** -0.5
LOG2E = 1.4426950408889634
LANES = 128
INT_MIN = -2 ** 31

VMEM_LIMIT = 56 * 1024 * 1024


def _cparams(sem):
    return pltpu.CompilerParams(dimension_semantics=sem, vmem_limit_bytes=VMEM_LIMIT)


def _dot(a, b):
    return jnp.dot(a, b, preferred_element_type=F32)


def _dot_nt(a, b):
    return lax.dot_general(a, b, (((1,), (1,)), ((), ())), preferred_element_type=F32)


def _split3(x):
    hi = x.astype(BF16)
    r = x - hi.astype(F32)
    mid = r.astype(BF16)
    lo = (r - mid.astype(F32)).astype(BF16)
    return hi, mid, lo


def _ada_kernel(c_ref, w_ref, b_ref, o_ref):
    c = c_ref[...]
    cs = (c * jax.nn.sigmoid(c)).astype(BF16)
    o_ref[0] = _dot(cs, w_ref[0]) + b_ref[0]


def ada_mod(c_all, ada_w_bf, ada_b):
    n_layers, d, n = ada_w_bf.shape
    r = c_all.shape[0]
    tn = 1536
    return pl.pallas_call(
        _ada_kernel,
        grid=(n_layers, n // tn),
        in_specs=[pl.BlockSpec((r, d), lambda l, j: (0, 0)),
                  pl.BlockSpec((1, d, tn), lambda l, j: (l, 0, j)),
                  pl.BlockSpec((1, 1, tn), lambda l, j: (l, 0, j))],
        out_specs=pl.BlockSpec((1, r, tn), lambda l, j: (l, 0, j)),
        out_shape=jax.ShapeDtypeStruct((n_layers, r, n), F32),
        compiler_params=_cparams(("arbitrary", "arbitrary")),
        name="ada_mod",
    )(c_all, ada_w_bf, ada_b.reshape(n_layers, 1, n))


def _mod_chunk(mod_ref, k, per_row):
    return mod_ref[k] if per_row else mod_ref[0, k:k + 1, :]


def _mod_spec(per_row, tm, d, tiles_per_batch):
    if per_row:
        return pl.BlockSpec((6, tm, d), lambda i, *_: (0, i, 0))
    return pl.BlockSpec((1, 6, d), lambda i, *_: (i // tiles_per_batch, 0, 0))


def _norm_mod(x, g, shift, scale):
    y = x * lax.rsqrt(jnp.mean(x * x, axis=-1, keepdims=True) + EPS) * g
    return y * (1.0 + scale) + shift


def _kin_kernel(per_row, groups, x_ref, mod_ref, g_ref, w_ref, b_ref, *refs):
    outs, h_scr = refs[:-1], refs[-1]
    h_scr[...] = _norm_mod(x_ref[...], g_ref[...], _mod_chunk(mod_ref, 0, per_row),
                           _mod_chunk(mod_ref, 1, per_row)).astype(BF16)
    for (c0, wpad, wout), o_ref in zip(groups, outs):
        z = _dot(h_scr[...], w_ref[:, c0:c0 + wpad]) + b_ref[:, c0:c0 + wpad]
        o_ref[...] = z[:, :wout]


def k_in(x2d, mod, g, w_bf, b, groups, tm, tiles_per_batch, per_row):
    m, d = x2d.shape
    wp = w_bf.shape[1]
    return pl.pallas_call(
        functools.partial(_kin_kernel, per_row, groups),
        grid=(m // tm,),
        in_specs=[pl.BlockSpec((tm, d), lambda i: (i, 0)),
                  _mod_spec(per_row, tm, d, tiles_per_batch),
                  pl.BlockSpec((1, d), lambda i: (0, 0)),
                  pl.BlockSpec((d, wp), lambda i: (0, 0)),
                  pl.BlockSpec((1, wp), lambda i: (0, 0))],
        out_specs=[pl.BlockSpec((tm, wout), lambda i: (i, 0)) for _, _, wout in groups],
        out_shape=[jax.ShapeDtypeStruct((m, wout), F32) for _, _, wout in groups],
        scratch_shapes=[pltpu.VMEM((tm, d), BF16)],
        compiler_params=_cparams(("arbitrary",)),
        name="k_in",
    )(x2d, mod, g.reshape(1, d), w_bf, b.reshape(1, wp))


def _kout_kernel(per_row, n_a, *refs):
    a_refs = refs[:n_a]
    b_ref, x_ref, mod_ref, w_ref, o_ref = refs[n_a:]
    a = a_refs[0][...]
    for r in a_refs[1:]:
        a = a + r[...]
    half = a.shape[1]
    y = _dot(a.astype(BF16), w_ref[:half, :]) + _dot(b_ref[...].astype(BF16), w_ref[half:, :])
    o_ref[...] = x_ref[...] + _mod_chunk(mod_ref, 2, per_row) * y


def k_out(a_list, b2d, x2d, mod, w_bf, tm, tiles_per_batch, per_row):
    m, d = x2d.shape
    half = b2d.shape[1]
    n_a = len(a_list)
    row_spec = pl.BlockSpec((tm, half), lambda i: (i, 0))
    return pl.pallas_call(
        functools.partial(_kout_kernel, per_row, n_a),
        grid=(m // tm,),
        in_specs=[row_spec] * (n_a + 1) + [
            pl.BlockSpec((tm, d), lambda i: (i, 0)),
            _mod_spec(per_row, tm, d, tiles_per_batch),
            pl.BlockSpec((2 * half, d), lambda i: (0, 0))],
        out_specs=pl.BlockSpec((tm, d), lambda i: (i, 0)),
        out_shape=jax.ShapeDtypeStruct((m, d), F32),
        compiler_params=_cparams(("arbitrary",)),
        name="k_out",
    )(*a_list, b2d, x2d, mod, w_bf)


def _mlp_kernel(per_row, final, x_ref, mod_ref, g_ref, fg_ref, w1_ref, w2_ref, o_ref, h_scr, acc_scr):
    f = pl.program_id(1)

    @pl.when(f == 0)
    def _():
        h_scr[...] = _norm_mod(x_ref[...], g_ref[...], _mod_chunk(mod_ref, 3, per_row),
                               _mod_chunk(mod_ref, 4, per_row)).astype(BF16)
        acc_scr[...] = jnp.zeros_like(acc_scr)

    a = jnp.maximum(_dot(h_scr[...], w1_ref[...]), 0.0)
    acc_scr[...] += _dot((a * a).astype(BF16), w2_ref[...])

    @pl.when(f == pl.num_programs(1) - 1)
    def _():
        xn = x_ref[...] + _mod_chunk(mod_ref, 5, per_row) * acc_scr[...]
        if final:
            xn = xn * lax.rsqrt(jnp.mean(xn * xn, axis=-1, keepdims=True) + EPS) * fg_ref[...]
        o_ref[...] = xn


def k_mlp(x2d, mod, g, final_g, w1_bf, w2_bf, tm, tf, tiles_per_batch, per_row, final):
    m, d = x2d.shape
    dff = w1_bf.shape[1]
    return pl.pallas_call(
        functools.partial(_mlp_kernel, per_row, final),
        grid=(m // tm, dff // tf),
        in_specs=[pl.BlockSpec((tm, d), lambda i, f: (i, 0)),
                  _mod_spec(per_row, tm, d, tiles_per_batch),
                  pl.BlockSpec((1, d), lambda i, f: (0, 0)),
                  pl.BlockSpec((1, d), lambda i, f: (0, 0)),
                  pl.BlockSpec((d, tf), lambda i, f: (0, f)),
                  pl.BlockSpec((tf, d), lambda i, f: (f, 0))],
        out_specs=pl.BlockSpec((tm, d), lambda i, f: (i, 0)),
        out_shape=jax.ShapeDtypeStruct((m, d), F32),
        scratch_shapes=[pltpu.VMEM((tm, d), BF16), pltpu.VMEM((tm, d), F32)],
        compiler_params=_cparams(("arbitrary", "arbitrary")),
        name="k_mlp",
    )(x2d, mod, g.reshape(1, d), final_g.reshape(1, d), w1_bf, w2_bf)


MASK_BIG = 2e30


def _qz_block(q_ref, h, scale):
    pair = q_ref[0, :, (h // 2) * LANES:(h // 2 + 1) * LANES]
    if (h % 2) != (h // GROUP):
        pair = pltpu.roll(pair, DH, 1)
    lane = lax.broadcasted_iota(I32, pair.shape, 1)
    keep = (lane < DH) if h // GROUP == 0 else (lane >= DH)
    return jnp.where(keep, pair * scale, 0.0).astype(BF16)


def _assemble_heads(o_list):
    lane = lax.broadcasted_iota(I32, o_list[0].shape, 1)
    pairs = []
    for p in range(N_HEADS // 2):
        a, b = o_list[2 * p], o_list[2 * p + 1]
        if (2 * p) // GROUP != 0:
            a = pltpu.roll(a, DH, 1)
        if (2 * p + 1) // GROUP != 1:
            b = pltpu.roll(b, DH, 1)
        pairs.append(jnp.where(lane < DH, a, b))
    return jnp.concatenate(pairs, axis=1)


def _gate(g_ref, col):
    return jax.nn.sigmoid(g_ref[0, :, col:col + 1])


def _lanes(x, n):
    return x if n == LANES else jnp.concatenate([x] * (n // LANES), axis=1)


def _flash_init(m_scr, l_scr, acc_scr):
    m_scr[...] = jnp.full(m_scr.shape, NEG, F32)
    l_scr[...] = jnp.zeros(l_scr.shape, F32)
    acc_scr[...] = jnp.zeros(acc_scr.shape, F32)


def _flash_chunk(s_all, keep, v_bf, tq, m_scr, l_scr, acc_scr):
    ck = s_all.shape[1]
    ps, alphas = [], []
    for h in range(N_HEADS):
        r0 = h * tq
        s = s_all[r0:r0 + tq, :]
        if keep is not None:
            s = jnp.where(keep, s, -MASK_BIG)
        m_old = m_scr[r0:r0 + tq, :]
        m_new = jnp.maximum(m_old, jnp.max(s, axis=-1, keepdims=True))
        p = jnp.exp2(s - _lanes(m_new, ck))
        alpha = jnp.exp2(m_old - m_new)
        l_scr[r0:r0 + tq, :] = alpha * l_scr[r0:r0 + tq, :] + jnp.sum(p, axis=-1, keepdims=True)
        m_scr[r0:r0 + tq, :] = m_new
        ps.append(p.astype(BF16))
        alphas.append(alpha)
    acc_scr[...] = jnp.concatenate(alphas, axis=0) * acc_scr[...] + _dot(jnp.concatenate(ps, axis=0), v_bf)


def _flash_heads(l_scr, acc_scr, tq):
    return [acc_scr[h * tq:(h + 1) * tq, :] / jnp.maximum(l_scr[h * tq:(h + 1) * tq, :], TINY)
            for h in range(N_HEADS)]


def _sort_key(x):
    key = pltpu.bitcast(x + 0.0, I32)
    return jnp.where(key < 0, key ^ jnp.int32(0x7FFFFFFF), key)


_HALF_NEG_KEY = int(np.float32(NEG / 2).view(np.int32) ^ 0x7FFFFFFF)


def _topk_mask(s, k):
    r, n = s.shape
    key = _sort_key(s)

    def step(it, t):
        shift = 30 - 2 * it
        digit = jnp.zeros((r, LANES), I32)
        for c in (1, 2, 3):
            cand = t + lax.shift_left(jnp.int32(c), shift)
            cnt = jnp.sum((key >= _lanes(cand, n)).astype(F32), axis=-1, keepdims=True)
            digit = digit + (cnt >= k).astype(I32)
        return t + lax.shift_left(digit, shift)

    t = _lanes(lax.fori_loop(0, 16, step, jnp.full((r, LANES), INT_MIN, I32)), n)
    gt = key > t
    eq = key == t
    need = k - jnp.sum(gt.astype(F32), axis=-1, keepdims=True)
    lower = (lax.broadcasted_iota(I32, (n, n), 0) < lax.broadcasted_iota(I32, (n, n), 1))
    before = _dot(eq.astype(BF16), lower.astype(BF16))
    return gt | (eq & (before < need))


def _rank_topk_mask(s, k, n_valid):
    nv = -(-n_valid // 8) * 8
    st = s.T
    rows = [st[8 * v:8 * v + 8, :] for v in range(nv // 8)]
    sub = lax.broadcasted_iota(I32, (8, 1), 0)
    rank = [jnp.zeros(rows[0].shape, F32) for _ in rows]
    for i in range(nv):
        cand = rows[i // 8][i % 8:i % 8 + 1, :]
        for v in range(nv // 8):
            ge = jnp.where(cand >= rows[v], 1.0, 0.0)
            gt = jnp.where(cand > rows[v], 1.0, 0.0)
            if v > i // 8:
                rank[v] = rank[v] + ge
            elif v < i // 8:
                rank[v] = rank[v] + gt
            else:
                rank[v] = rank[v] + jnp.where(sub > i % 8, ge, gt)
    sel_t = jnp.concatenate([jnp.where(r < k, 1.0, 0.0) for r in rank]
                            + [jnp.zeros((s.shape[1] - nv, s.shape[0]), F32)], axis=0)
    return sel_t.T > 0.5


def _top_few_mask(s, k):
    idx = lax.broadcasted_iota(I32, s.shape, 1).astype(F32)
    sel = jnp.zeros(s.shape, jnp.bool_)
    cur = s
    for _ in range(k):
        m = jnp.max(cur, axis=-1, keepdims=True)
        first = jnp.min(jnp.where(cur == m, idx, 3e38), axis=-1, keepdims=True)
        pick = idx == first
        sel = sel | pick
        cur = jnp.where(pick, -3e38, cur)
    return sel


def _battn_kernel(cfg, q_ref, k_ref, v_ref, bm_ref, g_ref, o_ref, qa_scr, m_scr, l_scr, acc_scr):
    tq, ck, nbp, mg = cfg["tq"], cfg["ck"], cfg["nbp"], cfg["mask_group"]
    lp = k_ref.shape[1]
    get_k = lambda k0: k_ref[0, pl.ds(k0, ck), :]
    get_v = lambda k0: v_ref[0, pl.ds(k0, ck), :]
    qbase = cfg["q0"] + pl.program_id(1) * tq
    for h in range(N_HEADS):
        g = h // mg
        qa_scr[h * tq:(h + 1) * tq, 0:LANES] = _qz_block(q_ref, h, SCALE * LOG2E)
        qa_scr[h * tq:(h + 1) * tq, LANES:LANES + nbp] = bm_ref[0, :, g * nbp:(g + 1) * nbp] - 1
    _flash_init(m_scr, l_scr, acc_scr)
    qpos = qbase + lax.broadcasted_iota(I32, (tq, 1), 0)
    last = jnp.minimum((qbase + tq - 1) // ck, lp // ck - 1)

    def chunk(c, causal):
        k0 = pl.multiple_of(c * ck, ck)
        blk = lax.shift_right_logical(k0 + lax.broadcasted_iota(I32, (ck, 1), 0), cfg["bshift"])
        bias = jnp.where(lax.broadcasted_iota(I32, (ck, nbp), 1) == blk, MASK_BIG, 0.0).astype(BF16)
        k_aug = jnp.concatenate([get_k(k0).astype(BF16), bias], axis=1)
        s_all = _dot_nt(qa_scr[...], k_aug)
        keep = ((k0 + lax.broadcasted_iota(I32, (1, ck), 1)) <= qpos) if causal else None
        _flash_chunk(s_all, keep, get_v(k0).astype(BF16), tq, m_scr, l_scr, acc_scr)

    def body(c, carry):
        chunk(c, False)
        return carry

    lax.fori_loop(0, last, body, 0)
    chunk(last, True)
    heads = _flash_heads(l_scr, acc_scr, tq)
    if cfg["gate_col"] is not None:
        heads = [o * _gate(g_ref, cfg["gate_col"] + h) for h, o in enumerate(heads)]
    o_ref[0] = _assemble_heads(heads)


def block_attention(q, kv, k_col, v_col, gates, bmask, *, tq, ck, q0, nbp, bshift, mask_group, gate_col):
    b, t, dq = q.shape
    lp = kv.shape[1]
    assert ck % tq == 0 and q0 % tq == 0 and lp % ck == 0
    cfg = dict(tq=tq, ck=ck, q0=q0, nbp=nbp, bshift=bshift, mask_group=mask_group, gate_col=gate_col)
    rows = N_HEADS * tq
    im = lambda bi, i: (bi, i, 0)
    return pl.pallas_call(
        functools.partial(_battn_kernel, cfg),
        grid=(b, t // tq),
        in_specs=[pl.BlockSpec((1, tq, dq), im),
                  pl.BlockSpec((1, lp, LANES), lambda bi, i: (bi, 0, k_col)),
                  pl.BlockSpec((1, lp, LANES), lambda bi, i: (bi, 0, v_col)),
                  pl.BlockSpec((1, tq, bmask.shape[2]), im), pl.BlockSpec((1, tq, LANES), im)],
        out_specs=pl.BlockSpec((1, tq, dq), im),
        out_shape=jax.ShapeDtypeStruct((b, t, dq), F32),
        scratch_shapes=[pltpu.VMEM((rows, LANES + nbp), BF16), pltpu.VMEM((rows, LANES), F32),
                        pltpu.VMEM((rows, LANES), F32), pltpu.VMEM((rows, LANES), F32)],
        compiler_params=_cparams(("arbitrary", "arbitrary")),
        name="attn_block",
    )(q, kv, kv, bmask, gates)


def _win_kernel(cfg, q_ref, k_ref, v_ref, g_ref, o_ref):
    tq, ckw, koff = cfg["tq"], cfg["ckw"], cfg["koff"]
    lp = k_ref.shape[1]
    qbase = cfg["q0"] + pl.program_id(1) * tq
    start = pl.multiple_of(jnp.clip(qbase - koff - WINDOW, 0, lp - ckw), 8)
    k_bf = k_ref[0, pl.ds(start, ckw), :].astype(BF16)
    v_bf = v_ref[0, pl.ds(start, ckw), :].astype(BF16)
    kpos = koff + start + lax.broadcasted_iota(I32, (1, ckw), 1)
    qpos = qbase + lax.broadcasted_iota(I32, (tq, 1), 0)
    mask = (kpos <= qpos) & (qpos - kpos < WINDOW) & (kpos >= 0)
    s_all = _dot_nt(jnp.concatenate([_qz_block(q_ref, h, SCALE) for h in range(N_HEADS)], axis=0), k_bf)
    es, ls = [], []
    for h in range(N_HEADS):
        s = jnp.where(mask, s_all[h * tq:(h + 1) * tq, :], NEG)
        e = jnp.where(mask, jnp.exp(s - jnp.max(s, axis=-1, keepdims=True)), 0.0)
        ls.append(jnp.maximum(jnp.sum(e, axis=-1, keepdims=True), TINY))
        es.append(e.astype(BF16))
    o_all = _dot(jnp.concatenate(es, axis=0), v_bf)
    heads = [o_all[h * tq:(h + 1) * tq, :] / ls[h] * _gate(g_ref, cfg["gate_col"] + h) for h in range(N_HEADS)]
    o_ref[0] = _assemble_heads(heads)


def window_attention(q, kv, gates, *, tq, ckw, q0, koff, gate_col):
    b, t, dq = q.shape
    lp = kv.shape[1]
    cfg = dict(tq=tq, ckw=ckw, q0=q0, koff=koff, gate_col=gate_col)
    return pl.pallas_call(
        functools.partial(_win_kernel, cfg),
        grid=(b, t // tq),
        in_specs=[pl.BlockSpec((1, tq, dq), lambda bi, i: (bi, i, 0)),
                  pl.BlockSpec((1, lp, LANES), lambda bi, i: (bi, 0, 0)),
                  pl.BlockSpec((1, lp, LANES), lambda bi, i: (bi, 0, 1)),
                  pl.BlockSpec((1, tq, LANES), lambda bi, i: (bi, i, 0))],
        out_specs=pl.BlockSpec((1, tq, dq), lambda bi, i: (bi, i, 0)),
        out_shape=jax.ShapeDtypeStruct((b, t, dq), F32),
        compiler_params=_cparams(("arbitrary", "arbitrary")),
        name="attn_window",
    )(q, kv, kv, gates)


def _compress_kernel(n_groups, x_ref, pe_ref, w_ref, o_ref, a_scr, b_scr):
    acc_a = jnp.zeros((n_groups, LANES), F32)
    acc_b = jnp.zeros((n_groups, LANES), F32)
    for l in range(CMP_STRIDE):
        x = x_ref[0, pl.ds(l, n_groups, stride=CMP_STRIDE), :]
        acc_a += _dot((x + pe_ref[0, l:l + 1, :]).astype(BF16), w_ref[0, l])
        acc_b += _dot((x + pe_ref[0, CMP_STRIDE + l:CMP_STRIDE + l + 1, :]).astype(BF16),
                      w_ref[0, CMP_STRIDE + l])
    a_scr[...] = acc_a
    b_scr[0:n_groups, :] = acc_b
    b_scr[n_groups:n_groups + 8, :] = jnp.zeros((8, LANES), F32)
    o_ref[0] = jnp.zeros(o_ref.shape[1:], F32)
    o_ref[0, 0:n_groups, :] = a_scr[...] + b_scr[pl.ds(1, n_groups), :]


def nsa_compress(kv, pe2, w_bd, ncp):
    b, lp, _ = kv.shape
    n_groups = lp // CMP_STRIDE
    return pl.pallas_call(
        functools.partial(_compress_kernel, n_groups),
        grid=(b, 2),
        in_specs=[pl.BlockSpec((1, lp, LANES), lambda bi, j: (bi, 0, j)),
                  pl.BlockSpec((1, CMP_LEN, LANES), lambda bi, j: (j, 0, 0)),
                  pl.BlockSpec((1, CMP_LEN, LANES, LANES), lambda bi, j: (j, 0, 0, 0))],
        out_specs=pl.BlockSpec((1, ncp, LANES), lambda bi, j: (bi, 0, j)),
        out_shape=jax.ShapeDtypeStruct((b, ncp, 2 * LANES), F32),
        scratch_shapes=[pltpu.VMEM((n_groups, LANES), F32), pltpu.VMEM((n_groups + 8, LANES), F32)],
        compiler_params=_cparams(("arbitrary", "arbitrary")),
        name="nsa_compress",
    )(kv, pe2, w_bd)


def _nsa_select_kernel(cfg, q_ref, c_ref, g_ref, o_ref, bm_ref):
    tq, nc, nsp = cfg["tq"], cfg["nc"], cfg["nsp"]
    ncp = c_ref.shape[1]
    qbase = cfg["q0"] + pl.program_id(1) * tq
    qpos = qbase + lax.broadcasted_iota(I32, (tq, 1), 0)
    kc = c_ref[0, :, 0:LANES].astype(BF16)
    vc = c_ref[0, :, LANES:2 * LANES].astype(BF16)
    n = lax.broadcasted_iota(I32, (1, ncp), 1)
    mask = (n * CMP_STRIDE + (CMP_LEN - 1) <= qpos) & (n < nc)
    s_all = _dot_nt(jnp.concatenate([_qz_block(q_ref, h, SCALE) for h in range(N_HEADS)], axis=0), kc)
    ps, psum = [], [None] * HKV
    for h in range(N_HEADS):
        s = jnp.where(mask, s_all[h * tq:(h + 1) * tq, :], NEG)
        e = jnp.where(mask, jnp.exp(s - jnp.max(s, axis=-1, keepdims=True)), 0.0)
        p = e / jnp.maximum(jnp.sum(e, axis=-1, keepdims=True), TINY)
        ps.append(p.astype(BF16))
        psum[h // GROUP] = p if psum[h // GROUP] is None else psum[h // GROUP] + p
    o_all = _dot(jnp.concatenate(ps, axis=0), vc)
    o_ref[0] = _assemble_heads([o_all[h * tq:(h + 1) * tq, :] * _gate(g_ref, cfg["gate_col"] + h)
                                for h in range(N_HEADS)])
    cs = lax.broadcasted_iota(I32, (ncp, nsp), 0) * CMP_STRIDE
    ss = lax.broadcasted_iota(I32, (ncp, nsp), 1) * SEL_BLOCK
    overlap = ((cs < ss + SEL_BLOCK) & (cs + CMP_LEN > ss)).astype(BF16)
    hi, mid, lo = _split3(jnp.concatenate(psum, axis=0))
    imp = _dot(hi, overlap) + _dot(mid, overlap) + _dot(lo, overlap)
    j = lax.broadcasted_iota(I32, (1, nsp), 1)
    blk = jnp.concatenate([qpos // SEL_BLOCK] * HKV, axis=0)
    forced = (j == 0) | (j == blk) | (j == blk - 1)
    score = jnp.where(j <= blk, imp + jnp.where(forced, FORCE, 0.0), NEG)
    if tq == LANES and nsp == LANES:
        top = jnp.concatenate([_rank_topk_mask(score[g * tq:(g + 1) * tq, :], N_SEL, cfg["ns"])
                               for g in range(HKV)], axis=0)
    else:
        top = _topk_mask(score, N_SEL)
    sel = jnp.where(top & (score > NEG / 2), 1.0, 0.0).astype(BF16)
    for g in range(HKV):
        bm_ref[0, :, g * nsp:(g + 1) * nsp] = sel[g * tq:(g + 1) * tq, :]


def nsa_select(q, cmp_kv, gates, *, tq, q0, nc, ns, nsp, gate_col):
    b, t, dq = q.shape
    ncp = cmp_kv.shape[1]
    cfg = dict(tq=tq, q0=q0, nc=nc, ns=ns, nsp=nsp, gate_col=gate_col)
    return pl.pallas_call(
        functools.partial(_nsa_select_kernel, cfg),
        grid=(b, t // tq),
        in_specs=[pl.BlockSpec((1, tq, dq), lambda bi, i: (bi, i, 0)),
                  pl.BlockSpec((1, ncp, 2 * LANES), lambda bi, i: (bi, 0, 0)),
                  pl.BlockSpec((1, tq, LANES), lambda bi, i: (bi, i, 0))],
        out_specs=[pl.BlockSpec((1, tq, dq), lambda bi, i: (bi, i, 0)),
                   pl.BlockSpec((1, tq, HKV * nsp), lambda bi, i: (bi, i, 0))],
        out_shape=[jax.ShapeDtypeStruct((b, t, dq), F32),
                   jax.ShapeDtypeStruct((b, t, HKV * nsp), BF16)],
        compiler_params=_cparams(("arbitrary", "arbitrary")),
        name="nsa_select",
    )(q, cmp_kv, gates)


def _dsa_kernel(cfg, q_ref, iq_ref, g_ref, ik_ref, k_ref, v_ref, o_ref,
                qz_scr, iq_scr, iw_scr, key_scr, low_scr, m_scr, l_scr, acc_scr):
    tq, ck, n_top = cfg["tq"], cfg["ck"], cfg["n_top"]
    lp = k_ref.shape[1]
    get_i = lambda k0: ik_ref[0, pl.ds(k0, ck), :]
    get_k = lambda k0: k_ref[0, pl.ds(k0, ck), :]
    get_v = lambda k0: v_ref[0, pl.ds(k0, ck), :]
    qbase = cfg["q0"] + pl.program_id(1) * tq
    qpos = qbase + lax.broadcasted_iota(I32, (tq, 1), 0)
    hi_chunk = jnp.minimum((qbase + tq - 1) // ck + 1, lp // ck)
    for h in range(N_HEADS):
        qz_scr[h * tq:(h + 1) * tq, :] = _qz_block(q_ref, h, SCALE * LOG2E)
        pair = iq_ref[0, :, (h // 2) * LANES:(h // 2 + 1) * LANES]
        if h % 2:
            pair = pltpu.roll(pair, DH, 1)
        iq_scr[h * tq:(h + 1) * tq, :] = pair[:, :DH].astype(BF16)
        iw_scr[h * tq:(h + 1) * tq, :] = jnp.broadcast_to(g_ref[0, :, h:h + 1], (tq, LANES))
    low_scr[...] = (lax.broadcasted_iota(I32, (ck, ck), 0)
                    < lax.broadcasted_iota(I32, (ck, ck), 1)).astype(BF16)

    def score_body(c, carry):
        k0 = pl.multiple_of(c * ck, ck)
        s_all = jnp.maximum(_dot_nt(iq_scr[...], get_i(k0).astype(BF16)), 0.0)
        sc = s_all[0:tq, :] * _lanes(iw_scr[0:tq, :], ck)
        for h in range(1, N_HEADS):
            sc = sc + s_all[h * tq:(h + 1) * tq, :] * _lanes(iw_scr[h * tq:(h + 1) * tq, :], ck)
        kpos = k0 + lax.broadcasted_iota(I32, (1, ck), 1)
        key_scr[c] = _sort_key(jnp.where(kpos <= qpos, sc, NEG))
        return carry

    lax.fori_loop(0, hi_chunk, score_body, 0)

    def count(pred, thr):
        thr_w = _lanes(thr, ck)

        def body(c, a):
            hit = pred(key_scr[c], thr_w).astype(F32)
            for j in range(ck // LANES):
                a = a + hit[:, j * LANES:(j + 1) * LANES]
            return a
        a = lax.fori_loop(0, hi_chunk, body, jnp.zeros((tq, LANES), F32))
        return jnp.sum(a, axis=-1, keepdims=True)

    def bit_step(it, t):
        cand = t + lax.shift_left(jnp.int32(1), 31 - it)
        return jnp.where(count(lambda kk, th: kk >= th, cand) >= n_top, cand, t)

    t = lax.fori_loop(0, 32, bit_step, jnp.full((tq, LANES), INT_MIN, I32))
    need = n_top - count(lambda kk, th: kk > th, t)
    t_w = _lanes(t, ck)
    _flash_init(m_scr, l_scr, acc_scr)

    def attn_body(c, before):
        k0 = pl.multiple_of(c * ck, ck)
        kk = key_scr[c]
        eq = kk == t_w
        rank_eq = before + _dot(eq.astype(BF16), low_scr[...])
        mask = ((kk > t_w) | (eq & (rank_eq < need))) & (kk > _HALF_NEG_KEY)
        s_all = _dot_nt(qz_scr[...], get_k(k0).astype(BF16))
        _flash_chunk(s_all, mask, get_v(k0).astype(BF16), tq, m_scr, l_scr, acc_scr)
        return before + jnp.sum(eq.astype(F32), axis=-1, keepdims=True)

    lax.fori_loop(0, hi_chunk, attn_body, jnp.zeros((tq, 1), F32))
    o_ref[0] = _assemble_heads(_flash_heads(l_scr, acc_scr, tq))


def dsa_attention(q, iq, misc, ik, kv, *, tq, ck, q0, n_top):
    b, t, dq = q.shape
    lp = kv.shape[1]
    cfg = dict(tq=tq, ck=ck, q0=q0, n_top=n_top)
    qspec = pl.BlockSpec((1, tq, dq), lambda bi, i: (bi, i, 0))
    rows = N_HEADS * tq
    return pl.pallas_call(
        functools.partial(_dsa_kernel, cfg),
        grid=(b, t // tq),
        in_specs=[qspec, qspec,
                  pl.BlockSpec((1, tq, LANES), lambda bi, i: (bi, i, 0)),
                  pl.BlockSpec((1, lp, DH), lambda bi, i: (bi, 0, 0)),
                  pl.BlockSpec((1, lp, LANES), lambda bi, i: (bi, 0, 0)),
                  pl.BlockSpec((1, lp, LANES), lambda bi, i: (bi, 0, 1))],
        out_specs=qspec,
        out_shape=jax.ShapeDtypeStruct((b, t, dq), F32),
        scratch_shapes=[pltpu.VMEM((rows, LANES), BF16), pltpu.VMEM((rows, DH), BF16),
                        pltpu.VMEM((rows, LANES), F32), pltpu.VMEM((lp // ck, tq, ck), I32),
                        pltpu.VMEM((ck, ck), BF16), pltpu.VMEM((rows, LANES), F32),
                        pltpu.VMEM((rows, LANES), F32), pltpu.VMEM((rows, LANES), F32)],
        compiler_params=_cparams(("arbitrary", "arbitrary")),
        name="dsa_attention",
    )(q, iq, misc, ik, kv, kv)


def _moba_select_kernel(cfg, q_ref, k_ref, bm_ref, km_scr):
    tq, nblk, nbp, n_top = cfg["tq"], cfg["nblk"], cfg["nbp"], cfg["n_top"]
    lp = k_ref.shape[1]
    qpos = cfg["q0"] + pl.program_id(1) * tq + lax.broadcasted_iota(I32, (tq, 1), 0)
    km_scr[...] = jnp.zeros(km_scr.shape, F32)
    for j in range(nblk):
        r1 = min((j + 1) * MOBA_BLOCK, lp)
        km_scr[j:j + 1, :] = jnp.sum(k_ref[0, j * MOBA_BLOCK:r1, :], axis=0, keepdims=True) * (1.0 / MOBA_BLOCK)
    km = km_scr[...].astype(BF16)
    own = jnp.concatenate([qpos // MOBA_BLOCK] * N_HEADS, axis=0)
    j = lax.broadcasted_iota(I32, (1, nbp), 1)
    s = _dot_nt(jnp.concatenate([_qz_block(q_ref, h, 1.0) for h in range(N_HEADS)], axis=0), km)
    s = jnp.where(j < own, s, NEG)
    sel = (j == own)
    if n_top > 0:
        sel = sel | (_top_few_mask(s, n_top) & (s > NEG / 2))
    sel = sel.astype(BF16)
    for h in range(N_HEADS):
        bm_ref[0, :, h * nbp:(h + 1) * nbp] = sel[h * tq:(h + 1) * tq, :]


def moba_select(q, kv, *, tq, q0, nblk, nbp):
    b, t, dq = q.shape
    lp = kv.shape[1]
    cfg = dict(tq=tq, q0=q0, nblk=nblk, nbp=nbp, n_top=min(MOBA_TOPK, nblk - 1))
    return pl.pallas_call(
        functools.partial(_moba_select_kernel, cfg),
        grid=(b, t // tq),
        in_specs=[pl.BlockSpec((1, tq, dq), lambda bi, i: (bi, i, 0)),
                  pl.BlockSpec((1, lp, LANES), lambda bi, i: (bi, 0, 0))],
        out_specs=pl.BlockSpec((1, tq, N_HEADS * nbp), lambda bi, i: (bi, i, 0)),
        out_shape=jax.ShapeDtypeStruct((b, t, N_HEADS * nbp), BF16),
        scratch_shapes=[pltpu.VMEM((nbp, LANES), F32)],
        compiler_params=_cparams(("arbitrary", "arbitrary")),
        name="moba_select",
    )(q, kv)


def _log_sigmoid(x):
    return jnp.minimum(x, 0.0) - jnp.log(1.0 + jnp.exp(-jnp.abs(x)))


def _dot3_rhs(a_bf, x):
    hi, mid, lo = _split3(x)
    return _dot(a_bf, hi) + _dot(a_bf, mid) + _dot(a_bf, lo)


def _dot3_lhs(x, b_bf):
    hi, mid, lo = _split3(x)
    return _dot(hi, b_bf) + _dot(mid, b_bf) + _dot(lo, b_bf)


def _mlstm_kernel(cfg, u_ref, v_ref, og_ref, g_ref, cin_ref, ct0_ref, n0_ref, m0_ref,
                  cw_ref, cb_ref, wq_ref, wk_ref, fb_ref, ng_ref,
                  h_out, ct_out, n_out, m_out,
                  ubuf, q_scr, k_scr, v_scr, g_scr, h_scr, ct_scr, n_scr, m_scr, hm_scr):
    tc, tcp, t_valid = cfg["tc"], cfg["tcp"], cfg["t_valid"]
    d_c = N_HEADS * DH
    i = pl.program_id(1)

    @pl.when(i == 0)
    def _():
        ubuf[0:8, :] = cin_ref[0]
        ct_scr[...] = ct0_ref[0]
        n_scr[...] = n0_ref[0]
        m_scr[...] = m0_ref[0]
        hm_scr[...] = (lax.broadcasted_iota(I32, (d_c, d_c), 0) // DH
                       == lax.broadcasted_iota(I32, (d_c, d_c), 1) // DH).astype(F32)

    if tc < tcp:
        ubuf[8:, :] = jnp.zeros((tcp, d_c), F32)
        v_scr[...] = jnp.zeros((tcp, d_c), F32)
        g_scr[...] = jnp.zeros((tcp, LANES), F32)
    ubuf[8:8 + tc, :] = u_ref[0]
    v_scr[0:tc, :] = v_ref[0]
    g_scr[0:tc, :] = g_ref[0]
    conv = ubuf[pl.ds(CONV_W + 1, tcp), :] * cw_ref[0:1, :]
    for j in range(1, CONV_W):
        conv = conv + ubuf[pl.ds(CONV_W + 1 + j, tcp), :] * cw_ref[j:j + 1, :]
    conv = conv + cb_ref[...]
    uc = (conv * jax.nn.sigmoid(conv)).astype(BF16)
    q_scr[...] = _dot(uc, wq_ref[...])
    k_scr[...] = _dot(uc, wk_ref[...]) * SCALE

    lane = lax.broadcasted_iota(I32, (1, LANES), 1)
    head_lane = lane < N_HEADS
    t_io = lax.broadcasted_iota(I32, (CHUNK, 1), 0)
    causal = lane <= t_io
    tri = (lax.broadcasted_iota(I32, (CHUNK, CHUNK), 1)
           <= lax.broadcasted_iota(I32, (CHUNK, CHUNK), 0)).astype(BF16)
    tri_t = (lax.broadcasted_iota(I32, (LANES, LANES), 0)
             <= lax.broadcasted_iota(I32, (LANES, LANES), 1)).astype(BF16)
    expand = (lax.broadcasted_iota(I32, (LANES, d_c), 0)
              == lax.broadcasted_iota(I32, (LANES, d_c), 1) // DH).astype(BF16)
    zeros_gate = jnp.zeros((CHUNK, LANES), F32)
    zeros_feat = jnp.zeros((CHUNK, d_c), F32)

    def chunk_body(c, carry):
        r0 = pl.multiple_of(c * CHUNK, CHUNK)
        hm = hm_scr[...]
        g = g_scr[pl.ds(r0, CHUNK), :]
        valid = (i * tc + r0 + t_io) < t_valid
        ig = jnp.where(head_lane, jnp.where(valid, g, NEG), 0.0)
        lf = pltpu.roll(_log_sigmoid(g + fb_ref[...]), LANES - N_HEADS, 1)
        lf = jnp.where(head_lane & valid, lf, 0.0)
        b_col = _dot3_rhs(tri, lf)
        ig_t = jnp.concatenate([ig, zeros_gate], axis=0).T[0:8, :]
        lf_t = jnp.concatenate([lf, zeros_gate], axis=0).T[0:8, :]
        rowterm = ig_t - _dot3_lhs(lf_t, tri_t)
        m_row = m_scr[...]
        dws, iws, emts = [], [], []
        for h in range(N_HEADS):
            bc = b_col[:, h:h + 1]
            dlog = jnp.where(causal, bc + rowterm[h:h + 1, :], NEG)
            inter = bc + m_row[:, h:h + 1]
            m_t = jnp.maximum(inter, jnp.max(dlog, axis=-1, keepdims=True))
            dws.append(jnp.exp(dlog - m_t))
            iws.append(jnp.exp(inter - m_t))
            emts.append(jnp.exp(-m_t))
        dw = jnp.concatenate(dws, axis=0)
        iw = jnp.concatenate(iws, axis=0)
        emt = jnp.concatenate(emts, axis=0)
        q_c = q_scr[pl.ds(r0, CHUNK), :]
        k_c = k_scr[pl.ds(r0, CHUNK), :]
        v_c = v_scr[pl.ds(r0, CHUNK), :]
        qz = jnp.concatenate([q_c] * N_HEADS, axis=0) * hm
        qz_bf = qz.astype(BF16)
        k_pad = jnp.concatenate([k_c, zeros_feat], axis=0)
        v_pad = jnp.concatenate([v_c, zeros_feat], axis=0).astype(BF16)
        qkw = _dot_nt(qz_bf, k_pad.astype(BF16)) * dw
        intra = _dot(qkw.astype(BF16), v_pad)
        inter_z = _dot(qz_bf, ct_scr[...].astype(BF16))
        num = iw * inter_z + intra * hm
        den = iw * jnp.sum(qz * n_scr[...], axis=-1, keepdims=True) + jnp.sum(qkw, axis=-1, keepdims=True)
        hz = num / jnp.maximum(jnp.abs(den), emt)
        h_c = hz[0:CHUNK, :]
        for h in range(1, N_HEADS):
            h_c = h_c + hz[h * CHUNK:(h + 1) * CHUNK, :]
        h_scr[pl.ds(r0, CHUNK), :] = h_c
        b_last = b_col[CHUNK - 1:CHUNK, :]
        m_new = jnp.maximum(b_last + m_row, jnp.max(b_last - b_col + ig, axis=0, keepdims=True))
        decay = jnp.where(head_lane, jnp.exp(b_last + m_row - m_new), 0.0)
        ws = jnp.where(head_lane, jnp.exp(b_last - b_col + ig - m_new), 0.0)
        wide = _dot3_lhs(jnp.concatenate([ws, jnp.broadcast_to(decay, (8, LANES))], axis=0), expand)
        w8, decay_w = wide[0:CHUNK, :], wide[CHUNK:CHUNK + 1, :]
        vw_pad = jnp.concatenate([v_c * w8, zeros_feat], axis=0).astype(BF16)
        k_t = jnp.concatenate([k_pad[:, j * LANES:(j + 1) * LANES].T for j in range(d_c // LANES)], axis=0)
        ct_scr[...] = decay_w * ct_scr[...] + _dot(k_t.astype(BF16), vw_pad) * hm
        n_scr[...] = decay_w * n_scr[...] + jnp.sum(k_c * w8, axis=0, keepdims=True)
        m_scr[...] = m_new
        return carry

    n_chunks = tcp // CHUNK
    if n_chunks % 4 == 0:
        def four_chunks(c4, carry):
            for k in range(4):
                carry = chunk_body(4 * c4 + k, carry)
            return carry
        lax.fori_loop(0, n_chunks // 4, four_chunks, 0)
    else:
        lax.fori_loop(0, n_chunks, chunk_body, 0)
    h_all = h_scr[...]
    hi, mid, lo = _split3(h_all * h_all)
    hm_bf = hm_scr[...].astype(BF16)
    ms = (_dot(hi, hm_bf) + _dot(mid, hm_bf) + _dot(lo, hm_bf)) * (1.0 / DH)
    hc = h_all * lax.rsqrt(ms + EPS) * ng_ref[...]
    h_out[0] = jax.nn.sigmoid(og_ref[0]) * hc[0:tc, :]
    ubuf[0:8, :] = ubuf[tc:tc + 8, :]

    @pl.when(i == pl.num_programs(1) - 1)
    def _():
        ct_out[0] = ct_scr[...]
        n_out[0] = n_scr[...]
        m_out[0] = m_scr[...]


def mlstm(u, v, og, misc, conv_in, ct0, n0, m0, conv_w, conv_b, wq_bd, wk_bd, fb_row, norm_g, *, tc, t_valid):
    b, t, d_c = u.shape
    tcp = max(tc, CHUNK)
    cfg = dict(tc=tc, tcp=tcp, t_valid=t_valid)
    row = pl.BlockSpec((1, tc, d_c), lambda bi, i: (bi, i, 0))
    const = lambda shape: pl.BlockSpec(shape, lambda bi, i: (0,) * len(shape))
    per_b = lambda shape: pl.BlockSpec((1,) + shape, lambda bi, i: (bi,) + (0,) * len(shape))
    return pl.pallas_call(
        functools.partial(_mlstm_kernel, cfg),
        grid=(b, t // tc),
        in_specs=[row, row, row, pl.BlockSpec((1, tc, LANES), lambda bi, i: (bi, i, 0)),
                  per_b((8, d_c)), per_b((d_c, d_c)), per_b((1, d_c)), per_b((1, LANES)),
                  const((CONV_W, d_c)), const((1, d_c)), const((d_c, d_c)), const((d_c, d_c)),
                  const((1, LANES)), const((1, d_c))],
        out_specs=[row, per_b((d_c, d_c)), per_b((1, d_c)), per_b((1, LANES))],
        out_shape=[jax.ShapeDtypeStruct((b, t, d_c), F32),
                   jax.ShapeDtypeStruct((b, d_c, d_c), F32),
                   jax.ShapeDtypeStruct((b, 1, d_c), F32),
                   jax.ShapeDtypeStruct((b, 1, LANES), F32)],
        scratch_shapes=[pltpu.VMEM((tcp + 8, d_c), F32),
                        pltpu.VMEM((tcp, d_c), F32), pltpu.VMEM((tcp, d_c), F32), pltpu.VMEM((tcp, d_c), F32),
                        pltpu.VMEM((tcp, LANES), F32), pltpu.VMEM((tcp, d_c), F32),
                        pltpu.VMEM((d_c, d_c), F32), pltpu.VMEM((1, d_c), F32), pltpu.VMEM((1, LANES), F32),
                        pltpu.VMEM((d_c, d_c), F32)],
        compiler_params=_cparams(("arbitrary", "arbitrary")),
        name="mlstm",
    )(u, v, og, misc, conv_in, ct0, n0, m0, conv_w, conv_b, wq_bd, wk_bd, fb_row, norm_g)


def _page_copies(pool_ref, layer, pt_ref, b, row0, buf_ref, slot, sems):
    n_pages, page = pt_ref.shape[1], pool_ref.shape[3]
    nrows = buf_ref.shape[1]
    return [pltpu.make_async_copy(pool_ref.at[layer, pt_ref[b, p], pl.ds(row0, nrows), :],
                                  buf_ref.at[slot, :, pl.ds(p * page, page)], sems.at[slot, p])
            for p in range(n_pages)]


def _prefetch_pages(streams, pt_ref, layer):
    b = pl.program_id(0)
    slot = b % 2

    def start(seq, to_slot):
        for pool_ref, row0, buf_ref, sems, _ in streams:
            for cp in _page_copies(pool_ref, layer, pt_ref, seq, row0, buf_ref, to_slot, sems):
                cp.start()

    @pl.when(b == 0)
    def _():
        start(0, 0)

    @pl.when(b + 1 < pl.num_programs(0))
    def _():
        start(b + 1, 1 - slot)

    for pool_ref, row0, buf_ref, sems, new_ref in streams:
        n_tok = pt_ref.shape[1] * pool_ref.shape[3]
        buf_ref[slot, :, n_tok:] = new_ref[0]
        for cp in _page_copies(pool_ref, layer, pt_ref, b, row0, buf_ref, slot, sems):
            cp.wait()
    return slot


def _stack_qpos(qpos, n):
    return jnp.concatenate([qpos] * n, axis=0)


def _flash_rows(s, keep, v_t, m_scr, l_scr, acc_scr):
    ck = s.shape[1]
    if keep is not None:
        s = jnp.where(keep, s, -MASK_BIG)
    m_old = m_scr[...]
    m_new = jnp.maximum(m_old, jnp.max(s, axis=-1, keepdims=True))
    p = jnp.exp2(s - _lanes(m_new, ck))
    alpha = jnp.exp2(m_old - m_new)
    l_scr[...] = alpha * l_scr[...] + jnp.sum(p, axis=-1, keepdims=True)
    m_scr[...] = m_new
    acc_scr[...] = alpha * acc_scr[...] + _dot_nt(p.astype(BF16), v_t)


def _dec_battn_kernel(cfg, pt_ref, q_ref, pool_ref, knew_ref, vnew_ref, bm_ref, g_ref, o_ref,
                      qa_scr, m_scr, l_scr, acc_scr, k_buf, v_buf, sems):
    tq, ck, nbp, mg, lp = cfg["tq"], cfg["ck"], cfg["nbp"], cfg["mask_group"], cfg["lp"]
    slot = _prefetch_pages([(pool_ref, cfg["k_col"] * LANES, k_buf, sems.at[0], knew_ref),
                            (pool_ref, cfg["v_col"] * LANES, v_buf, sems.at[1], vnew_ref)], pt_ref, cfg["layer"])
    qbase = cfg["q0"]
    for h in range(N_HEADS):
        g = h // mg
        qa_scr[h * tq:(h + 1) * tq, 0:LANES] = _qz_block(q_ref, h, SCALE * LOG2E)
        qa_scr[h * tq:(h + 1) * tq, LANES:LANES + nbp] = bm_ref[0, :, g * nbp:(g + 1) * nbp] - 1
    _flash_init(m_scr, l_scr, acc_scr)
    qpos = _stack_qpos(qbase + lax.broadcasted_iota(I32, (tq, 1), 0), N_HEADS)
    last = min((qbase + tq - 1) // ck, lp // ck - 1)

    for c in range(last + 1):
        k0 = c * ck
        kpos = k0 + lax.broadcasted_iota(I32, (1, ck), 1)
        bias = jnp.where(lax.broadcasted_iota(I32, (nbp, ck), 0) == lax.shift_right_logical(kpos, cfg["bshift"]),
                         MASK_BIG, 0.0).astype(BF16)
        k_aug = jnp.concatenate([k_buf[slot, :, k0:k0 + ck].astype(BF16), bias], axis=0)
        keep = (kpos <= qpos) if c == last else None
        _flash_rows(_dot(qa_scr[...], k_aug), keep, v_buf[slot, :, k0:k0 + ck].astype(BF16), m_scr, l_scr, acc_scr)
    heads = _flash_heads(l_scr, acc_scr, tq)
    if cfg["gate_col"] is not None:
        heads = [o * _gate(g_ref, cfg["gate_col"] + h) for h, o in enumerate(heads)]
    o_ref[0] = _assemble_heads(heads)


def dec_block_attention(q, pool_t, k_col, v_col, gates, bmask, new_t, *, layer, page_table, ck, q0, nbp, bshift,
                        mask_group, gate_col):
    b, tq, dq = q.shape
    n_pages, page = page_table.shape[1], pool_t.shape[3]
    lp = (n_pages + 1) * page
    assert lp % ck == 0 and (q0 + tq - 1) // ck == q0 // ck
    cfg = dict(tq=tq, ck=ck, q0=q0, nbp=nbp, bshift=bshift, mask_group=mask_group, gate_col=gate_col,
               lp=lp, layer=layer, k_col=k_col, v_col=v_col)
    rows = N_HEADS * tq
    im = lambda bi, pt: (bi, 0, 0)
    return pl.pallas_call(
        functools.partial(_dec_battn_kernel, cfg),
        grid_spec=pltpu.PrefetchScalarGridSpec(
            num_scalar_prefetch=1, grid=(b,),
            in_specs=[pl.BlockSpec((1, tq, dq), im), pl.BlockSpec(memory_space=pl.ANY),
                      pl.BlockSpec((1, LANES, page), lambda bi, pt: (bi, k_col, 0)),
                      pl.BlockSpec((1, LANES, page), lambda bi, pt: (bi, v_col, 0)),
                      pl.BlockSpec((1, tq, bmask.shape[2]), im), pl.BlockSpec((1, tq, LANES), im)],
            out_specs=pl.BlockSpec((1, tq, dq), im),
            scratch_shapes=[pltpu.VMEM((rows, LANES + nbp), BF16), pltpu.VMEM((rows, LANES), F32),
                            pltpu.VMEM((rows, LANES), F32), pltpu.VMEM((rows, LANES), F32),
                            pltpu.VMEM((2, LANES, lp), F32), pltpu.VMEM((2, LANES, lp), F32),
                            pltpu.SemaphoreType.DMA((2, 2, n_pages))]),
        out_shape=jax.ShapeDtypeStruct((b, tq, dq), F32),
        compiler_params=_cparams(("arbitrary",)), name="dec_attn_block",
    )(page_table, q, pool_t, new_t, new_t, bmask, gates)


def _dec_compress_kernel(cfg, pt_ref, pool_ref, knew_ref, vnew_ref, pe_ref, w_ref, o_ref,
                         a_scr, b_scr, x_scr, k_buf, v_buf, sems):
    n_groups, lp = cfg["n_groups"], cfg["lp"]
    slot = _prefetch_pages([(pool_ref, 0, k_buf, sems.at[0], knew_ref),
                            (pool_ref, LANES, v_buf, sems.at[1], vnew_ref)], pt_ref, cfg["layer"])
    o_ref[0] = jnp.zeros(o_ref.shape[1:], F32)
    for kind, buf in enumerate((k_buf, v_buf)):
        for p in range(lp // LANES):
            x_scr[p * LANES:(p + 1) * LANES, :] = buf[slot, :, p * LANES:(p + 1) * LANES].T
        acc_a = jnp.zeros((n_groups, LANES), F32)
        acc_b = jnp.zeros((n_groups, LANES), F32)
        for l in range(CMP_STRIDE):
            x = x_scr[pl.ds(l, n_groups, stride=CMP_STRIDE), :]
            acc_a += _dot((x + pe_ref[kind, l:l + 1, :]).astype(BF16), w_ref[kind, l])
            acc_b += _dot((x + pe_ref[kind, CMP_STRIDE + l:CMP_STRIDE + l + 1, :]).astype(BF16),
                          w_ref[kind, CMP_STRIDE + l])
        a_scr[...] = acc_a
        b_scr[0:n_groups, :] = acc_b
        b_scr[n_groups:n_groups + 8, :] = jnp.zeros((8, LANES), F32)
        o_ref[0, 0:n_groups, kind * LANES:(kind + 1) * LANES] = a_scr[...] + b_scr[pl.ds(1, n_groups), :]


def dec_compress(pool_t, pe2, w_bd, ncp, new_t, *, layer, page_table):
    b = new_t.shape[0]
    n_pages, page = page_table.shape[1], pool_t.shape[3]
    lp = (n_pages + 1) * page
    n_groups = lp // CMP_STRIDE
    cfg = dict(n_groups=n_groups, lp=lp, layer=layer)
    return pl.pallas_call(
        functools.partial(_dec_compress_kernel, cfg),
        grid_spec=pltpu.PrefetchScalarGridSpec(
            num_scalar_prefetch=1, grid=(b,),
            in_specs=[pl.BlockSpec(memory_space=pl.ANY),
                      pl.BlockSpec((1, LANES, page), lambda bi, pt: (bi, 0, 0)),
                      pl.BlockSpec((1, LANES, page), lambda bi, pt: (bi, 1, 0)),
                      pl.BlockSpec((2, CMP_LEN, LANES), lambda bi, pt: (0, 0, 0)),
                      pl.BlockSpec((2, CMP_LEN, LANES, LANES), lambda bi, pt: (0, 0, 0, 0))],
            out_specs=pl.BlockSpec((1, ncp, 2 * LANES), lambda bi, pt: (bi, 0, 0)),
            scratch_shapes=[pltpu.VMEM((n_groups, LANES), F32), pltpu.VMEM((n_groups + 8, LANES), F32),
                            pltpu.VMEM((lp, LANES), F32),
                            pltpu.VMEM((2, LANES, lp), F32), pltpu.VMEM((2, LANES, lp), F32),
                            pltpu.SemaphoreType.DMA((2, 2, n_pages))]),
        out_shape=jax.ShapeDtypeStruct((b, ncp, 2 * LANES), F32),
        compiler_params=_cparams(("arbitrary",)), name="dec_compress",
    )(page_table, pool_t, new_t, new_t, pe2, w_bd)


def _dec_dsa_kernel(cfg, pt_ref, q_ref, iq_ref, g_ref, ipool_ref, inew_ref, pool_ref, knew_ref, vnew_ref, o_ref,
                    qz_scr, iq_scr, key_scr, low_scr, m_scr, l_scr, acc_scr, i_buf, k_buf, v_buf, sems):
    tq, ck, n_top, lp = cfg["tq"], cfg["ck"], cfg["n_top"], cfg["lp"]
    slot = _prefetch_pages([(ipool_ref, 0, i_buf, sems.at[0], inew_ref),
                            (pool_ref, 0, k_buf, sems.at[1], knew_ref),
                            (pool_ref, LANES, v_buf, sems.at[2], vnew_ref)], pt_ref, cfg["layer"])
    qbase = cfg["q0"]
    qpos = qbase + lax.broadcasted_iota(I32, (tq, 1), 0)
    n_chunks = min((qbase + tq - 1) // ck + 1, lp // ck)
    for h in range(N_HEADS):
        qz_scr[h * tq:(h + 1) * tq, :] = _qz_block(q_ref, h, SCALE * LOG2E)
        pair = iq_ref[0, :, (h // 2) * LANES:(h // 2 + 1) * LANES]
        if h % 2:
            pair = pltpu.roll(pair, DH, 1)
        iq_scr[h * tq:(h + 1) * tq, :] = pair[:, :DH].astype(BF16)
    low_scr[...] = (lax.broadcasted_iota(I32, (ck, ck), 0)
                    < lax.broadcasted_iota(I32, (ck, ck), 1)).astype(BF16)
    iw = [jnp.broadcast_to(g_ref[0, :, h:h + 1], (tq, LANES)) for h in range(N_HEADS)]

    for c in range(n_chunks):
        k0 = c * ck
        s_all = jnp.maximum(_dot(iq_scr[...], i_buf[slot, :, k0:k0 + ck].astype(BF16)), 0.0)
        sc = s_all[0:tq, :] * _lanes(iw[0], ck)
        for h in range(1, N_HEADS):
            sc = sc + s_all[h * tq:(h + 1) * tq, :] * _lanes(iw[h], ck)
        kpos = k0 + lax.broadcasted_iota(I32, (1, ck), 1)
        key_scr[c] = _sort_key(jnp.where(kpos <= qpos, sc, NEG))

    def count(pred, thr):
        thr_w = _lanes(thr, ck)

        def body(c, a):
            hit = pred(key_scr[c], thr_w).astype(F32)
            for j in range(ck // LANES):
                a = a + hit[:, j * LANES:(j + 1) * LANES]
            return a
        a = lax.fori_loop(0, n_chunks, body, jnp.zeros((tq, LANES), F32))
        return jnp.sum(a, axis=-1, keepdims=True)

    def digit_step(it, t):
        shift = 30 - 2 * it
        cands = [_lanes(t + lax.shift_left(jnp.int32(c), shift), ck) for c in (1, 2, 3)]

        def body(c, accs):
            kk = key_scr[c]
            out = []
            for cand, a in zip(cands, accs):
                hit = (kk >= cand).astype(F32)
                for j in range(ck // LANES):
                    a = a + hit[:, j * LANES:(j + 1) * LANES]
                out.append(a)
            return tuple(out)

        accs = lax.fori_loop(0, n_chunks, body, (jnp.zeros((tq, LANES), F32),) * 3)
        digit = jnp.zeros((tq, LANES), I32)
        for a in accs:
            digit = digit + (jnp.sum(a, axis=-1, keepdims=True) >= n_top).astype(I32)
        return t + lax.shift_left(digit, shift)

    t = lax.fori_loop(0, 16, digit_step, jnp.full((tq, LANES), INT_MIN, I32))
    need = n_top - count(lambda kk, th: kk > th, t)
    t_w = _lanes(t, ck)
    _flash_init(m_scr, l_scr, acc_scr)

    before = jnp.zeros((tq, 1), F32)
    for c in range(n_chunks):
        k0 = c * ck
        kk = key_scr[c]
        eq = kk == t_w
        rank_eq = before + _dot(eq.astype(BF16), low_scr[...])
        mask = ((kk > t_w) | (eq & (rank_eq < need))) & (kk > _HALF_NEG_KEY)
        keep = jnp.concatenate([mask.astype(F32)] * N_HEADS, axis=0) > 0.5
        s = _dot(qz_scr[...], k_buf[slot, :, k0:k0 + ck].astype(BF16))
        _flash_rows(s, keep, v_buf[slot, :, k0:k0 + ck].astype(BF16), m_scr, l_scr, acc_scr)
        before = before + jnp.sum(eq.astype(F32), axis=-1, keepdims=True)
    o_ref[0] = _assemble_heads(_flash_heads(l_scr, acc_scr, tq))


def dec_dsa_attention(q, iq, misc, ipool_t, pool_t, inew_t, new_t, *, layer, page_table, ck, q0, n_top):
    b, tq, dq = q.shape
    n_pages, page = page_table.shape[1], pool_t.shape[3]
    lp = (n_pages + 1) * page
    cfg = dict(tq=tq, ck=ck, q0=q0, n_top=n_top, lp=lp, layer=layer)
    rows = N_HEADS * tq
    im = lambda bi, pt: (bi, 0, 0)
    qspec = pl.BlockSpec((1, tq, dq), im)
    return pl.pallas_call(
        functools.partial(_dec_dsa_kernel, cfg),
        grid_spec=pltpu.PrefetchScalarGridSpec(
            num_scalar_prefetch=1, grid=(b,),
            in_specs=[qspec, qspec, pl.BlockSpec((1, tq, LANES), im),
                      pl.BlockSpec(memory_space=pl.ANY), pl.BlockSpec((1, DH, page), im),
                      pl.BlockSpec(memory_space=pl.ANY),
                      pl.BlockSpec((1, LANES, page), lambda bi, pt: (bi, 0, 0)),
                      pl.BlockSpec((1, LANES, page), lambda bi, pt: (bi, 1, 0))],
            out_specs=qspec,
            scratch_shapes=[pltpu.VMEM((rows, LANES), BF16), pltpu.VMEM((rows, DH), BF16),
                            pltpu.VMEM((lp // ck, tq, ck), I32), pltpu.VMEM((ck, ck), BF16),
                            pltpu.VMEM((rows, LANES), F32), pltpu.VMEM((rows, LANES), F32),
                            pltpu.VMEM((rows, LANES), F32),
                            pltpu.VMEM((2, DH, lp), F32), pltpu.VMEM((2, LANES, lp), F32),
                            pltpu.VMEM((2, LANES, lp), F32), pltpu.SemaphoreType.DMA((3, 2, n_pages))]),
        out_shape=jax.ShapeDtypeStruct((b, tq, dq), F32),
        compiler_params=_cparams(("arbitrary",)), name="dec_dsa",
    )(page_table, q, iq, misc, ipool_t, inew_t, pool_t, new_t, new_t)


def _dec_moba_select_kernel(cfg, pt_ref, q_ref, pool_ref, knew_ref, bm_ref, k_buf, sems):
    tq, nblk, nbp, n_top, lp = cfg["tq"], cfg["nblk"], cfg["nbp"], cfg["n_top"], cfg["lp"]
    slot = _prefetch_pages([(pool_ref, 0, k_buf, sems, knew_ref)], pt_ref, cfg["layer"])
    qpos = cfg["q0"] + lax.broadcasted_iota(I32, (tq, 1), 0)
    lane = lax.broadcasted_iota(I32, (1, nbp), 1)
    km_t = jnp.zeros((LANES, nbp), F32)
    for j in range(nblk):
        r1 = min((j + 1) * MOBA_BLOCK, lp)
        col = jnp.sum(k_buf[slot, :, j * MOBA_BLOCK:r1], axis=1, keepdims=True) * (1.0 / MOBA_BLOCK)
        km_t = jnp.where(lane == j, col, km_t)
    own = _stack_qpos(qpos // MOBA_BLOCK, N_HEADS)
    s = _dot(jnp.concatenate([_qz_block(q_ref, h, 1.0) for h in range(N_HEADS)], axis=0), km_t.astype(BF16))
    s = jnp.where(lane < own, s, NEG)
    sel = (lane == own)
    if n_top > 0:
        sel = sel | (_top_few_mask(s, n_top) & (s > NEG / 2))
    sel = sel.astype(BF16)
    for h in range(N_HEADS):
        bm_ref[0, :, h * nbp:(h + 1) * nbp] = sel[h * tq:(h + 1) * tq, :]


def dec_moba_select(q, pool_t, new_t, *, layer, page_table, q0, nblk, nbp):
    b, tq, dq = q.shape
    n_pages, page = page_table.shape[1], pool_t.shape[3]
    lp = (n_pages + 1) * page
    cfg = dict(tq=tq, q0=q0, nblk=nblk, nbp=nbp, n_top=min(MOBA_TOPK, nblk - 1), lp=lp, layer=layer)
    im = lambda bi, pt: (bi, 0, 0)
    return pl.pallas_call(
        functools.partial(_dec_moba_select_kernel, cfg),
        grid_spec=pltpu.PrefetchScalarGridSpec(
            num_scalar_prefetch=1, grid=(b,),
            in_specs=[pl.BlockSpec((1, tq, dq), im), pl.BlockSpec(memory_space=pl.ANY),
                      pl.BlockSpec((1, LANES, page), im)],
            out_specs=pl.BlockSpec((1, tq, N_HEADS * nbp), im),
            scratch_shapes=[pltpu.VMEM((2, LANES, lp), F32), pltpu.SemaphoreType.DMA((2, n_pages))]),
        out_shape=jax.ShapeDtypeStruct((b, tq, N_HEADS * nbp), BF16),
        compiler_params=_cparams(("arbitrary",)), name="dec_moba_select",
    )(page_table, q, pool_t, new_t)


def _dec_win_kernel(cfg, q_ref, kv_ref, g_ref, o_ref):
    tq, koff = cfg["tq"], cfg["koff"]
    lw = kv_ref.shape[2]
    kpos = koff + lax.broadcasted_iota(I32, (1, lw), 1)
    qpos = _stack_qpos(cfg["q0"] + lax.broadcasted_iota(I32, (tq, 1), 0), N_HEADS)
    mask = (kpos <= qpos) & (qpos - kpos < WINDOW) & (kpos >= 0)
    qz = jnp.concatenate([_qz_block(q_ref, h, SCALE) for h in range(N_HEADS)], axis=0)
    s = jnp.where(mask, _dot(qz, kv_ref[0, 0:LANES, :].astype(BF16)), NEG)
    e = jnp.where(mask, jnp.exp(s - jnp.max(s, axis=-1, keepdims=True)), 0.0)
    l = jnp.maximum(jnp.sum(e, axis=-1, keepdims=True), TINY)
    o_all = _dot_nt(e.astype(BF16), kv_ref[0, LANES:2 * LANES, :].astype(BF16)) / l
    o_ref[0] = _assemble_heads([o_all[h * tq:(h + 1) * tq, :] * _gate(g_ref, cfg["gate_col"] + h)
                                for h in range(N_HEADS)])


def dec_window_attention(q, kv_t, gates, *, q0, koff, gate_col):
    b, tq, dq = q.shape
    lw = kv_t.shape[2]
    cfg = dict(tq=tq, q0=q0, koff=koff, gate_col=gate_col)
    return pl.pallas_call(
        functools.partial(_dec_win_kernel, cfg),
        grid=(b,),
        in_specs=[pl.BlockSpec((1, tq, dq), lambda bi: (bi, 0, 0)),
                  pl.BlockSpec((1, 2 * LANES, lw), lambda bi: (bi, 0, 0)),
                  pl.BlockSpec((1, tq, LANES), lambda bi: (bi, 0, 0))],
        out_specs=pl.BlockSpec((1, tq, dq), lambda bi: (bi, 0, 0)),
        out_shape=jax.ShapeDtypeStruct((b, tq, dq), F32),
        compiler_params=_cparams(("arbitrary",)), name="dec_attn_window",
    )(q, kv_t, gates)


def _regroup_columns(w, b, pieces):
    n_src = w.shape[-1]
    idx = []
    for s, wd, wp in pieces:
        idx += list(range(s, s + wd)) + [n_src] * (wp - wd)
    idx = np.asarray(idx, np.int32)
    w_ext = jnp.concatenate([w, jnp.zeros(w.shape[:-1] + (1,), w.dtype)], axis=-1)
    b_ext = jnp.concatenate([b, jnp.zeros(b.shape[:-1] + (1,), b.dtype)], axis=-1)
    return jnp.take(w_ext, idx, axis=-1).astype(BF16), jnp.take(b_ext, idx, axis=-1)


def _block_diag(w):
    h, a, b = w.shape[-3:]
    eye = jnp.eye(h, dtype=w.dtype)
    out = w[..., :, :, None, :] * eye[:, None, :, None]
    return out.reshape(w.shape[:-3] + (h * a, h * b))


_EVEN_SRC = [(0, 512, 512), (512, 512, 512), (1024, 256, 256), (1304, 512, 512), (1816, 256, 256),
             (2072, 512, 512), (2584, 64, 128), (2648, 8, 8), (1280, 24, 120)]
_EVEN_GROUPS = [(0, 512, 512), (512, 512, 512), (1024, 256, 256), (1280, 512, 512), (1792, 256, 256),
                (2048, 512, 512), (2560, 128, 64), (2688, 128, 128)]
_ODD_SRC = [(0, 512, 512), (512, 512, 512), (1024, 512, 512), (1552, 512, 512), (2064, 256, 256),
            (1536, 16, 128)]
_ODD_GROUPS = [(0, 512, 512), (512, 512, 512), (1024, 512, 512), (1536, 512, 512), (2048, 256, 256),
               (2304, 128, 128)]
_GATE_COL = N_HEADS


def _pick_chunk(n, limit):
    return max(c for c in range(LANES, limit + 1, LANES) if n % c == 0)


def _pad_rows(a, rows):
    return jnp.pad(a, ((0, 0), (0, rows - a.shape[1])) + ((0, 0),) * (a.ndim - 2))


def _new_cols(a, t_real, width):
    keep = (jnp.arange(a.shape[1]) < t_real)[None, :, None]
    return jnp.swapaxes(_pad_rows(jnp.where(keep, a, 0.0), width), 1, 2)


def _even_mixer(p, i, outs, bsz, t, t_real, past, page_table, q0):
    a_q, kv4, win, b_q, b_kv, b_iq, b_ik, misc = [o.reshape(bsz, t, -1) for o in outs]
    lk = t if past is None else q0 + t_real
    n16 = -(-lk // CMP_STRIDE)
    ns = -(-lk // SEL_BLOCK)
    nsp = -(-ns // LANES) * LANES
    gates = dict(cmp=_GATE_COL, sel=_GATE_COL + N_HEADS, win=_GATE_COL + 2 * N_HEADS)
    if past is None:
        tq, ck = 128, 512
        cmp_kv = nsa_compress(kv4, p["pe2"][i], p["cmp_w"][i], -(-(t // CMP_STRIDE) // LANES) * LANES)
        o_cmp, bmask = nsa_select(a_q, cmp_kv, misc, tq=tq, q0=q0, nc=n16 - 1, ns=ns, nsp=nsp, gate_col=gates["cmp"])
        o_sel = block_attention(a_q, kv4, 2, 3, misc, bmask, tq=tq, ck=ck, q0=q0, nbp=nsp, bshift=6,
                                mask_group=GROUP, gate_col=gates["sel"])
        o_win = window_attention(a_q, win, misc, tq=tq, ckw=WINDOW + tq, q0=q0, koff=0, gate_col=gates["win"])
        o_dsa = dsa_attention(b_q, b_iq, misc, b_ik, b_kv, tq=tq, ck=ck, q0=q0, n_top=min(DSA_TOPK, lk // 4))
        win_state = win[:, -min(WINDOW, t):].reshape(bsz, -1, 2, HKV, DH)
    else:
        page = past["nsa"].shape[3]
        pg = dict(layer=i, page_table=page_table)
        lp = (page_table.shape[1] + 1) * page
        ck = _pick_chunk(lp, 640)
        nsa_new = _new_cols(kv4, t_real, page)
        cmp_kv = dec_compress(past["nsa"], p["pe2"][i], p["cmp_w"][i], -(-(lp // CMP_STRIDE) // LANES) * LANES,
                              nsa_new, **pg)
        o_cmp, bmask = nsa_select(a_q, cmp_kv, misc, tq=t, q0=q0, nc=n16 - 1, ns=ns, nsp=nsp, gate_col=gates["cmp"])
        o_sel = dec_block_attention(a_q, past["nsa"], 2, 3, misc, bmask, nsa_new, ck=ck, q0=q0, nbp=nsp, bshift=6,
                                    mask_group=GROUP, gate_col=gates["sel"], **pg)
        win_buf = past["win"][i]
        n_buf = win_buf.shape[2]
        win_t = jnp.concatenate([win_buf, _new_cols(win, t_real, -(-t_real // LANES) * LANES)], axis=2)
        o_win = dec_window_attention(a_q, win_t, misc, q0=q0, koff=q0 - n_buf, gate_col=gates["win"])
        o_dsa = dec_dsa_attention(b_q, b_iq, misc, past["idx"], past["dsa"], _new_cols(b_ik, t_real, page),
                                  _new_cols(b_kv, t_real, page), ck=ck, q0=q0, n_top=min(DSA_TOPK, lk // 4), **pg)
        win_state = jnp.swapaxes(win_t[:, :, t_real:t_real + n_buf], 1, 2).reshape(bsz, n_buf, 2, HKV, DH)
    state = (kv4[:, :t_real].reshape(bsz, t_real, 4, HKV, DH), win_state,
             b_kv[:, :t_real].reshape(bsz, t_real, 2, HKV, DH), b_ik[:, :t_real])
    return [o_cmp, o_sel, o_win], o_dsa, state


def _odd_mixer(p, i, outs, bsz, t, t_real, past, page_table, q0):
    u, c_v, c_o, d_q, d_kv, misc = [o.reshape(bsz, t, -1) for o in outs]
    d_c = N_HEADS * DH
    idx = np.arange(N_HEADS)
    lk = t if past is None else q0 + t_real
    nblk = -(-lk // MOBA_BLOCK)
    if past is None:
        ct0 = jnp.zeros((bsz, d_c, d_c), F32)
        n0 = jnp.zeros((bsz, 1, d_c), F32)
        m0 = jnp.zeros((bsz, 1, LANES), F32)
        conv_prev = jnp.zeros((bsz, CONV_W - 1, d_c), F32)
        tc = 512
        bmask = moba_select(d_q, d_kv, tq=128, q0=q0, nblk=nblk, nbp=LANES)
        o_d = block_attention(d_q, d_kv, 0, 1, misc, bmask, tq=128, ck=512, q0=q0, nbp=LANES, bshift=8,
                              mask_group=1, gate_col=None)
    else:
        page = past["moba"].shape[3]
        pg = dict(layer=i, page_table=page_table)
        c_t = jnp.swapaxes(past["c"][i], -1, -2)
        ct0 = jnp.zeros((bsz, N_HEADS, DH, N_HEADS, DH), F32).at[:, idx, :, idx, :].set(
            jnp.moveaxis(c_t, 1, 0)).reshape(bsz, d_c, d_c)
        n0 = past["n"][i].reshape(bsz, 1, d_c)
        m0 = jnp.pad(past["m"][i], ((0, 0), (0, LANES - N_HEADS)))[:, None, :]
        conv_prev = past["conv"][i]
        tc = t
        moba_new = _new_cols(d_kv, t_real, page)
        bmask = dec_moba_select(d_q, past["moba"], moba_new, q0=q0, nblk=nblk, nbp=LANES, **pg)
        o_d = dec_block_attention(d_q, past["moba"], 0, 1, misc, bmask, moba_new,
                                  ck=_pick_chunk((page_table.shape[1] + 1) * page, 640), q0=q0, nbp=LANES,
                                  bshift=8, mask_group=1, gate_col=None, **pg)
    conv_in = jnp.concatenate([jnp.zeros((bsz, 8 - (CONV_W - 1), d_c), F32), conv_prev], axis=1)
    hc, ct1, n1, m1 = mlstm(u, c_v, c_o, misc, conv_in, ct0, n0, m0, p["conv_w"][i], p["conv_b"][i],
                            p["wq_bd"][i], p["wk_bd"][i], p["fb_row"][i], p["norm_g"][i], tc=tc, t_valid=t_real)
    c1 = jnp.swapaxes(jnp.moveaxis(ct1.reshape(bsz, N_HEADS, DH, N_HEADS, DH)[:, idx, :, idx, :], 0, 1), -1, -2)
    conv_state = jnp.concatenate([conv_prev, u[:, :t_real]], axis=1)[:, -(CONV_W - 1):]
    state = (c1, n1.reshape(bsz, N_HEADS, DH), m1[:, 0, :N_HEADS], conv_state,
             d_kv[:, :t_real].reshape(bsz, t_real, 2, HKV, DH))
    return [hc], o_d, state


def _run_group(p, x, mod, t_real, past, page_table, q0, per_row):
    bsz, t, d = x.shape
    m = bsz * t
    tm = min(512, m)
    tmm = min(1024, m)
    tiles = max(t // tm, 1)
    tiles_mlp = max(t // tmm, 1)
    x2d = x.reshape(m, d)
    n_layers = p["mlp_w1"].shape[0]
    ev_states, od_states = [], []
    for l in range(n_layers):
        i = l // 2
        if per_row:
            mod_l = jnp.moveaxis(jnp.repeat(mod[l], t, axis=0), 1, 0)
        else:
            mod_l = mod[l]
        if l % 2 == 0:
            outs = k_in(x2d, mod_l, p["norm1_g"][l], p["ev_w"][i], p["ev_b"][i], _EVEN_GROUPS, tm, tiles, per_row)
            a_list, b_o, st = _even_mixer(p, i, outs, bsz, t, t_real, past, page_table, q0)
            ev_states.append(st)
            w_out = p["ev_w_out"][i]
        else:
            outs = k_in(x2d, mod_l, p["norm1_g"][l], p["od_w"][i], p["od_b"][i], _ODD_GROUPS, tm, tiles, per_row)
            a_list, b_o, st = _odd_mixer(p, i, outs, bsz, t, t_real, past, page_table, q0)
            od_states.append(st)
            w_out = p["od_w_out"][i]
        x2d = k_out([a.reshape(m, -1) for a in a_list], b_o.reshape(m, -1), x2d, mod_l, w_out, tm, tiles, per_row)
        x2d = k_mlp(x2d, mod_l, p["norm2_g"][l], p["final_g"], p["mlp_w1"][l], p["mlp_w2"][l], tmm,
                    min(1024, p["mlp_w1"].shape[2]), tiles_mlp, per_row, final=(l == n_layers - 1))
    stack = lambda states: tuple(jnp.stack(a) for a in zip(*states))
    return x2d.reshape(bsz, t, d)[:, :t_real], stack(ev_states), stack(od_states)


def kernel(x_prompt, x_sample, cache_nsa_kv, state_nsa_win, cache_dsa_kv, cache_dsa_idx, state_mlstm_c, state_mlstm_n, state_mlstm_m, state_mlstm_conv, cache_moba_kv, page_table, c_prompt, c_sample, ada_w, ada_b, norm1_g, norm2_g, ev_w_in, ev_b_in, ev_w_out, nsa_cmp_pe, nsa_cmp_w, od_w_in, od_b_in, od_w_out, ml_conv_w, ml_conv_b, ml_wq, ml_wk, ml_f_bias, ml_norm_g, mlp_w1, mlp_w2, final_g):
    n_even, n_odd = ev_w_in.shape[0], od_w_in.shape[0]
    d = x_prompt.shape[-1]
    bp, bs = x_prompt.shape[0], x_sample.shape[0]
    t_dec = x_sample.shape[1]
    t_pad = -(-t_dec // 8) * 8
    page = cache_nsa_kv.shape[2]

    ev_w, ev_b = _regroup_columns(ev_w_in, ev_b_in, _EVEN_SRC)
    od_w, od_b = _regroup_columns(od_w_in, od_b_in, _ODD_SRC)
    d_c = N_HEADS * DH
    p = dict(
        norm1_g=norm1_g, norm2_g=norm2_g, final_g=final_g,
        ev_w=ev_w, ev_b=ev_b, od_w=od_w, od_b=od_b,
        ev_w_out=ev_w_out.astype(BF16), od_w_out=od_w_out.astype(BF16),
        mlp_w1=mlp_w1.astype(BF16), mlp_w2=mlp_w2.astype(BF16),
        pe2=jnp.tile(nsa_cmp_pe, (1, 1, 1, HKV)),
        cmp_w=_block_diag(jnp.broadcast_to(nsa_cmp_w[:, :, :, None], nsa_cmp_w.shape[:3] + (HKV, DH, DH))).astype(BF16),
        conv_w=ml_conv_w, conv_b=ml_conv_b.reshape(n_odd, 1, d_c),
        wq_bd=_block_diag(ml_wq).astype(BF16), wk_bd=_block_diag(ml_wk).astype(BF16),
        fb_row=jnp.pad(ml_f_bias, ((0, 0), (N_HEADS, LANES - 2 * N_HEADS))).reshape(n_odd, 1, LANES),
        norm_g=ml_norm_g.reshape(n_odd, 1, d_c),
    )
    n_rows = -(-(bs + bp) // 8) * 8
    c_all = jnp.pad(jnp.concatenate([c_sample, c_prompt], axis=0), ((0, n_rows - bs - bp), (0, 0)))
    mod = ada_mod(c_all, ada_w.astype(BF16), ada_b).reshape(ada_w.shape[0], n_rows, 6, d)
    mod_s, mod_p = mod[:, :bs], mod[:, bs:bs + bp]

    y_p, ev_p, od_p = _run_group(p, x_prompt, mod_p, x_prompt.shape[1], None, None, 0, False)

    def feat_major(a):
        nd = a.ndim
        a = jnp.transpose(a, (0, 1) + tuple(range(3, nd)) + (2,))
        return a.reshape(a.shape[:2] + (-1, a.shape[-1]))
    past = dict(nsa=feat_major(cache_nsa_kv), win=feat_major(state_nsa_win), dsa=feat_major(cache_dsa_kv),
                idx=feat_major(cache_dsa_idx), moba=feat_major(cache_moba_kv),
                c=state_mlstm_c, n=state_mlstm_n, m=state_mlstm_m, conv=state_mlstm_conv)
    x_s = _pad_rows(x_sample, t_pad)
    y_s, ev_s, od_s = _run_group(p, x_s, mod_s, t_dec, past, page_table, page_table.shape[1] * page, True)

    nsa_kv_p, nsa_win_p, dsa_kv_p, dsa_idx_p = ev_p
    nsa_kv_s, nsa_win_s, dsa_kv_s, dsa_idx_s = ev_s
    ml_c_p, ml_n_p, ml_m_p, ml_conv_p, moba_kv_p = od_p
    ml_c_s, ml_n_s, ml_m_s, ml_conv_s, moba_kv_s = od_s
    return (y_p, y_s,
            nsa_kv_p, nsa_kv_s, nsa_win_p, nsa_win_s, dsa_kv_p, dsa_kv_s, dsa_idx_p, dsa_idx_s,
            ml_c_p, ml_c_s, ml_n_p, ml_n_s, ml_m_p, ml_m_s, ml_conv_p, ml_conv_s, moba_kv_p, moba_kv_s)
```
